```python
import math
import jax
import jax.numpy as jnp
from jax import lax
import numpy as np

D_MODEL = 1024
BATCH = 8
SEQ = 2048
DEPTH = 4
DEC_BATCH = 32
DEC_SEQ = 8
PAST_LEN = 8192
PAGE_SIZE = 128

N_EVEN = (DEPTH + 1) // 2
N_ODD = DEPTH // 2
RMS_EPS = 1e-6
D_FF = 4 * D_MODEL

D_A = D_MODEL // 2
DH_A = 64
H_A = D_A // DH_A
BRANCHES = ((128, 1), (512, 4), (2048, 16))
WINDOW = max(w for w, _ in BRANCHES)
ATTN_QBLK = 128

D_B = D_MODEL - D_A
CH_B = 16
G_B = D_B // CH_B
P_B = 64

D_INNER = 2 * D_MODEL
P_C = 64
H_C = D_INNER // P_C
G_C = 4
J_C = H_C // G_C
N_C = 128
CONV_K = 4
CONV_DIM = D_INNER + 2 * G_C * N_C
SSD_CHUNK = 128

kernel_name = 'hybrid_dilated_s5_ssd_decoder_step'


def _rmsnorm(x, w):
    xf = x.astype(jnp.float32)
    y = xf * lax.rsqrt(jnp.mean(xf * xf, axis=-1, keepdims=True) + RMS_EPS)
    return (y * w.astype(jnp.float32)).astype(x.dtype)


def _sq_relu_mlp(h, w_up, w_down):
    a = jax.nn.relu(h @ w_up)
    return (a * a) @ w_down


def _alibi_slopes(n):
    return jnp.asarray(np.power(2.0, -8.0 * np.arange(1, n + 1) / n), dtype=jnp.float32)


def _dilated_block(q, pos_q, k_all, v_all, base):
    n_keys = k_all.shape[1]
    slopes = _alibi_slopes(H_A)
    qf = q.astype(jnp.float32) * (DH_A ** -0.5)
    outs, lses = [], []
    for win, dil in BRANCHES:
        offs = jnp.arange(win // dil + 1, dtype=jnp.int32) * dil
        kpos = pos_q[:, None] - offs[None, :]
        idx = kpos - base
        valid = (kpos >= 0) & (idx >= 0)
        idx = jnp.clip(idx, 0, n_keys - 1)
        kg = k_all[:, idx].astype(jnp.float32)
        vg = v_all[:, idx].astype(jnp.float32)
        s = jnp.einsum('bthd,btkhd->bhtk', qf, kg) - slopes[:, None, None] * offs.astype(jnp.float32)
        s = jnp.where(valid[None, None], s, -jnp.inf)
        lse = jax.nn.logsumexp(s, axis=-1)
        prob = jnp.exp(s - lse[..., None])
        outs.append(jnp.einsum('bhtk,btkhd->bthd', prob, vg))
        lses.append(lse)
    wts = jax.nn.softmax(jnp.stack(lses), axis=0)
    wts = jnp.swapaxes(wts, 2, 3)[..., None]
    return jnp.sum(jnp.stack(outs) * wts, axis=0)


def _dilated_attention(q, pos_q, k_all, v_all, base):
    b, t = q.shape[:2]
    qb = ATTN_QBLK if t % ATTN_QBLK == 0 else t
    nb = t // qb
    if nb == 1:
        return _dilated_block(q, pos_q, k_all, v_all, base)
    qs = jnp.swapaxes(q.reshape(b, nb, qb, H_A, DH_A), 0, 1)
    ps = pos_q.reshape(nb, qb)
    out = lax.map(lambda args: _dilated_block(args[0], args[1], k_all, v_all, base), (qs, ps))
    return jnp.swapaxes(out, 0, 1).reshape(b, t, H_A, DH_A)


def _s5(u, h0, lam_re, lam_im, log_dt, b_re, b_im, c_re, c_im, d_skip, w_glu, b_glu):
    f32 = jnp.float32
    bsz, l, _ = u.shape
    uf = u.astype(f32).reshape(bsz, l, G_B, CH_B)
    lam = lax.complex(lam_re.astype(f32), lam_im.astype(f32))
    dt = jnp.exp(log_dt.astype(f32))[:, None]
    lam_bar = jnp.exp(lam * dt)
    b_bar = ((lam_bar - 1.0) / lam)[..., None] * lax.complex(b_re.astype(f32), b_im.astype(f32))
    c_mat = lax.complex(c_re.astype(f32), c_im.astype(f32))
    bu = jnp.einsum('blgc,gpc->blgp', uf.astype(jnp.complex64), b_bar)
    a = jnp.broadcast_to(lam_bar, bu.shape)

    def combine(e1, e2):
        return e2[0] * e1[0], e2[0] * e1[1] + e2[1]

    a_cum, h = lax.associative_scan(combine, (a, bu), axis=1)
    if h0 is not None:
        h0f = h0.astype(f32)
        h = h + a_cum * lax.complex(h0f[..., 0], h0f[..., 1])[:, None]
    y = jnp.einsum('blgp,gcp->blgc', h, c_mat).real + d_skip.astype(f32) * uf
    g = jax.nn.gelu(y.reshape(bsz, l, D_B))
    out = g * jax.nn.sigmoid(g @ w_glu.astype(f32) + b_glu.astype(f32))
    h_last = h[:, -1]
    return out.astype(u.dtype), jnp.stack([h_last.real, h_last.imag], axis=-1).astype(u.dtype)


def _even_mixer(h, k_past, v_past, s5_h0, pos0, w_in, w_out, s5p):
    bsz, l, _ = h.shape
    proj = h @ w_in
    q = proj[..., :D_A].reshape(bsz, l, H_A, DH_A)
    k = proj[..., D_A:2 * D_A].reshape(bsz, l, H_A, DH_A)
    v = proj[..., 2 * D_A:3 * D_A].reshape(bsz, l, H_A, DH_A)
    u = proj[..., 3 * D_A:]
    if k_past is None:
        k_all, v_all, base = k, v, 0
    else:
        k_all = jnp.concatenate([k_past.astype(k.dtype), k], axis=1)
        v_all = jnp.concatenate([v_past.astype(v.dtype), v], axis=1)
        base = pos0 - k_past.shape[1]
    pos_q = pos0 + jnp.arange(l, dtype=jnp.int32)
    o_a = _dilated_attention(q, pos_q, k_all, v_all, base).reshape(bsz, l, D_A).astype(h.dtype)
    o_b, s5_state = _s5(u, s5_h0, *s5p)
    out = jnp.concatenate([o_a, o_b], axis=-1) @ w_out
    return out, k, v, s5_state


def _ssd(xdt, dA, bm, cm, h0, chunk):
    b, l = xdt.shape[:2]
    c = l // chunk
    X = xdt.reshape(b, c, chunk, G_C, J_C, P_C)
    A = dA.reshape(b, c, chunk, G_C, J_C)
    Bc = bm.reshape(b, c, chunk, G_C, N_C)
    Cc = cm.reshape(b, c, chunk, G_C, N_C)
    a_cs = jnp.cumsum(A, axis=2)
    tri = jnp.tril(jnp.ones((chunk, chunk), bool))
    seg = a_cs[:, :, :, None] - a_cs[:, :, None, :]
    decay = jnp.exp(jnp.where(tri[None, None, :, :, None, None], seg, -jnp.inf))
    cb = jnp.einsum('bclgn,bcsgn->bclsg', Cc, Bc)
    y_diag = jnp.einsum('bclsg,bclsgj,bcsgjp->bclgjp', cb, decay, X)
    decay_to_end = jnp.exp(a_cs[:, :, -1:] - a_cs)
    chunk_states = jnp.einsum('bcsgn,bcsgj,bcsgjp->bcgjpn', Bc, decay_to_end, X)
    states = jnp.concatenate([h0[:, None], chunk_states], axis=1)
    tot = jnp.pad(a_cs[:, :, -1], ((0, 0), (1, 0), (0, 0), (0, 0)))
    tot_cs = jnp.cumsum(tot, axis=1)
    tri_c = jnp.tril(jnp.ones((c + 1, c + 1), bool))
    seg_c = tot_cs[:, :, None] - tot_cs[:, None, :]
    decay_c = jnp.exp(jnp.where(tri_c[None, :, :, None, None], seg_c, -jnp.inf))
    states = jnp.einsum('bzcgj,bcgjpn->bzgjpn', decay_c, states)
    prev, h_last = states[:, :-1], states[:, -1]
    y_off = jnp.einsum('bclgn,bcgjpn,bclgj->bclgjp', Cc, prev, jnp.exp(a_cs))
    return (y_diag + y_off).reshape(b, l, G_C, J_C, P_C), h_last


def _mamba2(h, conv_buf, ssm_h0, w_in, conv_w, conv_b, dt_bias, a_log, d_skip, gnorm_w, w_out):
    f32 = jnp.float32
    bsz, l, _ = h.shape
    zxbcdt = h @ w_in
    z = zxbcdt[..., :D_INNER]
    xbc = zxbcdt[..., D_INNER:D_INNER + CONV_DIM]
    dt = zxbcdt[..., D_INNER + CONV_DIM:]
    if conv_buf is None:
        left = jnp.zeros((bsz, CONV_K - 1, CONV_DIM), xbc.dtype)
    else:
        left = conv_buf.astype(xbc.dtype)
    xbc_full = jnp.concatenate([left, xbc], axis=1)
    conv = lax.conv_general_dilated(xbc_full, conv_w[:, None, :].astype(xbc.dtype), window_strides=(1,), padding='VALID', dimension_numbers=('NWC', 'WIO', 'NWC'), feature_group_count=CONV_DIM)
    xbc_c = jax.nn.silu(conv.astype(f32) + conv_b.astype(f32))
    x = xbc_c[..., :D_INNER].reshape(bsz, l, G_C, J_C, P_C)
    bm = xbc_c[..., D_INNER:D_INNER + G_C * N_C].reshape(bsz, l, G_C, N_C)
    cm = xbc_c[..., D_INNER + G_C * N_C:].reshape(bsz, l, G_C, N_C)
    dt = jax.nn.softplus(dt.astype(f32) + dt_bias.astype(f32)).reshape(bsz, l, G_C, J_C)
    a = -jnp.exp(a_log.astype(f32)).reshape(G_C, J_C)
    if ssm_h0 is None:
        h0 = jnp.zeros((bsz, G_C, J_C, P_C, N_C), f32)
    else:
        h0 = ssm_h0.astype(f32).reshape(bsz, G_C, J_C, P_C, N_C)
    chunk = SSD_CHUNK if l % SSD_CHUNK == 0 else l
    y, h_last = _ssd(x * dt[..., None], dt * a, bm, cm, h0, chunk)
    y = (y + d_skip.astype(f32).reshape(G_C, J_C)[..., None] * x).reshape(bsz, l, D_INNER)
    g = (y * jax.nn.silu(z.astype(f32))).reshape(bsz, l, G_C, D_INNER // G_C)
    g = g * lax.rsqrt(jnp.mean(g * g, axis=-1, keepdims=True) + RMS_EPS)
    g = g.reshape(bsz, l, D_INNER) * gnorm_w.astype(f32)
    out = g.astype(h.dtype) @ w_out
    return out, xbc_full[:, -(CONV_K - 1):], h_last.reshape(bsz, H_C, P_C, N_C).astype(h.dtype)


def _trunk(x, pos0, p, cache_k, cache_v, state_s5, state_conv, state_ssm):
    new_k, new_v, new_s5, new_conv, new_ssm = [], [], [], [], []
    for i in range(DEPTH):
        j = i // 2
        h = _rmsnorm(x, p['norm_mix_pre'][i])
        if i % 2 == 0:
            s5p = (p['s5_lambda_re'][j], p['s5_lambda_im'][j], p['s5_log_dt'][j], p['s5_b_re'][j], p['s5_b_im'][j], p['s5_c_re'][j], p['s5_c_im'][j], p['s5_d'][j], p['s5_w_glu'][j], p['s5_b_glu'][j])
            m, k_new, v_new, s5_st = _even_mixer(h, None if cache_k is None else cache_k[j], None if cache_v is None else cache_v[j], None if state_s5 is None else state_s5[j], pos0, p['w_in_even'][j], p['w_out_even'][j], s5p)
            if cache_k is None:
                keep = min(WINDOW, k_new.shape[1])
                k_new, v_new = k_new[:, -keep:], v_new[:, -keep:]
            new_k.append(k_new)
            new_v.append(v_new)
            new_s5.append(s5_st)
        else:
            m, conv_st, ssm_st = _mamba2(h, None if state_conv is None else state_conv[j], None if state_ssm is None else state_ssm[j], p['w_in_odd'][j], p['conv_w'][j], p['conv_b'][j], p['dt_bias'][j], p['a_log'][j], p['d_skip'][j], p['gnorm_w'][j], p['w_out_odd'][j])
            new_conv.append(conv_st)
            new_ssm.append(ssm_st)
        x = x + _rmsnorm(m, p['norm_mix_post'][i])
        h = _rmsnorm(x, p['norm_mlp_pre'][i])
        x = x + _rmsnorm(_sq_relu_mlp(h, p['w_mlp_up'][i], p['w_mlp_down'][i]), p['norm_mlp_post'][i])
    return x, jnp.stack(new_k), jnp.stack(new_v), jnp.stack(new_s5), jnp.stack(new_conv), jnp.stack(new_ssm)


def setup_inputs(seed: int = 0) -> dict:
    key = jax.random.key(seed)
    keys = iter(jax.random.split(key, 48))
    f32 = jnp.float32

    def nrm(shape, scale=1.0):
        return scale * jax.random.normal(next(keys), shape, f32)

    def unif(shape, lo, hi):
        return jax.random.uniform(next(keys), shape, f32, lo, hi)

    w_buf = min(WINDOW, PAST_LEN)
    dt0 = jnp.exp(unif((N_ODD, H_C), math.log(1e-3), math.log(1e-1)))
    return {
        'x_prompt': nrm((BATCH, SEQ, D_MODEL)),
        'x_sample': nrm((DEC_BATCH, DEC_SEQ, D_MODEL)),
        'cache_k': nrm((N_EVEN, DEC_BATCH, w_buf, H_A, DH_A)),
        'cache_v': nrm((N_EVEN, DEC_BATCH, w_buf, H_A, DH_A)),
        'state_s5': nrm((N_EVEN, DEC_BATCH, G_B, P_B, 2), 0.1),
        'state_conv': nrm((N_ODD, DEC_BATCH, CONV_K - 1, CONV_DIM)),
        'state_ssm': nrm((N_ODD, DEC_BATCH, H_C, P_C, N_C), 0.1),
        'norm_mix_pre': 1.0 + nrm((DEPTH, D_MODEL), 0.05),
        'norm_mix_post': 1.0 + nrm((DEPTH, D_MODEL), 0.05),
        'norm_mlp_pre': 1.0 + nrm((DEPTH, D_MODEL), 0.05),
        'norm_mlp_post': 1.0 + nrm((DEPTH, D_MODEL), 0.05),
        'w_mlp_up': nrm((DEPTH, D_MODEL, D_FF), D_MODEL ** -0.5),
        'w_mlp_down': nrm((DEPTH, D_FF, D_MODEL), D_FF ** -0.5),
        'w_in_even': nrm((N_EVEN, D_MODEL, 3 * D_A + D_B), D_MODEL ** -0.5),
        'w_out_even': nrm((N_EVEN, D_A + D_B, D_MODEL), (D_A + D_B) ** -0.5),
        's5_lambda_re': -0.5 + nrm((N_EVEN, G_B, P_B), 0.01),
        's5_lambda_im': jnp.pi * jnp.arange(P_B, dtype=f32) + nrm((N_EVEN, G_B, P_B), 0.01),
        's5_log_dt': unif((N_EVEN, G_B), math.log(1e-3), math.log(1e-1)),
        's5_b_re': nrm((N_EVEN, G_B, P_B, CH_B), (2 * CH_B) ** -0.5),
        's5_b_im': nrm((N_EVEN, G_B, P_B, CH_B), (2 * CH_B) ** -0.5),
        's5_c_re': nrm((N_EVEN, G_B, CH_B, P_B), (2 * P_B) ** -0.5),
        's5_c_im': nrm((N_EVEN, G_B, CH_B, P_B), (2 * P_B) ** -0.5),
        's5_d': nrm((N_EVEN, G_B, CH_B)),
        's5_w_glu': nrm((N_EVEN, D_B, D_B), D_B ** -0.5),
        's5_b_glu': nrm((N_EVEN, D_B), 0.01),
        'w_in_odd': nrm((N_ODD, D_MODEL, D_INNER + CONV_DIM + H_C), D_MODEL ** -0.5),
        'conv_w': nrm((N_ODD, CONV_K, CONV_DIM), 0.5),
        'conv_b': nrm((N_ODD, CONV_DIM), 0.01),
        'dt_bias': dt0 + jnp.log(-jnp.expm1(-dt0)),
        'a_log': jnp.log(unif((N_ODD, H_C), 1.0, 16.0)),
        'd_skip': 1.0 + nrm((N_ODD, H_C), 0.01),
        'gnorm_w': 1.0 + nrm((N_ODD, D_INNER), 0.05),
        'w_out_odd': nrm((N_ODD, D_INNER, D_MODEL), D_INNER ** -0.5),
    }


def reference(x_prompt, x_sample, cache_k, cache_v, state_s5, state_conv, state_ssm,
              norm_mix_pre, norm_mix_post, norm_mlp_pre, norm_mlp_post, w_mlp_up, w_mlp_down,
              w_in_even, w_out_even, s5_lambda_re, s5_lambda_im, s5_log_dt, s5_b_re, s5_b_im,
              s5_c_re, s5_c_im, s5_d, s5_w_glu, s5_b_glu,
              w_in_odd, conv_w, conv_b, dt_bias, a_log, d_skip, gnorm_w, w_out_odd):
    p = dict(norm_mix_pre=norm_mix_pre, norm_mix_post=norm_mix_post, norm_mlp_pre=norm_mlp_pre,
             norm_mlp_post=norm_mlp_post, w_mlp_up=w_mlp_up, w_mlp_down=w_mlp_down,
             w_in_even=w_in_even, w_out_even=w_out_even, s5_lambda_re=s5_lambda_re,
             s5_lambda_im=s5_lambda_im, s5_log_dt=s5_log_dt, s5_b_re=s5_b_re, s5_b_im=s5_b_im,
             s5_c_re=s5_c_re, s5_c_im=s5_c_im, s5_d=s5_d, s5_w_glu=s5_w_glu, s5_b_glu=s5_b_glu,
             w_in_odd=w_in_odd, conv_w=conv_w, conv_b=conv_b, dt_bias=dt_bias, a_log=a_log,
             d_skip=d_skip, gnorm_w=gnorm_w, w_out_odd=w_out_odd)
    y_prompt, k_p, v_p, s5_p, conv_p, ssm_p = _trunk(x_prompt, 0, p, None, None, None, None, None)
    y_sample, k_s, v_s, s5_s, conv_s, ssm_s = _trunk(x_sample, PAST_LEN, p, cache_k, cache_v, state_s5, state_conv, state_ssm)
    return (y_prompt, y_sample, k_p, v_p, s5_p, conv_p, ssm_p, k_s, v_s, s5_s, conv_s, ssm_s)
```

```python
import functools
import math

import numpy as np
import jax
import jax.numpy as jnp
from jax import lax
from jax.experimental import pallas as pl
from jax.experimental.pallas import tpu as pltpu

F32 = jnp.float32
BF16 = jnp.bfloat16

D_MODEL = 1024
DEPTH = 4
PAST_LEN = 8192
RMS_EPS = 1e-6
D_FF = 4 * D_MODEL
D_A = D_MODEL // 2
DH_A = 64
H_A = D_A // DH_A
BRANCHES = ((128, 1), (512, 4), (2048, 16))
WINDOW = 2048
D_B = D_MODEL - D_A
CH_B = 16
G_B = D_B // CH_B
P_B = 64
S5_N = G_B * P_B
D_INNER = 2 * D_MODEL
P_C = 64
H_C = D_INNER // P_C
G_C = 4
J_C = H_C // G_C
N_C = 128
CONV_K = 4
CONV_DIM = D_INNER + 2 * G_C * N_C
SSD_CHUNK = 128

LANES = 128
SUBLANES = 8
VMEM_LIMIT = 56 * 1024 * 1024
NEG_BIG = -1e30


def _act_dtype(rows):
    return BF16 if rows % (2 * SUBLANES) == 0 else F32


def _cparams(*sem):
    return pltpu.CompilerParams(dimension_semantics=sem, vmem_limit_bytes=VMEM_LIMIT)


def _rms(x, w):
    ms = jnp.mean(x * x, axis=-1, keepdims=True)
    return x * lax.rsqrt(ms + RMS_EPS) * w


def _const_spec(shape):
    nd = len(shape)
    return pl.BlockSpec(shape, lambda *_: (0,) * nd)


def _norm_proj_body(x_ref, wn_ref, w_ref, *out_refs, splits, scales):
    h = _rms(x_ref[...], wn_ref[...]).astype(BF16)
    off = 0
    for o_ref, n, sc in zip(out_refs, splits, scales):
        r = jnp.dot(h, w_ref[:, off:off + n], preferred_element_type=F32)
        if sc != 1.0:
            r = r * sc
        o_ref[...] = r.astype(o_ref.dtype)
        off += n


def _norm_proj(x2, wn, w, splits, scales, tm):
    m = x2.shape[0]
    n_total = w.shape[1]
    assert sum(splits) == n_total and m % tm == 0
    body = functools.partial(_norm_proj_body, splits=splits, scales=scales)
    return pl.pallas_call(
        body,
        grid=(m // tm,),
        in_specs=[pl.BlockSpec((tm, D_MODEL), lambda i: (i, 0)),
                  _const_spec((1, D_MODEL)),
                  _const_spec((D_MODEL, n_total))],
        out_specs=[pl.BlockSpec((tm, n), lambda i: (i, 0)) for n in splits],
        out_shape=[jax.ShapeDtypeStruct((m, n), F32) for n in splits],
        compiler_params=_cparams("parallel"),
        name="norm_proj",
    )(x2, wn, w)


def _proj_mlp_body(*refs, n_in, tf):
    a_refs = refs[:n_in]
    w_refs = refs[n_in:2 * n_in]
    x_ref, wpost_ref, wpre_ref, wup_ref, wdown_ref, wpost2_ref, o_ref = refs[2 * n_in:]
    m = None
    for a_ref, w_ref in zip(a_refs, w_refs):
        t = jnp.dot(a_ref[...].astype(BF16), w_ref[...], preferred_element_type=F32)
        m = t if m is None else m + t
    x = x_ref[...] + _rms(m, wpost_ref[...])
    h = _rms(x, wpre_ref[...]).astype(BF16)
    acc = None
    for c in range(D_FF // tf):
        a = jnp.dot(h, wup_ref[:, c * tf:(c + 1) * tf], preferred_element_type=F32)
        a = jnp.maximum(a, 0.0)
        a = (a * a).astype(BF16)
        t = jnp.dot(a, wdown_ref[c * tf:(c + 1) * tf, :], preferred_element_type=F32)
        acc = t if acc is None else acc + t
    o_ref[...] = x + _rms(acc, wpost2_ref[...])


def _proj_mlp(acts, ws, x2, wpost, wpre, wup, wdown, wpost2, tm, tf=512):
    m = x2.shape[0]
    n_in = len(acts)
    body = functools.partial(_proj_mlp_body, n_in=n_in, tf=tf)
    in_specs = [pl.BlockSpec((tm, a.shape[1]), lambda i: (i, 0)) for a in acts]
    in_specs += [_const_spec(w.shape) for w in ws]
    in_specs += [pl.BlockSpec((tm, D_MODEL), lambda i: (i, 0)),
                 _const_spec((1, D_MODEL)), _const_spec((1, D_MODEL)),
                 _const_spec((D_MODEL, D_FF)), _const_spec((D_FF, D_MODEL)),
                 _const_spec((1, D_MODEL))]
    return pl.pallas_call(
        body,
        grid=(m // tm,),
        in_specs=in_specs,
        out_specs=pl.BlockSpec((tm, D_MODEL), lambda i: (i, 0)),
        out_shape=jax.ShapeDtypeStruct((m, D_MODEL), F32),
        compiler_params=_cparams("parallel"),
        name="proj_mlp",
    )(*acts, *ws, x2, wpost, wpre, wup, wdown, wpost2)


def _attn_prompt_body(slope_ref, q_ref, k_ref, v_ref, o_ref, *scr, seq):
    hp = pl.program_id(1)
    o_scr, m_scr, l_scr = scr[0:3], scr[3:6], scr[6:9]
    qb = LANES
    lane = lax.broadcasted_iota(jnp.int32, (qb, LANES), 1)
    head0 = lane < DH_A
    slopes = (slope_ref[2 * hp], slope_ref[2 * hp + 1])

    for bi, (win, dil) in enumerate(BRANCHES):
        n = seq // dil
        nqb = n // qb
        nk = 2 * qb if nqb >= 2 else qb
        band = win // dil
        rowi = lax.broadcasted_iota(jnp.int32, (qb, nk), 0)
        coli = lax.broadcasted_iota(jnp.int32, (qb, nk), 1)

        def rows(start, size, dil=dil):
            if dil == 1:
                return pl.ds(pl.multiple_of(start, LANES), size)
            return pl.ds(start, size, stride=dil)

        def unit(u, carry, bi=bi, dil=dil, n=n, nqb=nqb, nk=nk, band=band,
                 rowi=rowi, coli=coli, rows=rows):
            r = u // nqb
            i = u % nqb
            ks = jnp.clip(i * qb - qb, 0, n - nk)
            qstart = r + dil * qb * i
            kstart = r + dil * ks
            q = q_ref[rows(qstart, qb), :]
            k = k_ref[rows(kstart, nk), :].astype(BF16)
            v = v_ref[rows(kstart, nk), :].astype(BF16)
            d = (i * qb - ks) + rowi - coli
            valid = (d >= 0) & (d <= band)
            dist = (d * dil).astype(F32)
            res = []
            for hh in range(2):
                keep = head0 if hh == 0 else jnp.logical_not(head0)
                qm = jnp.where(keep, q, 0.0).astype(BF16)
                s = lax.dot_general(qm, k, (((1,), (1,)), ((), ())), preferred_element_type=F32)
                s = jnp.where(valid, s - slopes[hh] * dist, NEG_BIG)
                mx = jnp.max(s, axis=1, keepdims=True)
                p = jnp.exp(s - mx)
                den = jnp.sum(p, axis=1, keepdims=True)
                o = jnp.dot(p.astype(BF16), v, preferred_element_type=F32)
                res.append((o, mx, den))
            o_scr[bi][rows(qstart, qb), :] = jnp.where(head0, res[0][0], res[1][0])
            m_scr[bi][rows(qstart, qb), :] = jnp.where(head0, res[0][1], res[1][1])
            l_scr[bi][rows(qstart, qb), :] = jnp.where(head0, res[0][2], res[1][2])
            return carry

        lax.fori_loop(0, dil * nqb, unit, 0)

    cb = 256

    def combine(c, carry):
        sl = pl.ds(pl.multiple_of(c * cb, cb), cb)
        ms = [m_scr[b][sl, :] for b in range(3)]
        mx = jnp.maximum(jnp.maximum(ms[0], ms[1]), ms[2])
        num = None
        den = None
        for b in range(3):
            w = jnp.exp(ms[b] - mx)
            tn = o_scr[b][sl, :] * w
            td = l_scr[b][sl, :] * w
            num = tn if num is None else num + tn
            den = td if den is None else den + td
        o_ref[sl, :] = (num / den).astype(o_ref.dtype)
        return carry

    lax.fori_loop(0, seq // cb, combine, 0)


def _attn_prompt(q, k, v, slopes):
    b, seq, _ = q.shape
    assert seq % (16 * LANES) == 0
    blk = pl.BlockSpec((None, seq, LANES), lambda bb, hp: (bb, 0, hp))
    body = functools.partial(_attn_prompt_body, seq=seq)
    return pl.pallas_call(
        body,
        grid=(b, D_A // LANES),
        in_specs=[pl.BlockSpec(memory_space=pltpu.SMEM), blk, blk, blk],
        out_specs=blk,
        out_shape=jax.ShapeDtypeStruct((b, seq, D_A), BF16),
        scratch_shapes=[pltpu.VMEM((seq, LANES), F32) for _ in range(9)],
        compiler_params=_cparams("parallel", "parallel"),
        name="attn_prompt",
    )(slopes, q, k, v)


def _attn_sample_body(slope_ref, q_ref, kn_ref, vn_ref, kc_ref, vc_ref, o_ref, *, t_new, w_buf):
    hp = pl.program_id(1)
    lane = lax.broadcasted_iota(jnp.int32, (t_new, LANES), 1)
    head0 = lane < DH_A
    q = q_ref[...]
    qs = jnp.concatenate([jnp.where(head0, q, 0.0), jnp.where(head0, 0.0, q)], axis=0).astype(BF16)
    rows2 = 2 * t_new
    rsel = lax.broadcasted_iota(jnp.int32, (rows2, 1), 0)
    slope = jnp.where(rsel < t_new, slope_ref[2 * hp], slope_ref[2 * hp + 1])

    def scores(k_ref, ncols, d0):
        kk = k_ref[...].astype(BF16)
        s = lax.dot_general(qs, kk, (((1,), (1,)), ((), ())), preferred_element_type=F32)
        rowi = lax.broadcasted_iota(jnp.int32, (rows2, ncols), 0)
        coli = lax.broadcasted_iota(jnp.int32, (rows2, ncols), 1)
        t = jnp.where(rowi < t_new, rowi, rowi - t_new)
        d = d0 + t - coli
        mult = jnp.zeros((rows2, ncols), F32)
        for win, dil in BRANCHES:
            hit = (d >= 0) & (d <= win) & ((d & (dil - 1)) == 0)
            mult = mult + jnp.where(hit, 1.0, 0.0)
        s = jnp.where(mult > 0.0, s - slope * d.astype(F32), NEG_BIG)
        return s, mult

    s_c, mult_c = scores(kc_ref, w_buf, w_buf)
    s_n, mult_n = scores(kn_ref, t_new, 0)
    mx = jnp.maximum(jnp.max(s_c, axis=1, keepdims=True), jnp.max(s_n, axis=1, keepdims=True))
    p_c = mult_c * jnp.exp(s_c - mx)
    p_n = mult_n * jnp.exp(s_n - mx)
    den = jnp.sum(p_c, axis=1, keepdims=True) + jnp.sum(p_n, axis=1, keepdims=True)
    o = jnp.dot(p_c.astype(BF16), vc_ref[...].astype(BF16), preferred_element_type=F32)
    o = o + jnp.dot(p_n.astype(BF16), vn_ref[...].astype(BF16), preferred_element_type=F32)
    o = o / den
    o_ref[...] = jnp.where(head0, o[:t_new], o[t_new:]).astype(o_ref.dtype)


def _attn_sample(q, kn, vn, cache_k, cache_v, layer, slopes):
    b, t_new, _ = q.shape
    w_buf = cache_k.shape[2]
    assert w_buf >= WINDOW and t_new % SUBLANES == 0
    new_blk = pl.BlockSpec((None, t_new, LANES), lambda bb, hp: (bb, 0, hp))
    cache_blk = pl.BlockSpec((None, None, w_buf, LANES), lambda bb, hp: (layer, bb, 0, hp))
    body = functools.partial(_attn_sample_body, t_new=t_new, w_buf=w_buf)
    return pl.pallas_call(
        body,
        grid=(b, D_A // LANES),
        in_specs=[pl.BlockSpec(memory_space=pltpu.SMEM), new_blk, new_blk, new_blk, cache_blk, cache_blk],
        out_specs=new_blk,
        out_shape=jax.ShapeDtypeStruct((b, t_new, D_A), _act_dtype(t_new)),
        compiler_params=_cparams("parallel", "parallel"),
        name="attn_sample",
    )(slopes, q, kn, vn, cache_k, cache_v)


S5_UT = D_B // LANES
S5_TW = S5_N // S5_UT


def _gelu_tanh(x):
    c = math.sqrt(2.0 / math.pi)
    return 0.5 * x * (1.0 + jnp.tanh(c * (x + 0.044715 * (x * x * x))))


def _s5_body(u_ref, h0_ref, perm_ref, permt_ref, wb_ref, lam_ref, wc_ref, dskip_ref, wglu_ref, bglu_ref,
             o_ref, hl_ref, bu_scr, up_scr, st_scr, *, nb, tsub, nsub):
    step = pl.program_id(0)
    rsub = nb * tsub
    tstep = tsub * nsub

    @pl.when(step == 0)
    def _():
        st_scr[...] = h0_ref[...]

    for sb in range(nsub):
        u_nat = u_ref[:, sb * tsub:(sb + 1) * tsub, :].reshape(rsub, D_B)
        up_scr[sb * rsub:(sb + 1) * rsub, :] = jnp.dot(
            perm_ref[...], u_nat, preferred_element_type=F32, precision=lax.Precision.HIGHEST)
    for j in range(S5_UT):
        uj = up_scr[:, j * LANES:(j + 1) * LANES].astype(BF16)
        r = jnp.dot(uj, wb_ref[j], preferred_element_type=F32)
        bu_scr[:, j * S5_TW:(j + 1) * S5_TW] = r[:, :S5_TW]
        bu_scr[:, S5_N + j * S5_TW:S5_N + (j + 1) * S5_TW] = r[:, S5_TW:]

    tiles_per_pass = 4
    for pg in range(S5_N // LANES // tiles_per_pass):
        cols = [(pg * tiles_per_pass + i) * LANES for i in range(tiles_per_pass)]
        lr = [jnp.broadcast_to(lam_ref[0:1, c:c + LANES], (nb, LANES)) for c in cols]
        li = [jnp.broadcast_to(lam_ref[1:2, c:c + LANES], (nb, LANES)) for c in cols]
        init = tuple(st_scr[:, c:c + LANES] for c in cols) + tuple(st_scr[:, S5_N + c:S5_N + c + LANES] for c in cols)

        def tick(t, carry, cols=cols, lr=lr, li=li):
            row = pl.ds(pl.multiple_of(t * nb, nb), nb)
            hr, hi = carry[:tiles_per_pass], carry[tiles_per_pass:]
            nr, ni = [], []
            for i, c in enumerate(cols):
                br = bu_scr[row, c:c + LANES]
                bi = bu_scr[row, S5_N + c:S5_N + c + LANES]
                r_new = lr[i] * hr[i] - li[i] * hi[i] + br
                i_new = lr[i] * hi[i] + li[i] * hr[i] + bi
                bu_scr[row, c:c + LANES] = r_new
                bu_scr[row, S5_N + c:S5_N + c + LANES] = i_new
                nr.append(r_new)
                ni.append(i_new)
            return tuple(nr) + tuple(ni)

        fin = lax.fori_loop(0, tstep, tick, init)
        for i, c in enumerate(cols):
            st_scr[:, c:c + LANES] = fin[i]
            st_scr[:, S5_N + c:S5_N + c + LANES] = fin[tiles_per_pass + i]

    hl_ref[...] = st_scr[...]

    ys = []
    for j in range(S5_UT):
        hre = bu_scr[:, j * S5_TW:(j + 1) * S5_TW].astype(BF16)
        him = bu_scr[:, S5_N + j * S5_TW:S5_N + (j + 1) * S5_TW].astype(BF16)
        y = jnp.dot(hre, wc_ref[j, :S5_TW, :], preferred_element_type=F32)
        y = y + jnp.dot(him, wc_ref[j, S5_TW:, :], preferred_element_type=F32)
        ys.append(y)
    y = jnp.concatenate(ys, axis=1) + dskip_ref[...] * up_scr[...]
    g = _gelu_tanh(y)
    gate = jnp.dot(g.astype(BF16), wglu_ref[...], preferred_element_type=F32) + bglu_ref[...]
    out = (g * jax.nn.sigmoid(gate)).astype(BF16)
    for sb in range(nsub):
        o_nat = jnp.dot(permt_ref[...], out[sb * rsub:(sb + 1) * rsub, :], preferred_element_type=F32)
        o_ref[:, sb * tsub:(sb + 1) * tsub, :] = o_nat.reshape(nb, tsub, D_B).astype(o_ref.dtype)


def _s5_mixer(u, h0, prm, tsub, nsub):
    nb, seq, _ = u.shape
    tstep = tsub * nsub
    rsub = nb * tsub
    rows = nb * tstep
    assert seq % tstep == 0 and nb % SUBLANES == 0 and tsub % SUBLANES == 0
    perm = np.zeros((rsub, rsub), np.float32)
    for t in range(tsub):
        for bb in range(nb):
            perm[t * nb + bb, bb * tsub + t] = 1.0
    body = functools.partial(_s5_body, nb=nb, tsub=tsub, nsub=nsub)
    return pl.pallas_call(
        body,
        grid=(seq // tstep,),
        in_specs=[pl.BlockSpec((nb, tstep, D_B), lambda i: (0, i, 0)),
                  _const_spec((nb, 2 * S5_N)),
                  _const_spec((rsub, rsub)), _const_spec((rsub, rsub)),
                  _const_spec((S5_UT, LANES, 2 * S5_TW)),
                  _const_spec((2, S5_N)),
                  _const_spec((S5_UT, 2 * S5_TW, LANES)),
                  _const_spec((1, D_B)),
                  _const_spec((D_B, D_B)),
                  _const_spec((1, D_B))],
        out_specs=[pl.BlockSpec((nb, tstep, D_B), lambda i: (0, i, 0)),
                   _const_spec((nb, 2 * S5_N))],
        out_shape=[jax.ShapeDtypeStruct((nb, seq, D_B), _act_dtype(tsub)),
                   jax.ShapeDtypeStruct((nb, 2 * S5_N), F32)],
        scratch_shapes=[pltpu.VMEM((rows, 2 * S5_N), F32),
                        pltpu.VMEM((rows, D_B), F32),
                        pltpu.VMEM((nb, 2 * S5_N), F32)],
        compiler_params=_cparams("arbitrary"),
        name="s5_mixer",
    )(u, h0, jnp.asarray(perm, F32), jnp.asarray(perm.T, BF16), prm["wb"], prm["lam"], prm["wc"],
      prm["dskip"], prm["wglu"], prm["bglu"])


def _s5_params(lam_re, lam_im, log_dt, b_re, b_im, c_re, c_im, d_skip, w_glu, b_glu):
    lam = lax.complex(lam_re.astype(F32), lam_im.astype(F32))
    dt = jnp.exp(log_dt.astype(F32))[:, None]
    lam_bar = jnp.exp(lam * dt)
    b_bar = ((lam_bar - 1.0) / lam)[..., None] * lax.complex(b_re.astype(F32), b_im.astype(F32))
    gpt = LANES // CH_B
    eye = jnp.eye(gpt, dtype=F32)

    def b_blocks(part):
        x = part.reshape(S5_UT, gpt, P_B, CH_B)
        x = jnp.einsum("jgpc,gh->jgchp", x, eye)
        return x.reshape(S5_UT, gpt * CH_B, gpt * P_B)

    def c_blocks(part):
        x = part.reshape(S5_UT, gpt, CH_B, P_B)
        x = jnp.einsum("jgcp,gh->jgphc", x, eye)
        return x.reshape(S5_UT, gpt * P_B, gpt * CH_B)

    wb = jnp.concatenate([b_blocks(b_bar.real), b_blocks(b_bar.imag)], axis=2).astype(BF16)
    wc = jnp.concatenate([c_blocks(c_re.astype(F32)), -c_blocks(c_im.astype(F32))], axis=1).astype(BF16)
    lam2 = jnp.stack([lam_bar.real.reshape(S5_N), lam_bar.imag.reshape(S5_N)])
    return dict(wb=wb, wc=wc, lam=lam2, dskip=d_skip.astype(F32).reshape(1, D_B),
                wglu=w_glu.astype(BF16), bglu=b_glu.astype(F32).reshape(1, D_B))


HIST = SUBLANES
XB0 = D_INNER
XC0 = D_INNER + G_C * N_C
GW = D_INNER // G_C


def _exact_expand(a, e_ref):
    a1 = a.astype(BF16)
    r1 = a - a1.astype(F32)
    a2 = r1.astype(BF16)
    a3 = (r1 - a2.astype(F32)).astype(BF16)
    e = e_ref[...]
    out = jnp.dot(a1, e, preferred_element_type=F32)
    out = out + jnp.dot(a2, e, preferred_element_type=F32)
    return out + jnp.dot(a3, e, preferred_element_type=F32)


def _ssd_body(z_ref, xbc_ref, dt_ref, conv0_ref, ssm0_ref, convw_ref, convb_ref, dtb_ref, a_ref, dexp_ref,
              gw_ref, e_ref, y_ref, convst_ref, hl_ref, xpad, state, *, lc):
    ch = pl.program_id(1)
    cl = SSD_CHUNK

    @pl.when(ch == 0)
    def _():
        xpad[0:HIST, :] = conv0_ref[...]
        state[...] = ssm0_ref[...]

    if lc < cl:
        xpad[HIST:, :] = jnp.zeros((cl, CONV_DIM), F32)
    xpad[HIST:HIST + lc, :] = xbc_ref[...]
    conv = convb_ref[...] + convw_ref[CONV_K - 1:CONV_K, :] * xpad[HIST:HIST + cl, :]
    for kk in range(CONV_K - 1):
        sh = CONV_K - 1 - kk
        conv = conv + convw_ref[kk:kk + 1, :] * xpad[HIST - sh:HIST - sh + cl, :]
    xs = conv * jax.nn.sigmoid(conv)
    tail = xpad[lc:lc + HIST, :]
    xpad[0:HIST, :] = tail
    convst_ref[...] = tail

    dt = jax.nn.softplus(dt_ref[...] + dtb_ref[...])
    if lc < cl:
        dt = jnp.concatenate([dt, jnp.zeros((cl - lc, LANES), F32)], axis=0)
    da = dt * a_ref[...]
    ri = lax.broadcasted_iota(jnp.int32, (cl, cl), 0)
    ci = lax.broadcasted_iota(jnp.int32, (cl, cl), 1)
    tri = ri >= ci
    a_cs = jnp.dot(jnp.where(tri, 1.0, 0.0), da, preferred_element_type=F32, precision=lax.Precision.HIGHEST)
    a_cs_t = a_cs.T
    dt_t = dt.T
    e_cs = jnp.exp(a_cs)
    tot = a_cs[cl - 1:cl, :]
    w_end = jnp.exp(tot - a_cs) * dt
    scale_x = _exact_expand(e_cs, e_ref)
    wend_x = _exact_expand(w_end, e_ref)

    x = xs[:, :D_INNER]
    xw = (x * wend_x).astype(BF16)
    lane = lax.broadcasted_iota(jnp.int32, (cl, LANES), 1)
    head0 = lane < P_C
    y_parts = []
    for g in range(G_C):
        bg = xs[:, XB0 + g * N_C:XB0 + (g + 1) * N_C].astype(BF16)
        cg = xs[:, XC0 + g * N_C:XC0 + (g + 1) * N_C].astype(BF16)
        cb = lax.dot_general(cg, bg, (((1,), (1,)), ((), ())), preferred_element_type=F32)
        prev = state[g * GW:(g + 1) * GW, :]
        y_off = lax.dot_general(cg, prev.astype(BF16), (((1,), (1,)), ((), ())), preferred_element_type=F32)
        for jp in range(J_C // 2):
            ms = []
            for hh in range(2):
                h = g * J_C + 2 * jp + hh
                seg = a_cs[:, h:h + 1] - a_cs_t[h:h + 1, :]
                dec = jnp.exp(jnp.where(tri, seg, NEG_BIG))
                ms.append((cb * dec * dt_t[h:h + 1, :]).astype(BF16))
            c0 = g * GW + jp * LANES
            xp = x[:, c0:c0 + LANES].astype(BF16)
            r = jnp.dot(jnp.concatenate(ms, axis=0), xp, preferred_element_type=F32)
            yd = jnp.where(head0, r[:cl], r[cl:])
            y_parts.append(yd + y_off[:, jp * LANES:(jp + 1) * LANES] * scale_x[:, c0:c0 + LANES])
        new = lax.dot_general(xw[:, g * GW:(g + 1) * GW], bg, (((0,), (0,)), ((), ())),
                              preferred_element_type=F32)
        for j in range(J_C):
            h = g * J_C + j
            r0 = g * GW + j * P_C
            state[r0:r0 + P_C, :] = state[r0:r0 + P_C, :] * jnp.exp(tot[:, h:h + 1]) + new[j * P_C:(j + 1) * P_C, :]
    hl_ref[...] = state[...]

    y = jnp.concatenate(y_parts, axis=1) + dexp_ref[...] * x
    y = y[:lc] if lc < cl else y
    z = z_ref[...]
    gt = y * (z * jax.nn.sigmoid(z))
    outs = []
    for g in range(G_C):
        gg = gt[:, g * GW:(g + 1) * GW]
        ms = jnp.mean(gg * gg, axis=-1, keepdims=True)
        outs.append(gg * lax.rsqrt(ms + RMS_EPS))
    y_ref[...] = (jnp.concatenate(outs, axis=1) * gw_ref[...]).astype(y_ref.dtype)


def _ssd_mixer(z, xbc, dtr, conv0, ssm0, prm):
    b, seq, _ = z.shape
    lc = SSD_CHUNK if seq % SSD_CHUNK == 0 else seq
    assert lc % SUBLANES == 0 and CONV_K - 1 <= lc <= SSD_CHUNK
    nch = seq // lc
    body = functools.partial(_ssd_body, lc=lc)

    def row_blk(n):
        return pl.BlockSpec((None, lc, n), lambda bb, c: (bb, c, 0))

    def per_b(r, n):
        return pl.BlockSpec((None, r, n), lambda bb, c: (bb, 0, 0))

    return pl.pallas_call(
        body,
        grid=(b, nch),
        in_specs=[row_blk(D_INNER), row_blk(CONV_DIM), row_blk(LANES),
                  per_b(HIST, CONV_DIM), per_b(D_INNER, N_C),
                  _const_spec((CONV_K, CONV_DIM)), _const_spec((1, CONV_DIM)),
                  _const_spec((1, LANES)), _const_spec((1, LANES)),
                  _const_spec((1, D_INNER)), _const_spec((1, D_INNER)),
                  _const_spec((LANES, D_INNER))],
        out_specs=[row_blk(D_INNER), per_b(HIST, CONV_DIM), per_b(D_INNER, N_C)],
        out_shape=[jax.ShapeDtypeStruct((b, seq, D_INNER), _act_dtype(lc)),
                   jax.ShapeDtypeStruct((b, HIST, CONV_DIM), F32),
                   jax.ShapeDtypeStruct((b, D_INNER, N_C), F32)],
        scratch_shapes=[pltpu.VMEM((HIST + SSD_CHUNK, CONV_DIM), F32),
                        pltpu.VMEM((D_INNER, N_C), F32)],
        compiler_params=_cparams("parallel", "arbitrary"),
        name="ssd_mixer",
    )(z, xbc, dtr, conv0, ssm0, prm["convw"], prm["convb"], prm["dtb"], prm["a"], prm["dexp"],
      prm["gw"], prm["expand"])


def _ssd_params(conv_w, conv_b, dt_bias, a_log, d_skip, gnorm_w):
    pad = LANES - H_C
    expand = np.zeros((LANES, D_INNER), np.float32)
    for h in range(H_C):
        expand[h, h * P_C:(h + 1) * P_C] = 1.0
    return dict(
        convw=conv_w.astype(F32), convb=conv_b.astype(F32).reshape(1, CONV_DIM),
        dtb=jnp.pad(dt_bias.astype(F32), (0, pad)).reshape(1, LANES),
        a=jnp.pad(-jnp.exp(a_log.astype(F32)), (0, pad)).reshape(1, LANES),
        dexp=jnp.repeat(d_skip.astype(F32), P_C).reshape(1, D_INNER),
        gw=gnorm_w.astype(F32).reshape(1, D_INNER),
        expand=jnp.asarray(expand, BF16))


def _trunk(x, p, slopes, caches, tm):
    b, seq, _ = x.shape
    m = b * seq
    x2 = x.reshape(m, D_MODEL)
    new_k, new_v, new_s5, new_conv, new_ssm = [], [], [], [], []
    for i in range(DEPTH):
        j = i // 2
        wn_pre = p["norm_mix_pre"][i].reshape(1, D_MODEL)
        if i % 2 == 0:
            q, k, v, u = _norm_proj(x2, wn_pre, p["w_in_even"][j], (D_A, D_A, D_A, D_B),
                                    (DH_A ** -0.5, 1.0, 1.0, 1.0), tm)
            q3, k3, v3 = (t.reshape(b, seq, D_A) for t in (q, k, v))
            if caches is None:
                o_a = _attn_prompt(q3, k3, v3, slopes)
                h0 = jnp.zeros((b, 2 * S5_N), F32)
                tsub, nsub = 16, 4
            else:
                o_a = _attn_sample(q3, k3, v3, caches["k"], caches["v"], j, slopes)
                s0 = caches["s5"][j].astype(F32)
                h0 = jnp.concatenate([s0[..., 0].reshape(b, S5_N), s0[..., 1].reshape(b, S5_N)], axis=1)
                tsub, nsub = seq, 1
            o_b, hl = _s5_mixer(u.reshape(b, seq, D_B), h0, p["s5"][j], tsub, nsub)
            keep = min(WINDOW, seq)
            new_k.append(k3[:, seq - keep:].reshape(b, keep, H_A, DH_A))
            new_v.append(v3[:, seq - keep:].reshape(b, keep, H_A, DH_A))
            new_s5.append(jnp.stack([hl[:, :S5_N].reshape(b, G_B, P_B), hl[:, S5_N:].reshape(b, G_B, P_B)], axis=-1))
            acts = [o_a.reshape(m, D_A), o_b.reshape(m, D_B)]
            ws = [p["w_out_even"][j][:D_A], p["w_out_even"][j][D_A:]]
        else:
            z, xbc, dtr = _norm_proj(x2, wn_pre, p["w_in_odd"][j], (D_INNER, CONV_DIM, LANES),
                                     (1.0, 1.0, 1.0), tm)
            if caches is None:
                conv0 = jnp.zeros((b, HIST, CONV_DIM), F32)
                ssm0 = jnp.zeros((b, D_INNER, N_C), F32)
            else:
                conv0 = jnp.pad(caches["conv"][j].astype(F32), ((0, 0), (HIST - (CONV_K - 1), 0), (0, 0)))
                ssm0 = caches["ssm"][j].astype(F32).reshape(b, D_INNER, N_C)
            yg, convst, hl = _ssd_mixer(z.reshape(b, seq, D_INNER), xbc.reshape(b, seq, CONV_DIM),
                                        dtr.reshape(b, seq, LANES), conv0, ssm0, p["ssd"][j])
            new_conv.append(convst[:, HIST - (CONV_K - 1):])
            new_ssm.append(hl.reshape(b, H_C, P_C, N_C))
            acts = [yg.reshape(m, D_INNER)]
            ws = [p["w_out_odd"][j]]
        x2 = _proj_mlp(acts, ws, x2, p["norm_mix_post"][i].reshape(1, D_MODEL),
                       p["norm_mlp_pre"][i].reshape(1, D_MODEL), p["w_mlp_up"][i], p["w_mlp_down"][i],
                       p["norm_mlp_post"][i].reshape(1, D_MODEL), tm)
    return (x2.reshape(b, seq, D_MODEL), jnp.stack(new_k), jnp.stack(new_v), jnp.stack(new_s5),
            jnp.stack(new_conv), jnp.stack(new_ssm))


def kernel(x_prompt, x_sample, cache_k, cache_v, state_s5, state_conv, state_ssm, norm_mix_pre, norm_mix_post, norm_mlp_pre, norm_mlp_post, w_mlp_up, w_mlp_down, w_in_even, w_out_even, s5_lambda_re, s5_lambda_im, s5_log_dt, s5_b_re, s5_b_im, s5_c_re, s5_c_im, s5_d, s5_w_glu, s5_b_glu, w_in_odd, conv_w, conv_b, dt_bias, a_log, d_skip, gnorm_w, w_out_odd):
    n_even, n_odd = w_in_even.shape[0], w_in_odd.shape[0]
    odd_pad = LANES - H_C
    p = dict(
        norm_mix_pre=norm_mix_pre.astype(F32), norm_mix_post=norm_mix_post.astype(F32),
        norm_mlp_pre=norm_mlp_pre.astype(F32), norm_mlp_post=norm_mlp_post.astype(F32),
        w_mlp_up=w_mlp_up.astype(BF16), w_mlp_down=w_mlp_down.astype(BF16),
        w_in_even=w_in_even.astype(BF16), w_out_even=w_out_even.astype(BF16),
        w_in_odd=jnp.pad(w_in_odd, ((0, 0), (0, 0), (0, odd_pad))).astype(BF16),
        w_out_odd=w_out_odd.astype(BF16),
        s5=[_s5_params(s5_lambda_re[j], s5_lambda_im[j], s5_log_dt[j], s5_b_re[j], s5_b_im[j], s5_c_re[j],
                       s5_c_im[j], s5_d[j], s5_w_glu[j], s5_b_glu[j]) for j in range(n_even)],
        ssd=[_ssd_params(conv_w[j], conv_b[j], dt_bias[j], a_log[j], d_skip[j], gnorm_w[j]) for j in range(n_odd)],
    )
    slopes = jnp.asarray(np.power(2.0, -8.0 * np.arange(1, H_A + 1) / H_A), dtype=F32)
    w_buf = cache_k.shape[2]
    caches = dict(k=cache_k.reshape(n_even, -1, w_buf, D_A), v=cache_v.reshape(n_even, -1, w_buf, D_A),
                  s5=state_s5, conv=state_conv, ssm=state_ssm)
    y_p, k_p, v_p, s5_p, conv_p, ssm_p = _trunk(x_prompt, p, slopes, None, tm=256)
    y_s, k_s, v_s, s5_s, conv_s, ssm_s = _trunk(x_sample, p, slopes, caches, tm=256)
    return (y_p, y_s, k_p, v_p, s5_p, conv_p, ssm_p, k_s, v_s, s5_s, conv_s, ssm_s)
```

```python
import functools
import math

import numpy as np
import jax
import jax.numpy as jnp
from jax import lax
from jax.experimental import pallas as pl
from jax.experimental.pallas import tpu as pltpu

F32 = jnp.float32
BF16 = jnp.bfloat16

D_MODEL = 1024
DEPTH = 4
PAST_LEN = 8192
RMS_EPS = 1e-6
D_FF = 4 * D_MODEL
D_A = D_MODEL // 2
DH_A = 64
H_A = D_A // DH_A
BRANCHES = ((128, 1), (512, 4), (2048, 16))
WINDOW = 2048
D_B = D_MODEL - D_A
CH_B = 16
G_B = D_B // CH_B
P_B = 64
S5_N = G_B * P_B
D_INNER = 2 * D_MODEL
P_C = 64
H_C = D_INNER // P_C
G_C = 4
J_C = H_C // G_C
N_C = 128
CONV_K = 4
CONV_DIM = D_INNER + 2 * G_C * N_C
SSD_CHUNK = 128

LANES = 128
SUBLANES = 8
VMEM_LIMIT = 56 * 1024 * 1024
NEG_BIG = -1e30


def _act_dtype(rows):
    return BF16 if rows % (2 * SUBLANES) == 0 else F32


def _cparams(*sem):
    return pltpu.CompilerParams(dimension_semantics=sem, vmem_limit_bytes=VMEM_LIMIT)


def _rms(x, w):
    ms = jnp.mean(x * x, axis=-1, keepdims=True)
    return x * lax.rsqrt(ms + RMS_EPS) * w


def _const_spec(shape):
    nd = len(shape)
    return pl.BlockSpec(shape, lambda *_: (0,) * nd, pipeline_mode=pl.Buffered(1))


def _norm_proj_body(x_ref, wn_ref, w_ref, *out_refs, splits, scales):
    h = _rms(x_ref[...], wn_ref[...]).astype(BF16)
    off = 0
    for o_ref, n, sc in zip(out_refs, splits, scales):
        r = jnp.dot(h, w_ref[:, off:off + n], preferred_element_type=F32)
        if sc != 1.0:
            r = r * sc
        o_ref[...] = r.astype(o_ref.dtype)
        off += n


def _norm_proj(x2, wn, w, splits, scales, tm):
    m = x2.shape[0]
    n_total = w.shape[1]
    assert sum(splits) == n_total and m % tm == 0
    body = functools.partial(_norm_proj_body, splits=splits, scales=scales)
    return pl.pallas_call(
        body,
        grid=(m // tm,),
        in_specs=[pl.BlockSpec((tm, D_MODEL), lambda i: (i, 0)),
                  _const_spec((1, D_MODEL)),
                  _const_spec((D_MODEL, n_total))],
        out_specs=[pl.BlockSpec((tm, n), lambda i: (i, 0)) for n in splits],
        out_shape=[jax.ShapeDtypeStruct((m, n), F32) for n in splits],
        compiler_params=_cparams("parallel"),
        name="norm_proj",
    )(x2, wn, w)


def _proj_mlp_body(*refs, n_in, tf):
    a_refs = refs[:n_in]
    w_refs = refs[n_in:2 * n_in]
    x_ref, wpost_ref, wpre_ref, wup_ref, wdown_ref, wpost2_ref, o_ref = refs[2 * n_in:]
    m = None
    for a_ref, w_ref in zip(a_refs, w_refs):
        t = jnp.dot(a_ref[...].astype(BF16), w_ref[...], preferred_element_type=F32)
        m = t if m is None else m + t
    x = x_ref[...] + _rms(m, wpost_ref[...])
    h = _rms(x, wpre_ref[...]).astype(BF16)
    acc = None
    for c in range(D_FF // tf):
        a = jnp.dot(h, wup_ref[:, c * tf:(c + 1) * tf], preferred_element_type=F32)
        a = jnp.maximum(a, 0.0)
        a = (a * a).astype(BF16)
        t = jnp.dot(a, wdown_ref[c * tf:(c + 1) * tf, :], preferred_element_type=F32)
        acc = t if acc is None else acc + t
    o_ref[...] = x + _rms(acc, wpost2_ref[...])


def _proj_mlp(acts, ws, x2, wpost, wpre, wup, wdown, wpost2, tm, tf=512):
    m = x2.shape[0]
    n_in = len(acts)
    body = functools.partial(_proj_mlp_body, n_in=n_in, tf=tf)
    in_specs = [pl.BlockSpec((tm, a.shape[1]), lambda i: (i, 0)) for a in acts]
    in_specs += [_const_spec(w.shape) for w in ws]
    in_specs += [pl.BlockSpec((tm, D_MODEL), lambda i: (i, 0)),
                 _const_spec((1, D_MODEL)), _const_spec((1, D_MODEL)),
                 _const_spec((D_MODEL, D_FF)), _const_spec((D_FF, D_MODEL)),
                 _const_spec((1, D_MODEL))]
    return pl.pallas_call(
        body,
        grid=(m // tm,),
        in_specs=in_specs,
        out_specs=pl.BlockSpec((tm, D_MODEL), lambda i: (i, 0)),
        out_shape=jax.ShapeDtypeStruct((m, D_MODEL), F32),
        compiler_params=_cparams("parallel"),
        name="proj_mlp",
    )(*acts, *ws, x2, wpost, wpre, wup, wdown, wpost2)


def _attn_prompt_body(slope_ref, q_ref, k_ref, v_ref, o_ref, *scr, seq):
    hp = pl.program_id(1)
    o_scr, m_scr, l_scr, bias_scr = scr[0:3], scr[3:6], scr[6:9], scr[9:12]
    qb = LANES
    n_units = seq // qb
    lane = lax.broadcasted_iota(jnp.int32, (qb, LANES), 1)
    head0 = lane < DH_A

    cfg = []
    for bi, (win, dil) in enumerate(BRANCHES):
        nqb = seq // dil // qb
        nk = 2 * qb if nqb >= 2 else qb
        rowi = lax.broadcasted_iota(jnp.int32, (2 * qb, nk), 0)
        coli = lax.broadcasted_iota(jnp.int32, (2 * qb, nk), 1)
        d = (nk - qb) + (rowi & (qb - 1)) - coli
        slope = jnp.where(rowi < qb, slope_ref[2 * hp], slope_ref[2 * hp + 1])
        bias_scr[bi][...] = jnp.where((d >= 0) & (d <= win // dil), -(slope * dil) * d.astype(F32), NEG_BIG)
        cfg.append((dil, nqb, nk))

    def step(u, carry):
        for bi, (dil, nqb, nk) in enumerate(cfg):
            def rows(start, dil=dil):
                if dil == 1:
                    return pl.ds(pl.multiple_of(start, qb), qb)
                return pl.ds(start, qb, stride=dil)

            r, i = (u, 0) if nqb == 1 else (u // nqb, u % nqb)
            cur = rows(r + dil * qb * i)
            q = q_ref[cur, :]
            qs = jnp.concatenate([jnp.where(head0, q, 0.0), jnp.where(head0, 0.0, q)], axis=0).astype(BF16)
            if nk == qb:
                kk = k_ref[cur, :].astype(BF16)
                vv = v_ref[cur, :].astype(BF16)
            else:
                prev = rows(r + dil * qb * jnp.maximum(i - 1, 0))
                kk = jnp.concatenate([k_ref[prev, :], k_ref[cur, :]], axis=0).astype(BF16)
                vv = jnp.concatenate([v_ref[prev, :], v_ref[cur, :]], axis=0).astype(BF16)
            s = lax.dot_general(qs, kk, (((1,), (1,)), ((), ())), preferred_element_type=F32)
            s = s + bias_scr[bi][...]
            if nk != qb:
                pen = jnp.where(i == 0, NEG_BIG, 0.0)
                s = jnp.concatenate([s[:, :qb] + pen, s[:, qb:]], axis=1)
            mx = jnp.max(s, axis=1, keepdims=True)
            p = jnp.exp(s - mx).astype(BF16)
            va = jnp.concatenate([vv, jnp.ones((nk, LANES), BF16)], axis=1)
            od = jnp.dot(p, va, preferred_element_type=F32)
            o_scr[bi][cur, :] = jnp.where(head0, od[:qb, :LANES], od[qb:, :LANES])
            m_scr[bi][cur, :] = jnp.where(head0, mx[:qb], mx[qb:])
            l_scr[bi][cur, :] = jnp.where(head0, od[:qb, LANES:], od[qb:, LANES:])
        return carry

    lax.fori_loop(0, n_units, step, 0, unroll=4)

    cb = 256

    def combine(c, carry):
        sl = pl.ds(pl.multiple_of(c * cb, cb), cb)
        ms = [m_scr[b][sl, :] for b in range(3)]
        mx = jnp.maximum(jnp.maximum(ms[0], ms[1]), ms[2])
        num = None
        den = None
        for b in range(3):
            w = jnp.exp(ms[b] - mx)
            tn = o_scr[b][sl, :] * w
            td = l_scr[b][sl, :] * w
            num = tn if num is None else num + tn
            den = td if den is None else den + td
        o_ref[sl, :] = (num / den).astype(o_ref.dtype)
        return carry

    lax.fori_loop(0, seq // cb, combine, 0)


def _attn_prompt(q, k, v, slopes):
    b, seq, _ = q.shape
    assert seq % (16 * LANES) == 0
    blk = pl.BlockSpec((None, seq, LANES), lambda bb, hp: (bb, 0, hp))
    body = functools.partial(_attn_prompt_body, seq=seq)
    return pl.pallas_call(
        body,
        grid=(b, D_A // LANES),
        in_specs=[pl.BlockSpec(memory_space=pltpu.SMEM), blk, blk, blk],
        out_specs=blk,
        out_shape=jax.ShapeDtypeStruct((b, seq, D_A), BF16),
        scratch_shapes=[pltpu.VMEM((seq, LANES), F32) for _ in range(9)]
        + [pltpu.VMEM((2 * LANES, 2 * LANES if seq // dil >= 2 * LANES else LANES), F32) for _, dil in BRANCHES],
        compiler_params=_cparams("parallel", "parallel"),
        name="attn_prompt",
    )(slopes, q, k, v)


def _attn_sample_body(slope_ref, q_ref, kn_ref, vn_ref, kc_ref, vc_ref, o_ref, *, t_new, w_buf):
    hp = pl.program_id(1)
    lane = lax.broadcasted_iota(jnp.int32, (t_new, LANES), 1)
    head0 = lane < DH_A
    q = q_ref[...]
    qs = jnp.concatenate([jnp.where(head0, q, 0.0), jnp.where(head0, 0.0, q)], axis=0).astype(BF16)
    rows2 = 2 * t_new
    rsel = lax.broadcasted_iota(jnp.int32, (rows2, 1), 0)
    slope = jnp.where(rsel < t_new, slope_ref[2 * hp], slope_ref[2 * hp + 1])

    def scores(k_ref, ncols, d0):
        kk = k_ref[...].astype(BF16)
        s = lax.dot_general(qs, kk, (((1,), (1,)), ((), ())), preferred_element_type=F32)
        rowi = lax.broadcasted_iota(jnp.int32, (rows2, ncols), 0)
        coli = lax.broadcasted_iota(jnp.int32, (rows2, ncols), 1)
        t = jnp.where(rowi < t_new, rowi, rowi - t_new)
        d = d0 + t - coli
        mult = jnp.zeros((rows2, ncols), F32)
        for win, dil in BRANCHES:
            hit = (d >= 0) & (d <= win) & ((d & (dil - 1)) == 0)
            mult = mult + jnp.where(hit, 1.0, 0.0)
        s = jnp.where(mult > 0.0, s - slope * d.astype(F32), NEG_BIG)
        return s, mult

    s_c, mult_c = scores(kc_ref, w_buf, w_buf)
    s_n, mult_n = scores(kn_ref, t_new, 0)
    mx = jnp.maximum(jnp.max(s_c, axis=1, keepdims=True), jnp.max(s_n, axis=1, keepdims=True))
    p_c = mult_c * jnp.exp(s_c - mx)
    p_n = mult_n * jnp.exp(s_n - mx)
    den = jnp.sum(p_c, axis=1, keepdims=True) + jnp.sum(p_n, axis=1, keepdims=True)
    o = jnp.dot(p_c.astype(BF16), vc_ref[...].astype(BF16), preferred_element_type=F32)
    o = o + jnp.dot(p_n.astype(BF16), vn_ref[...].astype(BF16), preferred_element_type=F32)
    o = o / den
    o_ref[...] = jnp.where(head0, o[:t_new], o[t_new:]).astype(o_ref.dtype)


def _attn_sample(q, kn, vn, cache_k, cache_v, layer, slopes):
    b, t_new, _ = q.shape
    w_buf = cache_k.shape[2]
    assert w_buf >= WINDOW and t_new % SUBLANES == 0
    new_blk = pl.BlockSpec((None, t_new, LANES), lambda bb, hp: (bb, 0, hp))
    cache_blk = pl.BlockSpec((None, None, w_buf, LANES), lambda bb, hp: (layer, bb, 0, hp))
    body = functools.partial(_attn_sample_body, t_new=t_new, w_buf=w_buf)
    return pl.pallas_call(
        body,
        grid=(b, D_A // LANES),
        in_specs=[pl.BlockSpec(memory_space=pltpu.SMEM), new_blk, new_blk, new_blk, cache_blk, cache_blk],
        out_specs=new_blk,
        out_shape=jax.ShapeDtypeStruct((b, t_new, D_A), _act_dtype(t_new)),
        compiler_params=_cparams("parallel", "parallel"),
        name="attn_sample",
    )(slopes, q, kn, vn, cache_k, cache_v)


S5_UT = D_B // LANES
S5_TW = S5_N // S5_UT


def _gelu_tanh(x):
    c = math.sqrt(2.0 / math.pi)
    return 0.5 * x * (1.0 + jnp.tanh(c * (x + 0.044715 * (x * x * x))))


def _s5_body(u_ref, h0_ref, perm_ref, permt_ref, wb_ref, lam_ref, wc_ref, dskip_ref, wglu_ref, bglu_ref,
             o_ref, hl_ref, bu_scr, up_scr, st_scr, *, nb, tsub, nsub):
    step = pl.program_id(0)
    rsub = nb * tsub
    tstep = tsub * nsub

    @pl.when(step == 0)
    def _():
        st_scr[...] = h0_ref[...]

    for sb in range(nsub):
        u_nat = u_ref[:, sb * tsub:(sb + 1) * tsub, :].reshape(rsub, D_B)
        up_scr[sb * rsub:(sb + 1) * rsub, :] = jnp.dot(
            perm_ref[...], u_nat, preferred_element_type=F32, precision=lax.Precision.HIGHEST)
    for j in range(S5_UT):
        uj = up_scr[:, j * LANES:(j + 1) * LANES].astype(BF16)
        r = jnp.dot(uj, wb_ref[j], preferred_element_type=F32)
        bu_scr[:, j * S5_TW:(j + 1) * S5_TW] = r[:, :S5_TW]
        bu_scr[:, S5_N + j * S5_TW:S5_N + (j + 1) * S5_TW] = r[:, S5_TW:]

    tiles_per_pass = 4
    for pg in range(S5_N // LANES // tiles_per_pass):
        cols = [(pg * tiles_per_pass + i) * LANES for i in range(tiles_per_pass)]
        lr = [jnp.broadcast_to(lam_ref[0:1, c:c + LANES], (nb, LANES)) for c in cols]
        li = [jnp.broadcast_to(lam_ref[1:2, c:c + LANES], (nb, LANES)) for c in cols]
        init = tuple(st_scr[:, c:c + LANES] for c in cols) + tuple(st_scr[:, S5_N + c:S5_N + c + LANES] for c in cols)

        def tick(t, carry, cols=cols, lr=lr, li=li):
            row = pl.ds(pl.multiple_of(t * nb, nb), nb)
            hr, hi = carry[:tiles_per_pass], carry[tiles_per_pass:]
            nr, ni = [], []
            for i, c in enumerate(cols):
                br = bu_scr[row, c:c + LANES]
                bi = bu_scr[row, S5_N + c:S5_N + c + LANES]
                r_new = lr[i] * hr[i] - li[i] * hi[i] + br
                i_new = lr[i] * hi[i] + li[i] * hr[i] + bi
                bu_scr[row, c:c + LANES] = r_new
                bu_scr[row, S5_N + c:S5_N + c + LANES] = i_new
                nr.append(r_new)
                ni.append(i_new)
            return tuple(nr) + tuple(ni)

        fin = lax.fori_loop(0, tstep, tick, init)
        for i, c in enumerate(cols):
            st_scr[:, c:c + LANES] = fin[i]
            st_scr[:, S5_N + c:S5_N + c + LANES] = fin[tiles_per_pass + i]

    hl_ref[...] = st_scr[...]

    ys = []
    for j in range(S5_UT):
        hre = bu_scr[:, j * S5_TW:(j + 1) * S5_TW].astype(BF16)
        him = bu_scr[:, S5_N + j * S5_TW:S5_N + (j + 1) * S5_TW].astype(BF16)
        y = jnp.dot(hre, wc_ref[j, :S5_TW, :], preferred_element_type=F32)
        y = y + jnp.dot(him, wc_ref[j, S5_TW:, :], preferred_element_type=F32)
        ys.append(y)
    y = jnp.concatenate(ys, axis=1) + dskip_ref[...] * up_scr[...]
    g = _gelu_tanh(y)
    gate = jnp.dot(g.astype(BF16), wglu_ref[...], preferred_element_type=F32) + bglu_ref[...]
    out = (g * jax.nn.sigmoid(gate)).astype(BF16)
    for sb in range(nsub):
        o_nat = jnp.dot(permt_ref[...], out[sb * rsub:(sb + 1) * rsub, :], preferred_element_type=F32)
        o_ref[:, sb * tsub:(sb + 1) * tsub, :] = o_nat.reshape(nb, tsub, D_B).astype(o_ref.dtype)


def _s5_mixer(u, h0, prm, tsub, nsub):
    nb, seq, _ = u.shape
    tstep = tsub * nsub
    rsub = nb * tsub
    rows = nb * tstep
    assert seq % tstep == 0 and nb % SUBLANES == 0 and tsub % SUBLANES == 0
    perm = np.zeros((rsub, rsub), np.float32)
    for t in range(tsub):
        for bb in range(nb):
            perm[t * nb + bb, bb * tsub + t] = 1.0
    body = functools.partial(_s5_body, nb=nb, tsub=tsub, nsub=nsub)
    return pl.pallas_call(
        body,
        grid=(seq // tstep,),
        in_specs=[pl.BlockSpec((nb, tstep, D_B), lambda i: (0, i, 0)),
                  _const_spec((nb, 2 * S5_N)),
                  _const_spec((rsub, rsub)), _const_spec((rsub, rsub)),
                  _const_spec((S5_UT, LANES, 2 * S5_TW)),
                  _const_spec((2, S5_N)),
                  _const_spec((S5_UT, 2 * S5_TW, LANES)),
                  _const_spec((1, D_B)),
                  _const_spec((D_B, D_B)),
                  _const_spec((1, D_B))],
        out_specs=[pl.BlockSpec((nb, tstep, D_B), lambda i: (0, i, 0)),
                   pl.BlockSpec((nb, 2 * S5_N), lambda i: (0, 0))],
        out_shape=[jax.ShapeDtypeStruct((nb, seq, D_B), _act_dtype(tsub)),
                   jax.ShapeDtypeStruct((nb, 2 * S5_N), F32)],
        scratch_shapes=[pltpu.VMEM((rows, 2 * S5_N), F32),
                        pltpu.VMEM((rows, D_B), F32),
                        pltpu.VMEM((nb, 2 * S5_N), F32)],
        compiler_params=_cparams("arbitrary"),
        name="s5_mixer",
    )(u, h0, jnp.asarray(perm, F32), jnp.asarray(perm.T, BF16), prm["wb"], prm["lam"], prm["wc"],
      prm["dskip"], prm["wglu"], prm["bglu"])


def _s5_params(lam_re, lam_im, log_dt, b_re, b_im, c_re, c_im, d_skip, w_glu, b_glu):
    lam = lax.complex(lam_re.astype(F32), lam_im.astype(F32))
    dt = jnp.exp(log_dt.astype(F32))[:, None]
    lam_bar = jnp.exp(lam * dt)
    b_bar = ((lam_bar - 1.0) / lam)[..., None] * lax.complex(b_re.astype(F32), b_im.astype(F32))
    gpt = LANES // CH_B
    eye = jnp.eye(gpt, dtype=F32)

    def b_blocks(part):
        x = part.reshape(S5_UT, gpt, P_B, CH_B)
        x = jnp.einsum("jgpc,gh->jgchp", x, eye)
        return x.reshape(S5_UT, gpt * CH_B, gpt * P_B)

    def c_blocks(part):
        x = part.reshape(S5_UT, gpt, CH_B, P_B)
        x = jnp.einsum("jgcp,gh->jgphc", x, eye)
        return x.reshape(S5_UT, gpt * P_B, gpt * CH_B)

    wb = jnp.concatenate([b_blocks(b_bar.real), b_blocks(b_bar.imag)], axis=2).astype(BF16)
    wc = jnp.concatenate([c_blocks(c_re.astype(F32)), -c_blocks(c_im.astype(F32))], axis=1).astype(BF16)
    lam2 = jnp.stack([lam_bar.real.reshape(S5_N), lam_bar.imag.reshape(S5_N)])
    return dict(wb=wb, wc=wc, lam=lam2, dskip=d_skip.astype(F32).reshape(1, D_B),
                wglu=w_glu.astype(BF16), bglu=b_glu.astype(F32).reshape(1, D_B))


HIST = SUBLANES
XB0 = D_INNER
XC0 = D_INNER + G_C * N_C
GW = D_INNER // G_C


def _exact_expand(a, e_ref):
    a1 = a.astype(BF16)
    r1 = a - a1.astype(F32)
    a2 = r1.astype(BF16)
    a3 = (r1 - a2.astype(F32)).astype(BF16)
    e = e_ref[...]
    out = jnp.dot(a1, e, preferred_element_type=F32)
    out = out + jnp.dot(a2, e, preferred_element_type=F32)
    return out + jnp.dot(a3, e, preferred_element_type=F32)


def _ssd_body(z_ref, xbc_ref, dt_ref, conv0_ref, ssm0_ref, convw_ref, convb_ref, dtb_ref, a_ref, dexp_ref,
              gw_ref, e_ref, y_ref, convst_ref, hl_ref, xpad, state, *, lc):
    ch = pl.program_id(1)
    cl = SSD_CHUNK

    @pl.when(ch == 0)
    def _():
        xpad[0:HIST, :] = conv0_ref[...]
        state[...] = ssm0_ref[...]

    if lc < cl:
        xpad[HIST:, :] = jnp.zeros((cl, CONV_DIM), F32)
    xpad[HIST:HIST + lc, :] = xbc_ref[...]
    conv = convb_ref[...] + convw_ref[CONV_K - 1:CONV_K, :] * xpad[HIST:HIST + cl, :]
    for kk in range(CONV_K - 1):
        sh = CONV_K - 1 - kk
        conv = conv + convw_ref[kk:kk + 1, :] * xpad[HIST - sh:HIST - sh + cl, :]
    xs = conv * jax.nn.sigmoid(conv)
    tail = xpad[lc:lc + HIST, :]
    xpad[0:HIST, :] = tail
    convst_ref[...] = tail

    dt = jax.nn.softplus(dt_ref[...] + dtb_ref[...])
    if lc < cl:
        dt = jnp.concatenate([dt, jnp.zeros((cl - lc, LANES), F32)], axis=0)
    da = dt * a_ref[...]
    ri = lax.broadcasted_iota(jnp.int32, (cl, cl), 0)
    ci = lax.broadcasted_iota(jnp.int32, (cl, cl), 1)
    tri = ri >= ci
    a_cs = jnp.dot(jnp.where(tri, 1.0, 0.0), da, preferred_element_type=F32, precision=lax.Precision.HIGHEST)
    a_cs_t = a_cs.T
    dt_t = dt.T
    e_cs = jnp.exp(a_cs)
    tot = a_cs[cl - 1:cl, :]
    w_end = jnp.exp(tot - a_cs) * dt
    scale_x = _exact_expand(e_cs, e_ref)
    wend_x = _exact_expand(w_end, e_ref)

    x = xs[:, :D_INNER]
    xw = (x * wend_x).astype(BF16)
    lane = lax.broadcasted_iota(jnp.int32, (cl, LANES), 1)
    head0 = lane < P_C
    y_parts = []
    for g in range(G_C):
        bg = xs[:, XB0 + g * N_C:XB0 + (g + 1) * N_C].astype(BF16)
        cg = xs[:, XC0 + g * N_C:XC0 + (g + 1) * N_C].astype(BF16)
        cb = lax.dot_general(cg, bg, (((1,), (1,)), ((), ())), preferred_element_type=F32)
        prev = state[g * GW:(g + 1) * GW, :]
        y_off = lax.dot_general(cg, prev.astype(BF16), (((1,), (1,)), ((), ())), preferred_element_type=F32)
        for jp in range(J_C // 2):
            ms = []
            for hh in range(2):
                h = g * J_C + 2 * jp + hh
                seg = a_cs[:, h:h + 1] - a_cs_t[h:h + 1, :]
                dec = jnp.exp(jnp.where(tri, seg, NEG_BIG))
                ms.append((cb * dec * dt_t[h:h + 1, :]).astype(BF16))
            c0 = g * GW + jp * LANES
            xp = x[:, c0:c0 + LANES].astype(BF16)
            r = jnp.dot(jnp.concatenate(ms, axis=0), xp, preferred_element_type=F32)
            yd = jnp.where(head0, r[:cl], r[cl:])
            y_parts.append(yd + y_off[:, jp * LANES:(jp + 1) * LANES] * scale_x[:, c0:c0 + LANES])
        new = lax.dot_general(xw[:, g * GW:(g + 1) * GW], bg, (((0,), (0,)), ((), ())),
                              preferred_element_type=F32)
        for j in range(J_C):
            h = g * J_C + j
            r0 = g * GW + j * P_C
            state[r0:r0 + P_C, :] = state[r0:r0 + P_C, :] * jnp.exp(tot[:, h:h + 1]) + new[j * P_C:(j + 1) * P_C, :]
    hl_ref[...] = state[...]

    y = jnp.concatenate(y_parts, axis=1) + dexp_ref[...] * x
    y = y[:lc] if lc < cl else y
    z = z_ref[...]
    gt = y * (z * jax.nn.sigmoid(z))
    outs = []
    for g in range(G_C):
        gg = gt[:, g * GW:(g + 1) * GW]
        ms = jnp.mean(gg * gg, axis=-1, keepdims=True)
        outs.append(gg * lax.rsqrt(ms + RMS_EPS))
    y_ref[...] = (jnp.concatenate(outs, axis=1) * gw_ref[...]).astype(y_ref.dtype)


def _ssd_mixer(z, xbc, dtr, conv0, ssm0, prm):
    b, seq, _ = z.shape
    lc = SSD_CHUNK if seq % SSD_CHUNK == 0 else seq
    assert lc % SUBLANES == 0 and CONV_K - 1 <= lc <= SSD_CHUNK
    nch = seq // lc
    body = functools.partial(_ssd_body, lc=lc)

    def row_blk(n):
        return pl.BlockSpec((None, lc, n), lambda bb, c: (bb, c, 0))

    def per_b(r, n):
        return pl.BlockSpec((None, r, n), lambda bb, c: (bb, 0, 0))

    return pl.pallas_call(
        body,
        grid=(b, nch),
        in_specs=[row_blk(D_INNER), row_blk(CONV_DIM), row_blk(LANES),
                  per_b(HIST, CONV_DIM), per_b(D_INNER, N_C),
                  _const_spec((CONV_K, CONV_DIM)), _const_spec((1, CONV_DIM)),
                  _const_spec((1, LANES)), _const_spec((1, LANES)),
                  _const_spec((1, D_INNER)), _const_spec((1, D_INNER)),
                  _const_spec((LANES, D_INNER))],
        out_specs=[row_blk(D_INNER), per_b(HIST, CONV_DIM), per_b(D_INNER, N_C)],
        out_shape=[jax.ShapeDtypeStruct((b, seq, D_INNER), _act_dtype(lc)),
                   jax.ShapeDtypeStruct((b, HIST, CONV_DIM), F32),
                   jax.ShapeDtypeStruct((b, D_INNER, N_C), F32)],
        scratch_shapes=[pltpu.VMEM((HIST + SSD_CHUNK, CONV_DIM), F32),
                        pltpu.VMEM((D_INNER, N_C), F32)],
        compiler_params=_cparams("parallel", "arbitrary"),
        name="ssd_mixer",
    )(z, xbc, dtr, conv0, ssm0, prm["convw"], prm["convb"], prm["dtb"], prm["a"], prm["dexp"],
      prm["gw"], prm["expand"])


def _ssd_params(conv_w, conv_b, dt_bias, a_log, d_skip, gnorm_w):
    pad = LANES - H_C
    expand = np.zeros((LANES, D_INNER), np.float32)
    for h in range(H_C):
        expand[h, h * P_C:(h + 1) * P_C] = 1.0
    return dict(
        convw=conv_w.astype(F32), convb=conv_b.astype(F32).reshape(1, CONV_DIM),
        dtb=jnp.pad(dt_bias.astype(F32), (0, pad)).reshape(1, LANES),
        a=jnp.pad(-jnp.exp(a_log.astype(F32)), (0, pad)).reshape(1, LANES),
        dexp=jnp.repeat(d_skip.astype(F32), P_C).reshape(1, D_INNER),
        gw=gnorm_w.astype(F32).reshape(1, D_INNER),
        expand=jnp.asarray(expand, BF16))


def _trunk(x, p, slopes, caches, tm):
    b, seq, _ = x.shape
    m = b * seq
    x2 = x.reshape(m, D_MODEL)
    new_k, new_v, new_s5, new_conv, new_ssm = [], [], [], [], []
    for i in range(DEPTH):
        j = i // 2
        wn_pre = p["norm_mix_pre"][i].reshape(1, D_MODEL)
        if i % 2 == 0:
            q, k, v, u = _norm_proj(x2, wn_pre, p["w_in_even"][j], (D_A, D_A, D_A, D_B),
                                    (DH_A ** -0.5, 1.0, 1.0, 1.0), tm)
            q3, k3, v3 = (t.reshape(b, seq, D_A) for t in (q, k, v))
            if caches is None:
                o_a = _attn_prompt(q3, k3, v3, slopes)
                h0 = jnp.zeros((b, 2 * S5_N), F32)
                tsub, nsub = 16, 4
            else:
                o_a = _attn_sample(q3, k3, v3, caches["k"], caches["v"], j, slopes)
                s0 = caches["s5"][j].astype(F32)
                h0 = jnp.concatenate([s0[..., 0].reshape(b, S5_N), s0[..., 1].reshape(b, S5_N)], axis=1)
                tsub, nsub = seq, 1
            o_b, hl = _s5_mixer(u.reshape(b, seq, D_B), h0, p["s5"][j], tsub, nsub)
            keep = min(WINDOW, seq)
            new_k.append(k3[:, seq - keep:].reshape(b, keep, H_A, DH_A))
            new_v.append(v3[:, seq - keep:].reshape(b, keep, H_A, DH_A))
            new_s5.append(jnp.stack([hl[:, :S5_N].reshape(b, G_B, P_B), hl[:, S5_N:].reshape(b, G_B, P_B)], axis=-1))
            acts = [o_a.reshape(m, D_A), o_b.reshape(m, D_B)]
            ws = [p["w_out_even"][j][:D_A], p["w_out_even"][j][D_A:]]
        else:
            z, xbc, dtr = _norm_proj(x2, wn_pre, p["w_in_odd"][j], (D_INNER, CONV_DIM, LANES),
                                     (1.0, 1.0, 1.0), tm)
            if caches is None:
                conv0 = jnp.zeros((b, HIST, CONV_DIM), F32)
                ssm0 = jnp.zeros((b, D_INNER, N_C), F32)
            else:
                conv0 = jnp.pad(caches["conv"][j].astype(F32), ((0, 0), (HIST - (CONV_K - 1), 0), (0, 0)))
                ssm0 = caches["ssm"][j].astype(F32).reshape(b, D_INNER, N_C)
            yg, convst, hl = _ssd_mixer(z.reshape(b, seq, D_INNER), xbc.reshape(b, seq, CONV_DIM),
                                        dtr.reshape(b, seq, LANES), conv0, ssm0, p["ssd"][j])
            new_conv.append(convst[:, HIST - (CONV_K - 1):])
            new_ssm.append(hl.reshape(b, H_C, P_C, N_C))
            acts = [yg.reshape(m, D_INNER)]
            ws = [p["w_out_odd"][j]]
        x2 = _proj_mlp(acts, ws, x2, p["norm_mix_post"][i].reshape(1, D_MODEL),
                       p["norm_mlp_pre"][i].reshape(1, D_MODEL), p["w_mlp_up"][i], p["w_mlp_down"][i],
                       p["norm_mlp_post"][i].reshape(1, D_MODEL), tm)
    return (x2.reshape(b, seq, D_MODEL), jnp.stack(new_k), jnp.stack(new_v), jnp.stack(new_s5),
            jnp.stack(new_conv), jnp.stack(new_ssm))


def kernel(x_prompt, x_sample, cache_k, cache_v, state_s5, state_conv, state_ssm, norm_mix_pre, norm_mix_post, norm_mlp_pre, norm_mlp_post, w_mlp_up, w_mlp_down, w_in_even, w_out_even, s5_lambda_re, s5_lambda_im, s5_log_dt, s5_b_re, s5_b_im, s5_c_re, s5_c_im, s5_d, s5_w_glu, s5_b_glu, w_in_odd, conv_w, conv_b, dt_bias, a_log, d_skip, gnorm_w, w_out_odd):
    n_even, n_odd = w_in_even.shape[0], w_in_odd.shape[0]
    odd_pad = LANES - H_C
    p = dict(
        norm_mix_pre=norm_mix_pre.astype(F32), norm_mix_post=norm_mix_post.astype(F32),
        norm_mlp_pre=norm_mlp_pre.astype(F32), norm_mlp_post=norm_mlp_post.astype(F32),
        w_mlp_up=w_mlp_up.astype(BF16), w_mlp_down=w_mlp_down.astype(BF16),
        w_in_even=w_in_even.astype(BF16), w_out_even=w_out_even.astype(BF16),
        w_in_odd=jnp.pad(w_in_odd, ((0, 0), (0, 0), (0, odd_pad))).astype(BF16),
        w_out_odd=w_out_odd.astype(BF16),
        s5=[_s5_params(s5_lambda_re[j], s5_lambda_im[j], s5_log_dt[j], s5_b_re[j], s5_b_im[j], s5_c_re[j],
                       s5_c_im[j], s5_d[j], s5_w_glu[j], s5_b_glu[j]) for j in range(n_even)],
        ssd=[_ssd_params(conv_w[j], conv_b[j], dt_bias[j], a_log[j], d_skip[j], gnorm_w[j]) for j in range(n_odd)],
    )
    slopes = jnp.asarray(np.power(2.0, -8.0 * np.arange(1, H_A + 1) / H_A), dtype=F32)
    w_buf = cache_k.shape[2]
    caches = dict(k=cache_k.reshape(n_even, -1, w_buf, D_A), v=cache_v.reshape(n_even, -1, w_buf, D_A),
                  s5=state_s5, conv=state_conv, ssm=state_ssm)
    y_p, k_p, v_p, s5_p, conv_p, ssm_p = _trunk(x_prompt, p, slopes, None, tm=512)
    y_s, k_s, v_s, s5_s, conv_s, ssm_s = _trunk(x_sample, p, slopes, caches, tm=256)
    return (y_p, y_s, k_p, v_p, s5_p, conv_p, ssm_p, k_s, v_s, s5_s, conv_s, ssm_s)
```

```python
import functools
import math

import numpy as np
import jax
import jax.numpy as jnp
from jax import lax
from jax.experimental import pallas as pl
from jax.experimental.pallas import tpu as pltpu

F32 = jnp.float32
BF16 = jnp.bfloat16

D_MODEL = 1024
DEPTH = 4
PAST_LEN = 8192
RMS_EPS = 1e-6
D_FF = 4 * D_MODEL
D_A = D_MODEL // 2
DH_A = 64
H_A = D_A // DH_A
BRANCHES = ((128, 1), (512, 4), (2048, 16))
WINDOW = 2048
D_B = D_MODEL - D_A
CH_B = 16
G_B = D_B // CH_B
P_B = 64
S5_N = G_B * P_B
D_INNER = 2 * D_MODEL
P_C = 64
H_C = D_INNER // P_C
G_C = 4
J_C = H_C // G_C
N_C = 128
CONV_K = 4
CONV_DIM = D_INNER + 2 * G_C * N_C
SSD_CHUNK = 128

LANES = 128
SUBLANES = 8
VMEM_LIMIT = 56 * 1024 * 1024
NEG_BIG = -1e30


def _act_dtype(rows):
    return BF16 if rows % (2 * SUBLANES) == 0 else F32


def _cparams(*sem):
    return pltpu.CompilerParams(dimension_semantics=sem, vmem_limit_bytes=VMEM_LIMIT)


def _rms(x, w):
    ms = jnp.mean(x * x, axis=-1, keepdims=True)
    return x * lax.rsqrt(ms + RMS_EPS) * w


def _const_spec(shape):
    nd = len(shape)
    return pl.BlockSpec(shape, lambda *_: (0,) * nd, pipeline_mode=pl.Buffered(1))


def _norm_proj_body(*refs, splits, scales, head_splits, n_stacks):
    x_ref, wn_ref, w_ref = refs[:3]
    out_refs = refs[3 + n_stacks:3 + n_stacks + len(splits)]
    head_refs = refs[3 + n_stacks + len(splits):]
    h = _rms(x_ref[...], wn_ref[...]).astype(BF16)
    off = 0
    for idx, (o_ref, n, sc) in enumerate(zip(out_refs, splits, scales)):
        r = jnp.dot(h, w_ref[:, off:off + n], preferred_element_type=F32)
        if sc != 1.0:
            r = r * sc
        o_ref[...] = r.astype(o_ref.dtype)
        if idx in head_splits:
            head_refs[head_splits.index(idx)][...] = r.reshape(r.shape[0], H_A, DH_A)
        off += n


def _norm_proj(x2, wn, w, splits, scales, tm, head_splits=(), stacks=None, layer=0, n_layers=1):
    m = x2.shape[0]
    n_total = w.shape[1]
    assert sum(splits) == n_total and m % tm == 0
    nblk = m // tm
    n_stacks = 0 if stacks is None else len(stacks)
    body = functools.partial(_norm_proj_body, splits=splits, scales=scales, head_splits=tuple(head_splits),
                             n_stacks=n_stacks)
    in_specs = [pl.BlockSpec((tm, D_MODEL), lambda i: (i, 0)),
                _const_spec((1, D_MODEL)),
                _const_spec((D_MODEL, n_total))]
    in_specs += [pl.BlockSpec(memory_space=pl.ANY)] * n_stacks
    out_specs = [pl.BlockSpec((tm, n), lambda i: (i, 0)) for n in splits]
    out_specs += [pl.BlockSpec((tm, H_A, DH_A), lambda i: (layer * nblk + i, 0, 0)) for _ in head_splits]
    out_shape = [jax.ShapeDtypeStruct((m, n), F32) for n in splits]
    out_shape += [jax.ShapeDtypeStruct((n_layers * m, H_A, DH_A), F32) for _ in head_splits]
    aliases = {3 + s: len(splits) + s for s in range(n_stacks)}
    return pl.pallas_call(
        body,
        grid=(nblk,),
        in_specs=in_specs,
        out_specs=out_specs,
        out_shape=out_shape,
        input_output_aliases=aliases,
        compiler_params=_cparams("parallel"),
        name="norm_proj",
    )(x2, wn, w, *(stacks or ()))


def _proj_mlp_body(*refs, n_in, tf):
    a_refs = refs[:n_in]
    w_refs = refs[n_in:2 * n_in]
    x_ref, wpost_ref, wpre_ref, wup_ref, wdown_ref, wpost2_ref, o_ref = refs[2 * n_in:]
    m = None
    for a_ref, w_ref in zip(a_refs, w_refs):
        t = jnp.dot(a_ref[...].astype(BF16), w_ref[...], preferred_element_type=F32)
        m = t if m is None else m + t
    x = x_ref[...] + _rms(m, wpost_ref[...])
    h = _rms(x, wpre_ref[...]).astype(BF16)
    acc = None
    for c in range(D_FF // tf):
        a = jnp.dot(h, wup_ref[:, c * tf:(c + 1) * tf], preferred_element_type=F32)
        a = jnp.maximum(a, 0.0)
        a = (a * a).astype(BF16)
        t = jnp.dot(a, wdown_ref[c * tf:(c + 1) * tf, :], preferred_element_type=F32)
        acc = t if acc is None else acc + t
    o_ref[...] = x + _rms(acc, wpost2_ref[...])


def _proj_mlp(acts, ws, x2, wpost, wpre, wup, wdown, wpost2, tm, tf=512):
    m = x2.shape[0]
    n_in = len(acts)
    body = functools.partial(_proj_mlp_body, n_in=n_in, tf=tf)
    in_specs = [pl.BlockSpec((tm, a.shape[1]), lambda i: (i, 0)) for a in acts]
    in_specs += [_const_spec(w.shape) for w in ws]
    in_specs += [pl.BlockSpec((tm, D_MODEL), lambda i: (i, 0)),
                 _const_spec((1, D_MODEL)), _const_spec((1, D_MODEL)),
                 _const_spec((D_MODEL, D_FF)), _const_spec((D_FF, D_MODEL)),
                 _const_spec((1, D_MODEL))]
    return pl.pallas_call(
        body,
        grid=(m // tm,),
        in_specs=in_specs,
        out_specs=pl.BlockSpec((tm, D_MODEL), lambda i: (i, 0)),
        out_shape=jax.ShapeDtypeStruct((m, D_MODEL), F32),
        compiler_params=_cparams("parallel"),
        name="proj_mlp",
    )(*acts, *ws, x2, wpost, wpre, wup, wdown, wpost2)


def _attn_prompt_body(slope_ref, q_ref, k_ref, v_ref, o_ref, *scr, seq):
    hp = pl.program_id(1)
    o_scr, m_scr, l_scr, bias_scr = scr[0:3], scr[3:6], scr[6:9], scr[9:12]
    qb = LANES
    n_units = seq // qb
    lane = lax.broadcasted_iota(jnp.int32, (qb, LANES), 1)
    head0 = lane < DH_A

    cfg = []
    for bi, (win, dil) in enumerate(BRANCHES):
        nqb = seq // dil // qb
        nk = 2 * qb if nqb >= 2 else qb
        rowi = lax.broadcasted_iota(jnp.int32, (2 * qb, nk), 0)
        coli = lax.broadcasted_iota(jnp.int32, (2 * qb, nk), 1)
        d = (nk - qb) + (rowi & (qb - 1)) - coli
        slope = jnp.where(rowi < qb, slope_ref[2 * hp], slope_ref[2 * hp + 1])
        bias_scr[bi][...] = jnp.where((d >= 0) & (d <= win // dil), -(slope * dil) * d.astype(F32), NEG_BIG)
        cfg.append((dil, nqb, nk))

    def step(u, carry):
        for bi, (dil, nqb, nk) in enumerate(cfg):
            def rows(start, dil=dil):
                if dil == 1:
                    return pl.ds(pl.multiple_of(start, qb), qb)
                return pl.ds(start, qb, stride=dil)

            r, i = (u, 0) if nqb == 1 else (u // nqb, u % nqb)
            cur = rows(r + dil * qb * i)
            q = q_ref[cur, :]
            qs = jnp.concatenate([jnp.where(head0, q, 0.0), jnp.where(head0, 0.0, q)], axis=0).astype(BF16)
            if nk == qb:
                kk = k_ref[cur, :].astype(BF16)
                vv = v_ref[cur, :].astype(BF16)
            else:
                prev = rows(r + dil * qb * jnp.maximum(i - 1, 0))
                kk = jnp.concatenate([k_ref[prev, :], k_ref[cur, :]], axis=0).astype(BF16)
                vv = jnp.concatenate([v_ref[prev, :], v_ref[cur, :]], axis=0).astype(BF16)
            s = lax.dot_general(qs, kk, (((1,), (1,)), ((), ())), preferred_element_type=F32)
            s = s + bias_scr[bi][...]
            if nk != qb:
                pen = jnp.where(i == 0, NEG_BIG, 0.0)
                s = jnp.concatenate([s[:, :qb] + pen, s[:, qb:]], axis=1)
            mx = jnp.max(s, axis=1, keepdims=True)
            p = jnp.exp(s - mx).astype(BF16)
            va = jnp.concatenate([vv, jnp.ones((nk, LANES), BF16)], axis=1)
            od = jnp.dot(p, va, preferred_element_type=F32)
            o_scr[bi][cur, :] = jnp.where(head0, od[:qb, :LANES], od[qb:, :LANES])
            m_scr[bi][cur, :] = jnp.where(head0, mx[:qb], mx[qb:])
            l_scr[bi][cur, :] = jnp.where(head0, od[:qb, LANES:], od[qb:, LANES:])
        return carry

    lax.fori_loop(0, n_units, step, 0, unroll=4)

    cb = 256

    def combine(c, carry):
        sl = pl.ds(pl.multiple_of(c * cb, cb), cb)
        ms = [m_scr[b][sl, :] for b in range(3)]
        mx = jnp.maximum(jnp.maximum(ms[0], ms[1]), ms[2])
        num = None
        den = None
        for b in range(3):
            w = jnp.exp(ms[b] - mx)
            tn = o_scr[b][sl, :] * w
            td = l_scr[b][sl, :] * w
            num = tn if num is None else num + tn
            den = td if den is None else den + td
        o_ref[sl, :] = (num / den).astype(o_ref.dtype)
        return carry

    lax.fori_loop(0, seq // cb, combine, 0)


def _attn_prompt(q, k, v, slopes):
    b, seq, _ = q.shape
    assert seq % (16 * LANES) == 0
    blk = pl.BlockSpec((None, seq, LANES), lambda bb, hp: (bb, 0, hp))
    body = functools.partial(_attn_prompt_body, seq=seq)
    return pl.pallas_call(
        body,
        grid=(b, D_A // LANES),
        in_specs=[pl.BlockSpec(memory_space=pltpu.SMEM), blk, blk, blk],
        out_specs=blk,
        out_shape=jax.ShapeDtypeStruct((b, seq, D_A), BF16),
        scratch_shapes=[pltpu.VMEM((seq, LANES), F32) for _ in range(9)]
        + [pltpu.VMEM((2 * LANES, 2 * LANES if seq // dil >= 2 * LANES else LANES), F32) for _, dil in BRANCHES],
        compiler_params=_cparams("parallel", "parallel"),
        name="attn_prompt",
    )(slopes, q, k, v)


def _attn_sample_body(slope_ref, q_ref, kn_ref, vn_ref, kb_ref, ka_ref, vb_ref, va_ref, o_ref,
                      bias_b, bias_a, bias_n, *, t_new, w_buf):
    rows = t_new * H_A
    far_dil = BRANCHES[-1][1]
    near = BRANCHES[-2][0]

    @pl.when(pl.program_id(0) == 0)
    def _():
        def table(ref, ncols, dist0):
            rowi = lax.broadcasted_iota(jnp.int32, (rows, ncols), 0)
            coli = lax.broadcasted_iota(jnp.int32, (rows, ncols), 1)
            hq = rowi & (H_A - 1)
            d = dist0(coli // H_A) + rowi // H_A
            mult = jnp.zeros((rows, ncols), jnp.int32)
            for win, dil in BRANCHES:
                mult = mult + jnp.where((d >= 0) & (d <= win) & ((d & (dil - 1)) == 0), 1, 0)
            slope = jnp.zeros((rows, ncols), F32)
            for h in range(H_A):
                slope = jnp.where(hq == h, slope_ref[h], slope)
            logm = jnp.where(mult == 3, math.log(3.0), jnp.where(mult == 2, math.log(2.0), 0.0))
            ok = (hq == (coli & (H_A - 1))) & (mult > 0)
            ref[...] = jnp.where(ok, logm - slope * d.astype(F32), NEG_BIG)

        table(bias_b, bias_b.shape[1], lambda tok: w_buf - (far_dil * (tok // t_new) + (tok & (t_new - 1))))
        table(bias_a, bias_a.shape[1], lambda tok: near - tok)
        table(bias_n, bias_n.shape[1], lambda tok: -tok)

    q = q_ref[...].reshape(-1, DH_A).astype(BF16)
    keys = tuple(r[...].reshape(-1, DH_A) for r in (kb_ref, ka_ref, kn_ref))
    vals = tuple(r[...].reshape(-1, DH_A) for r in (vb_ref, va_ref, vn_ref))
    ss = []
    for kk, b_ref in zip(keys, (bias_b, bias_a, bias_n)):
        s = lax.dot_general(q, kk.astype(BF16), (((1,), (1,)), ((), ())), preferred_element_type=F32)
        ss.append(s + b_ref[...])
    mx = None
    for s in ss:
        m1 = jnp.max(s, axis=1, keepdims=True)
        mx = m1 if mx is None else jnp.maximum(mx, m1)
    o = None
    den = None
    for s, vv in zip(ss, vals):
        p = jnp.exp(s - mx)
        d1 = jnp.sum(p, axis=1, keepdims=True)
        o1 = jnp.dot(p.astype(BF16), vv.astype(BF16), preferred_element_type=F32)
        den = d1 if den is None else den + d1
        o = o1 if o is None else o + o1
    o_ref[...] = (o / den).reshape(t_new, H_A, DH_A)


def _attn_sample(q, kn, vn, cache_k, cache_v, layer, slopes):
    b, t_new, _ = q.shape
    n_layers, _, w_buf = cache_k.shape[:3]
    far_dil = BRANCHES[-1][1]
    near = BRANCHES[-2][0]
    assert w_buf == WINDOW and t_new == SUBLANES and w_buf % far_dil == 0 and (w_buf - near) % near == 0
    n_far = (w_buf - near) // far_dil
    far_view = (n_layers, b, w_buf // far_dil, far_dil, H_A, DH_A)
    near_view = (n_layers, b, w_buf // near, near, H_A, DH_A)
    far_blk = pl.BlockSpec((None, None, n_far, t_new, H_A, DH_A), lambda bb: (layer, bb, 0, 0, 0, 0))
    near_blk = pl.BlockSpec((None, None, None, near, H_A, DH_A), lambda bb: (layer, bb, w_buf // near - 1, 0, 0, 0))
    new_blk = pl.BlockSpec((None, t_new, H_A, DH_A), lambda bb: (bb, 0, 0, 0))
    q, kn, vn = (t.reshape(b, t_new, H_A, DH_A) for t in (q, kn, vn))
    rows = t_new * H_A
    body = functools.partial(_attn_sample_body, t_new=t_new, w_buf=w_buf)
    out = pl.pallas_call(
        body,
        grid=(b,),
        in_specs=[pl.BlockSpec(memory_space=pltpu.SMEM), new_blk, new_blk, new_blk,
                  far_blk, near_blk, far_blk, near_blk],
        out_specs=pl.BlockSpec((None, t_new, H_A, DH_A), lambda bb: (bb, 0, 0, 0)),
        out_shape=jax.ShapeDtypeStruct((b, t_new, H_A, DH_A), F32),
        scratch_shapes=[pltpu.VMEM((rows, n_far * t_new * H_A), F32),
                        pltpu.VMEM((rows, near * H_A), F32),
                        pltpu.VMEM((rows, t_new * H_A), F32)],
        compiler_params=_cparams("arbitrary"),
        name="attn_sample",
    )(slopes, q, kn, vn, cache_k.reshape(far_view), cache_k.reshape(near_view),
      cache_v.reshape(far_view), cache_v.reshape(near_view))
    return out.reshape(b, t_new, D_A)


S5_UT = D_B // LANES
S5_TW = S5_N // S5_UT


def _gelu_tanh(x):
    c = math.sqrt(2.0 / math.pi)
    return 0.5 * x * (1.0 + jnp.tanh(c * (x + 0.044715 * (x * x * x))))


def _s5_body(u_ref, h0_ref, perm_ref, permt_ref, wb_ref, lam_ref, wc_ref, dskip_ref, wglu_ref, bglu_ref,
             o_ref, hl_ref, bu_scr, up_scr, st_scr, *, nb, tsub, nsub):
    step = pl.program_id(0)
    rsub = nb * tsub
    tstep = tsub * nsub

    @pl.when(step == 0)
    def _():
        st_scr[...] = h0_ref[...]

    for sb in range(nsub):
        u_nat = u_ref[:, sb * tsub:(sb + 1) * tsub, :].reshape(rsub, D_B)
        up_scr[sb * rsub:(sb + 1) * rsub, :] = jnp.dot(
            perm_ref[...], u_nat, preferred_element_type=F32, precision=lax.Precision.HIGHEST)
    for j in range(S5_UT):
        uj = up_scr[:, j * LANES:(j + 1) * LANES].astype(BF16)
        r = jnp.dot(uj, wb_ref[j], preferred_element_type=F32)
        bu_scr[:, j * S5_TW:(j + 1) * S5_TW] = r[:, :S5_TW]
        bu_scr[:, S5_N + j * S5_TW:S5_N + (j + 1) * S5_TW] = r[:, S5_TW:]

    tiles_per_pass = 4
    for pg in range(S5_N // LANES // tiles_per_pass):
        cols = [(pg * tiles_per_pass + i) * LANES for i in range(tiles_per_pass)]
        lr = [jnp.broadcast_to(lam_ref[0:1, c:c + LANES], (nb, LANES)) for c in cols]
        li = [jnp.broadcast_to(lam_ref[1:2, c:c + LANES], (nb, LANES)) for c in cols]
        init = tuple(st_scr[:, c:c + LANES] for c in cols) + tuple(st_scr[:, S5_N + c:S5_N + c + LANES] for c in cols)

        def tick(t, carry, cols=cols, lr=lr, li=li):
            row = pl.ds(pl.multiple_of(t * nb, nb), nb)
            hr, hi = carry[:tiles_per_pass], carry[tiles_per_pass:]
            nr, ni = [], []
            for i, c in enumerate(cols):
                br = bu_scr[row, c:c + LANES]
                bi = bu_scr[row, S5_N + c:S5_N + c + LANES]
                r_new = lr[i] * hr[i] - li[i] * hi[i] + br
                i_new = lr[i] * hi[i] + li[i] * hr[i] + bi
                bu_scr[row, c:c + LANES] = r_new
                bu_scr[row, S5_N + c:S5_N + c + LANES] = i_new
                nr.append(r_new)
                ni.append(i_new)
            return tuple(nr) + tuple(ni)

        fin = lax.fori_loop(0, tstep, tick, init)
        for i, c in enumerate(cols):
            st_scr[:, c:c + LANES] = fin[i]
            st_scr[:, S5_N + c:S5_N + c + LANES] = fin[tiles_per_pass + i]

    hl_ref[...] = st_scr[...]

    ys = []
    for j in range(S5_UT):
        hre = bu_scr[:, j * S5_TW:(j + 1) * S5_TW].astype(BF16)
        him = bu_scr[:, S5_N + j * S5_TW:S5_N + (j + 1) * S5_TW].astype(BF16)
        y = jnp.dot(hre, wc_ref[j, :S5_TW, :], preferred_element_type=F32)
        y = y + jnp.dot(him, wc_ref[j, S5_TW:, :], preferred_element_type=F32)
        ys.append(y)
    y = jnp.concatenate(ys, axis=1) + dskip_ref[...] * up_scr[...]
    g = _gelu_tanh(y)
    gate = jnp.dot(g.astype(BF16), wglu_ref[...], preferred_element_type=F32) + bglu_ref[...]
    out = (g * jax.nn.sigmoid(gate)).astype(BF16)
    for sb in range(nsub):
        o_nat = jnp.dot(permt_ref[...], out[sb * rsub:(sb + 1) * rsub, :], preferred_element_type=F32)
        o_ref[:, sb * tsub:(sb + 1) * tsub, :] = o_nat.reshape(nb, tsub, D_B).astype(o_ref.dtype)


def _s5_mixer(u, h0, prm, tsub, nsub):
    nb, seq, _ = u.shape
    tstep = tsub * nsub
    rsub = nb * tsub
    rows = nb * tstep
    assert seq % tstep == 0 and nb % SUBLANES == 0 and tsub % SUBLANES == 0
    perm = np.zeros((rsub, rsub), np.float32)
    for t in range(tsub):
        for bb in range(nb):
            perm[t * nb + bb, bb * tsub + t] = 1.0
    body = functools.partial(_s5_body, nb=nb, tsub=tsub, nsub=nsub)
    return pl.pallas_call(
        body,
        grid=(seq // tstep,),
        in_specs=[pl.BlockSpec((nb, tstep, D_B), lambda i: (0, i, 0)),
                  _const_spec((nb, 2 * S5_N)),
                  _const_spec((rsub, rsub)), _const_spec((rsub, rsub)),
                  _const_spec((S5_UT, LANES, 2 * S5_TW)),
                  _const_spec((2, S5_N)),
                  _const_spec((S5_UT, 2 * S5_TW, LANES)),
                  _const_spec((1, D_B)),
                  _const_spec((D_B, D_B)),
                  _const_spec((1, D_B))],
        out_specs=[pl.BlockSpec((nb, tstep, D_B), lambda i: (0, i, 0)),
                   pl.BlockSpec((nb, 2 * S5_N), lambda i: (0, 0))],
        out_shape=[jax.ShapeDtypeStruct((nb, seq, D_B), _act_dtype(tsub)),
                   jax.ShapeDtypeStruct((nb, 2 * S5_N), F32)],
        scratch_shapes=[pltpu.VMEM((rows, 2 * S5_N), F32),
                        pltpu.VMEM((rows, D_B), F32),
                        pltpu.VMEM((nb, 2 * S5_N), F32)],
        compiler_params=_cparams("arbitrary"),
        name="s5_mixer",
    )(u, h0, jnp.asarray(perm, F32), jnp.asarray(perm.T, BF16), prm["wb"], prm["lam"], prm["wc"],
      prm["dskip"], prm["wglu"], prm["bglu"])


def _s5_params(lam_re, lam_im, log_dt, b_re, b_im, c_re, c_im, d_skip, w_glu, b_glu):
    lam = lax.complex(lam_re.astype(F32), lam_im.astype(F32))
    dt = jnp.exp(log_dt.astype(F32))[:, None]
    lam_bar = jnp.exp(lam * dt)
    b_bar = ((lam_bar - 1.0) / lam)[..., None] * lax.complex(b_re.astype(F32), b_im.astype(F32))
    gpt = LANES // CH_B
    eye = jnp.eye(gpt, dtype=F32)

    def b_blocks(part):
        x = part.reshape(S5_UT, gpt, P_B, CH_B)
        x = jnp.einsum("jgpc,gh->jgchp", x, eye)
        return x.reshape(S5_UT, gpt * CH_B, gpt * P_B)

    def c_blocks(part):
        x = part.reshape(S5_UT, gpt, CH_B, P_B)
        x = jnp.einsum("jgcp,gh->jgphc", x, eye)
        return x.reshape(S5_UT, gpt * P_B, gpt * CH_B)

    wb = jnp.concatenate([b_blocks(b_bar.real), b_blocks(b_bar.imag)], axis=2).astype(BF16)
    wc = jnp.concatenate([c_blocks(c_re.astype(F32)), -c_blocks(c_im.astype(F32))], axis=1).astype(BF16)
    lam2 = jnp.stack([lam_bar.real.reshape(S5_N), lam_bar.imag.reshape(S5_N)])
    return dict(wb=wb, wc=wc, lam=lam2, dskip=d_skip.astype(F32).reshape(1, D_B),
                wglu=w_glu.astype(BF16), bglu=b_glu.astype(F32).reshape(1, D_B))


HIST = SUBLANES
XB0 = D_INNER
XC0 = D_INNER + G_C * N_C
GW = D_INNER // G_C


def _exact_expand(a, e_ref):
    a1 = a.astype(BF16)
    r1 = a - a1.astype(F32)
    a2 = r1.astype(BF16)
    a3 = (r1 - a2.astype(F32)).astype(BF16)
    e = e_ref[...]
    out = jnp.dot(a1, e, preferred_element_type=F32)
    out = out + jnp.dot(a2, e, preferred_element_type=F32)
    return out + jnp.dot(a3, e, preferred_element_type=F32)


def _ssd_body(z_ref, xbc_ref, dt_ref, conv0_ref, ssm0_ref, convw_ref, convb_ref, dtb_ref, a_ref, dexp_ref,
              gw_ref, e_ref, *rest, lc):
    y_ref, convst_ref, hl_ref, xpad, state = rest[-5:]
    ch = pl.program_id(1)
    cl = SSD_CHUNK

    @pl.when(ch == 0)
    def _():
        xpad[0:HIST, :] = conv0_ref[...]
        state[...] = ssm0_ref[...]

    if lc < cl:
        xpad[HIST:, :] = jnp.zeros((cl, CONV_DIM), F32)
    xpad[HIST:HIST + lc, :] = xbc_ref[...]
    conv = convb_ref[...] + convw_ref[CONV_K - 1:CONV_K, :] * xpad[HIST:HIST + cl, :]
    for kk in range(CONV_K - 1):
        sh = CONV_K - 1 - kk
        conv = conv + convw_ref[kk:kk + 1, :] * xpad[HIST - sh:HIST - sh + cl, :]
    xs = conv * jax.nn.sigmoid(conv)
    tail = xpad[lc:lc + HIST, :]
    xpad[0:HIST, :] = tail
    convst_ref[...] = tail

    dt = jax.nn.softplus(dt_ref[...] + dtb_ref[...])
    if lc < cl:
        dt = jnp.concatenate([dt, jnp.zeros((cl - lc, LANES), F32)], axis=0)
    da = dt * a_ref[...]
    ri = lax.broadcasted_iota(jnp.int32, (cl, cl), 0)
    ci = lax.broadcasted_iota(jnp.int32, (cl, cl), 1)
    tri = ri >= ci
    a_cs = jnp.dot(jnp.where(tri, 1.0, 0.0), da, preferred_element_type=F32, precision=lax.Precision.HIGHEST)
    a_cs_t = a_cs.T
    dt_t = dt.T
    e_cs = jnp.exp(a_cs)
    tot = a_cs[cl - 1:cl, :]
    w_end = jnp.exp(tot - a_cs) * dt
    scale_x = _exact_expand(e_cs, e_ref)
    wend_x = _exact_expand(w_end, e_ref)

    x = xs[:, :D_INNER]
    xw = (x * wend_x).astype(BF16)
    lane = lax.broadcasted_iota(jnp.int32, (cl, LANES), 1)
    head0 = lane < P_C
    y_parts = []
    for g in range(G_C):
        bg = xs[:, XB0 + g * N_C:XB0 + (g + 1) * N_C].astype(BF16)
        cg = xs[:, XC0 + g * N_C:XC0 + (g + 1) * N_C].astype(BF16)
        cb = lax.dot_general(cg, bg, (((1,), (1,)), ((), ())), preferred_element_type=F32)
        prev = state[g * GW:(g + 1) * GW, :]
        y_off = lax.dot_general(cg, prev.astype(BF16), (((1,), (1,)), ((), ())), preferred_element_type=F32)
        for jp in range(J_C // 2):
            ms = []
            for hh in range(2):
                h = g * J_C + 2 * jp + hh
                seg = a_cs[:, h:h + 1] - a_cs_t[h:h + 1, :]
                dec = jnp.exp(jnp.where(tri, seg, NEG_BIG))
                ms.append((cb * dec * dt_t[h:h + 1, :]).astype(BF16))
            c0 = g * GW + jp * LANES
            xp = x[:, c0:c0 + LANES].astype(BF16)
            r = jnp.dot(jnp.concatenate(ms, axis=0), xp, preferred_element_type=F32)
            yd = jnp.where(head0, r[:cl], r[cl:])
            y_parts.append(yd + y_off[:, jp * LANES:(jp + 1) * LANES] * scale_x[:, c0:c0 + LANES])
        new = lax.dot_general(xw[:, g * GW:(g + 1) * GW], bg, (((0,), (0,)), ((), ())),
                              preferred_element_type=F32)
        for j in range(J_C):
            h = g * J_C + j
            r0 = g * GW + j * P_C
            state[r0:r0 + P_C, :] = state[r0:r0 + P_C, :] * jnp.exp(tot[:, h:h + 1]) + new[j * P_C:(j + 1) * P_C, :]
    hl_ref[...] = state[...]

    y = jnp.concatenate(y_parts, axis=1) + dexp_ref[...] * x
    y = y[:lc] if lc < cl else y
    z = z_ref[...]
    gt = y * (z * jax.nn.sigmoid(z))
    outs = []
    for g in range(G_C):
        gg = gt[:, g * GW:(g + 1) * GW]
        ms = jnp.mean(gg * gg, axis=-1, keepdims=True)
        outs.append(gg * lax.rsqrt(ms + RMS_EPS))
    y_ref[...] = (jnp.concatenate(outs, axis=1) * gw_ref[...]).astype(y_ref.dtype)


def _ssd_mixer(z, xbc, dtr, conv0, ssm0, prm, stack=None, layer=0, n_layers=1):
    b, seq, _ = z.shape
    lc = SSD_CHUNK if seq % SSD_CHUNK == 0 else seq
    assert lc % SUBLANES == 0 and CONV_K - 1 <= lc <= SSD_CHUNK
    nch = seq // lc
    body = functools.partial(_ssd_body, lc=lc)

    def row_blk(n):
        return pl.BlockSpec((None, lc, n), lambda bb, c: (bb, c, 0))

    def per_b(r, n):
        return pl.BlockSpec((None, r, n), lambda bb, c: (bb, 0, 0))

    return pl.pallas_call(
        body,
        grid=(b, nch),
        in_specs=[row_blk(D_INNER), row_blk(CONV_DIM), row_blk(LANES),
                  per_b(HIST, CONV_DIM), per_b(D_INNER, N_C),
                  _const_spec((CONV_K, CONV_DIM)), _const_spec((1, CONV_DIM)),
                  _const_spec((1, LANES)), _const_spec((1, LANES)),
                  _const_spec((1, D_INNER)), _const_spec((1, D_INNER)),
                  _const_spec((LANES, D_INNER))] + ([] if stack is None else [pl.BlockSpec(memory_space=pl.ANY)]),
        out_specs=[row_blk(D_INNER), per_b(HIST, CONV_DIM),
                   pl.BlockSpec((None, D_INNER, N_C), lambda bb, c: (layer * b + bb, 0, 0))],
        out_shape=[jax.ShapeDtypeStruct((b, seq, D_INNER), _act_dtype(lc)),
                   jax.ShapeDtypeStruct((b, HIST, CONV_DIM), F32),
                   jax.ShapeDtypeStruct((n_layers * b, D_INNER, N_C), F32)],
        scratch_shapes=[pltpu.VMEM((HIST + SSD_CHUNK, CONV_DIM), F32),
                        pltpu.VMEM((D_INNER, N_C), F32)],
        input_output_aliases={} if stack is None else {12: 2},
        compiler_params=_cparams("parallel", "arbitrary"),
        name="ssd_mixer",
    )(z, xbc, dtr, conv0, ssm0, prm["convw"], prm["convb"], prm["dtb"], prm["a"], prm["dexp"],
      prm["gw"], prm["expand"], *(() if stack is None else (stack,)))


def _ssd_params(conv_w, conv_b, dt_bias, a_log, d_skip, gnorm_w):
    pad = LANES - H_C
    expand = np.zeros((LANES, D_INNER), np.float32)
    for h in range(H_C):
        expand[h, h * P_C:(h + 1) * P_C] = 1.0
    return dict(
        convw=conv_w.astype(F32), convb=conv_b.astype(F32).reshape(1, CONV_DIM),
        dtb=jnp.pad(dt_bias.astype(F32), (0, pad)).reshape(1, LANES),
        a=jnp.pad(-jnp.exp(a_log.astype(F32)), (0, pad)).reshape(1, LANES),
        dexp=jnp.repeat(d_skip.astype(F32), P_C).reshape(1, D_INNER),
        gw=gnorm_w.astype(F32).reshape(1, D_INNER),
        expand=jnp.asarray(expand, BF16))


def _trunk(x, p, slopes, caches, tm):
    b, seq, _ = x.shape
    m = b * seq
    x2 = x.reshape(m, D_MODEL)
    new_s5, new_conv = [], []
    kv_stacks, ssm_stack = None, None
    n_even, n_odd = (DEPTH + 1) // 2, DEPTH // 2
    for i in range(DEPTH):
        j = i // 2
        wn_pre = p["norm_mix_pre"][i].reshape(1, D_MODEL)
        if i % 2 == 0:
            assert seq <= WINDOW
            q, k, v, u, *kv_stacks = _norm_proj(x2, wn_pre, p["w_in_even"][j], (D_A, D_A, D_A, D_B),
                                                (DH_A ** -0.5, 1.0, 1.0, 1.0), tm, head_splits=(1, 2),
                                                stacks=kv_stacks, layer=j, n_layers=n_even)
            q3, k3, v3 = (t.reshape(b, seq, D_A) for t in (q, k, v))
            if caches is None:
                o_a = _attn_prompt(q3, k3, v3, slopes)
                h0 = jnp.zeros((b, 2 * S5_N), F32)
                tsub, nsub = 16, 4
            else:
                o_a = _attn_sample(q3, k3, v3, caches["k"], caches["v"], j, slopes)
                s0 = caches["s5"][j].astype(F32)
                h0 = jnp.concatenate([s0[..., 0].reshape(b, S5_N), s0[..., 1].reshape(b, S5_N)], axis=1)
                tsub, nsub = seq, 1
            o_b, hl = _s5_mixer(u.reshape(b, seq, D_B), h0, p["s5"][j], tsub, nsub)
            new_s5.append(jnp.stack([hl[:, :S5_N].reshape(b, G_B, P_B), hl[:, S5_N:].reshape(b, G_B, P_B)], axis=-1))
            acts = [o_a.reshape(m, D_A), o_b.reshape(m, D_B)]
            ws = [p["w_out_even"][j][:D_A], p["w_out_even"][j][D_A:]]
        else:
            z, xbc, dtr = _norm_proj(x2, wn_pre, p["w_in_odd"][j], (D_INNER, CONV_DIM, LANES),
                                     (1.0, 1.0, 1.0), tm)
            if caches is None:
                conv0 = jnp.zeros((b, HIST, CONV_DIM), F32)
                ssm0 = jnp.zeros((b, D_INNER, N_C), F32)
            else:
                conv0 = jnp.pad(caches["conv"][j].astype(F32), ((0, 0), (HIST - (CONV_K - 1), 0), (0, 0)))
                ssm0 = caches["ssm"][j].astype(F32).reshape(b, D_INNER, N_C)
            yg, convst, ssm_stack = _ssd_mixer(z.reshape(b, seq, D_INNER), xbc.reshape(b, seq, CONV_DIM),
                                               dtr.reshape(b, seq, LANES), conv0, ssm0, p["ssd"][j],
                                               stack=ssm_stack, layer=j, n_layers=n_odd)
            new_conv.append(convst[:, HIST - (CONV_K - 1):])
            acts = [yg.reshape(m, D_INNER)]
            ws = [p["w_out_odd"][j]]
        x2 = _proj_mlp(acts, ws, x2, p["norm_mix_post"][i].reshape(1, D_MODEL),
                       p["norm_mlp_pre"][i].reshape(1, D_MODEL), p["w_mlp_up"][i], p["w_mlp_down"][i],
                       p["norm_mlp_post"][i].reshape(1, D_MODEL), tm)
    k_all, v_all = (t.reshape(n_even, b, seq, H_A, DH_A) for t in kv_stacks)
    return (x2.reshape(b, seq, D_MODEL), k_all, v_all, jnp.stack(new_s5), jnp.stack(new_conv),
            ssm_stack.reshape(n_odd, b, H_C, P_C, N_C))


def kernel(x_prompt, x_sample, cache_k, cache_v, state_s5, state_conv, state_ssm, norm_mix_pre, norm_mix_post, norm_mlp_pre, norm_mlp_post, w_mlp_up, w_mlp_down, w_in_even, w_out_even, s5_lambda_re, s5_lambda_im, s5_log_dt, s5_b_re, s5_b_im, s5_c_re, s5_c_im, s5_d, s5_w_glu, s5_b_glu, w_in_odd, conv_w, conv_b, dt_bias, a_log, d_skip, gnorm_w, w_out_odd):
    n_even, n_odd = w_in_even.shape[0], w_in_odd.shape[0]
    odd_pad = LANES - H_C
    p = dict(
        norm_mix_pre=norm_mix_pre.astype(F32), norm_mix_post=norm_mix_post.astype(F32),
        norm_mlp_pre=norm_mlp_pre.astype(F32), norm_mlp_post=norm_mlp_post.astype(F32),
        w_mlp_up=w_mlp_up.astype(BF16), w_mlp_down=w_mlp_down.astype(BF16),
        w_in_even=w_in_even.astype(BF16), w_out_even=w_out_even.astype(BF16),
        w_in_odd=jnp.pad(w_in_odd, ((0, 0), (0, 0), (0, odd_pad))).astype(BF16),
        w_out_odd=w_out_odd.astype(BF16),
        s5=[_s5_params(s5_lambda_re[j], s5_lambda_im[j], s5_log_dt[j], s5_b_re[j], s5_b_im[j], s5_c_re[j],
                       s5_c_im[j], s5_d[j], s5_w_glu[j], s5_b_glu[j]) for j in range(n_even)],
        ssd=[_ssd_params(conv_w[j], conv_b[j], dt_bias[j], a_log[j], d_skip[j], gnorm_w[j]) for j in range(n_odd)],
    )
    slopes = jnp.asarray(np.power(2.0, -8.0 * np.arange(1, H_A + 1) / H_A), dtype=F32)
    caches = dict(k=cache_k, v=cache_v, s5=state_s5, conv=state_conv, ssm=state_ssm)
    y_p, k_p, v_p, s5_p, conv_p, ssm_p = _trunk(x_prompt, p, slopes, None, tm=512)
    y_s, k_s, v_s, s5_s, conv_s, ssm_s = _trunk(x_sample, p, slopes, caches, tm=256)
    return (y_p, y_s, k_p, v_p, s5_p, conv_p, ssm_p, k_s, v_s, s5_s, conv_s, ssm_s)
```

```python
import functools
import math

import numpy as np
import jax
import jax.numpy as jnp
from jax import lax
from jax.experimental import pallas as pl
from jax.experimental.pallas import tpu as pltpu

F32 = jnp.float32
BF16 = jnp.bfloat16

D_MODEL = 1024
DEPTH = 4
PAST_LEN = 8192
RMS_EPS = 1e-6
D_FF = 4 * D_MODEL
D_A = D_MODEL // 2
DH_A = 64
H_A = D_A // DH_A
BRANCHES = ((128, 1), (512, 4), (2048, 16))
WINDOW = 2048
D_B = D_MODEL - D_A
CH_B = 16
G_B = D_B // CH_B
P_B = 64
S5_N = G_B * P_B
D_INNER = 2 * D_MODEL
P_C = 64
H_C = D_INNER // P_C
G_C = 4
J_C = H_C // G_C
N_C = 128
CONV_K = 4
CONV_DIM = D_INNER + 2 * G_C * N_C
SSD_CHUNK = 128

LANES = 128
SUBLANES = 8
VMEM_LIMIT = 56 * 1024 * 1024
NEG_BIG = -1e30


def _act_dtype(rows):
    return BF16 if rows % (2 * SUBLANES) == 0 else F32


def _cparams(*sem):
    return pltpu.CompilerParams(dimension_semantics=sem, vmem_limit_bytes=VMEM_LIMIT)


def _rms(x, w):
    ms = jnp.mean(x * x, axis=-1, keepdims=True)
    return x * lax.rsqrt(ms + RMS_EPS) * w


def _const_spec(shape):
    nd = len(shape)
    return pl.BlockSpec(shape, lambda *_: (0,) * nd, pipeline_mode=pl.Buffered(1))


def _norm_proj_body(*refs, splits, scales, head_splits, n_stacks, tokens_minor):
    x_ref, wn_ref, w_ref = refs[:3]
    out_refs = refs[3 + n_stacks:3 + n_stacks + len(splits)]
    head_refs = refs[3 + n_stacks + len(splits):]
    h = _rms(x_ref[...], wn_ref[...]).astype(BF16)
    off = 0
    for idx, (o_ref, n, sc) in enumerate(zip(out_refs, splits, scales)):
        r = jnp.dot(h, w_ref[:, off:off + n], preferred_element_type=F32)
        if sc != 1.0:
            r = r * sc
        o_ref[...] = r.astype(o_ref.dtype)
        if idx in head_splits:
            h_ref = head_refs[head_splits.index(idx)]
            if tokens_minor:
                h_ref[...] = r.T.reshape(H_A, DH_A, r.shape[0])
            else:
                h_ref[...] = r.reshape(r.shape[0], H_A, DH_A)
        off += n


def _norm_proj(x2, wn, w, splits, scales, tm, head_splits=(), stacks=None, layer=0, n_layers=1, seq=None):
    m = x2.shape[0]
    n_total = w.shape[1]
    assert sum(splits) == n_total and m % tm == 0
    nblk = m // tm
    n_stacks = 0 if stacks is None else len(stacks)
    tokens_minor = seq is not None
    body = functools.partial(_norm_proj_body, splits=splits, scales=scales, head_splits=tuple(head_splits),
                             n_stacks=n_stacks, tokens_minor=tokens_minor)
    in_specs = [pl.BlockSpec((tm, D_MODEL), lambda i: (i, 0)),
                _const_spec((1, D_MODEL)),
                _const_spec((D_MODEL, n_total))]
    in_specs += [pl.BlockSpec(memory_space=pl.ANY)] * n_stacks
    out_specs = [pl.BlockSpec((tm, n), lambda i: (i, 0)) for n in splits]
    out_shape = [jax.ShapeDtypeStruct((m, n), F32) for n in splits]
    if tokens_minor:
        assert seq % tm == 0 and tm % LANES == 0
        per_seq = seq // tm
        out_specs += [pl.BlockSpec((None, None, H_A, DH_A, tm),
                                   lambda i: (layer, i // per_seq, 0, 0, i % per_seq)) for _ in head_splits]
        out_shape += [jax.ShapeDtypeStruct((n_layers, m // seq, H_A, DH_A, seq), F32) for _ in head_splits]
    else:
        out_specs += [pl.BlockSpec((tm, H_A, DH_A), lambda i: (layer * nblk + i, 0, 0)) for _ in head_splits]
        out_shape += [jax.ShapeDtypeStruct((n_layers * m, H_A, DH_A), F32) for _ in head_splits]
    aliases = {3 + s: len(splits) + s for s in range(n_stacks)}
    return pl.pallas_call(
        body,
        grid=(nblk,),
        in_specs=in_specs,
        out_specs=out_specs,
        out_shape=out_shape,
        input_output_aliases=aliases,
        compiler_params=_cparams("parallel"),
        name="norm_proj",
    )(x2, wn, w, *(stacks or ()))


def _proj_mlp_body(*refs, n_in, tf):
    a_refs = refs[:n_in]
    w_refs = refs[n_in:2 * n_in]
    x_ref, wpost_ref, wpre_ref, wup_ref, wdown_ref, wpost2_ref, o_ref = refs[2 * n_in:]
    m = None
    for a_ref, w_ref in zip(a_refs, w_refs):
        t = jnp.dot(a_ref[...].astype(BF16), w_ref[...], preferred_element_type=F32)
        m = t if m is None else m + t
    x = x_ref[...] + _rms(m, wpost_ref[...])
    h = _rms(x, wpre_ref[...]).astype(BF16)
    acc = None
    for c in range(D_FF // tf):
        a = jnp.dot(h, wup_ref[:, c * tf:(c + 1) * tf], preferred_element_type=F32)
        a = jnp.maximum(a, 0.0)
        a = (a * a).astype(BF16)
        t = jnp.dot(a, wdown_ref[c * tf:(c + 1) * tf, :], preferred_element_type=F32)
        acc = t if acc is None else acc + t
    o_ref[...] = x + _rms(acc, wpost2_ref[...])


def _proj_mlp(acts, ws, x2, wpost, wpre, wup, wdown, wpost2, tm, tf=512):
    m = x2.shape[0]
    n_in = len(acts)
    body = functools.partial(_proj_mlp_body, n_in=n_in, tf=tf)
    in_specs = [pl.BlockSpec((tm, a.shape[1]), lambda i: (i, 0)) for a in acts]
    in_specs += [_const_spec(w.shape) for w in ws]
    in_specs += [pl.BlockSpec((tm, D_MODEL), lambda i: (i, 0)),
                 _const_spec((1, D_MODEL)), _const_spec((1, D_MODEL)),
                 _const_spec((D_MODEL, D_FF)), _const_spec((D_FF, D_MODEL)),
                 _const_spec((1, D_MODEL))]
    return pl.pallas_call(
        body,
        grid=(m // tm,),
        in_specs=in_specs,
        out_specs=pl.BlockSpec((tm, D_MODEL), lambda i: (i, 0)),
        out_shape=jax.ShapeDtypeStruct((m, D_MODEL), F32),
        compiler_params=_cparams("parallel"),
        name="proj_mlp",
    )(*acts, *ws, x2, wpost, wpre, wup, wdown, wpost2)


def _attn_prompt_body(slope_ref, q_ref, k_ref, v_ref, o_ref, *scr, seq):
    hp = pl.program_id(1)
    o_scr, m_scr, l_scr, bias_scr = scr[0:3], scr[3:6], scr[6:9], scr[9:12]
    qb = LANES
    n_units = seq // qb
    lane = lax.broadcasted_iota(jnp.int32, (qb, LANES), 1)
    head0 = lane < DH_A

    cfg = []
    for bi, (win, dil) in enumerate(BRANCHES):
        nqb = seq // dil // qb
        nk = 2 * qb if nqb >= 2 else qb
        rowi = lax.broadcasted_iota(jnp.int32, (2 * qb, nk), 0)
        coli = lax.broadcasted_iota(jnp.int32, (2 * qb, nk), 1)
        d = (nk - qb) + (rowi & (qb - 1)) - coli
        slope = jnp.where(rowi < qb, slope_ref[2 * hp], slope_ref[2 * hp + 1])
        bias_scr[bi][...] = jnp.where((d >= 0) & (d <= win // dil), -(slope * dil) * d.astype(F32), NEG_BIG)
        cfg.append((dil, nqb, nk))

    def step(u, carry):
        for bi, (dil, nqb, nk) in enumerate(cfg):
            def rows(start, dil=dil):
                if dil == 1:
                    return pl.ds(pl.multiple_of(start, qb), qb)
                return pl.ds(start, qb, stride=dil)

            r, i = (u, 0) if nqb == 1 else (u // nqb, u % nqb)
            cur = rows(r + dil * qb * i)
            q = q_ref[cur, :]
            qs = jnp.concatenate([jnp.where(head0, q, 0.0), jnp.where(head0, 0.0, q)], axis=0).astype(BF16)
            if nk == qb:
                kk = k_ref[cur, :].astype(BF16)
                vv = v_ref[cur, :].astype(BF16)
            else:
                prev = rows(r + dil * qb * jnp.maximum(i - 1, 0))
                kk = jnp.concatenate([k_ref[prev, :], k_ref[cur, :]], axis=0).astype(BF16)
                vv = jnp.concatenate([v_ref[prev, :], v_ref[cur, :]], axis=0).astype(BF16)
            s = lax.dot_general(qs, kk, (((1,), (1,)), ((), ())), preferred_element_type=F32)
            s = s + bias_scr[bi][...]
            if nk != qb:
                pen = jnp.where(i == 0, NEG_BIG, 0.0)
                s = jnp.concatenate([s[:, :qb] + pen, s[:, qb:]], axis=1)
            mx = jnp.max(s, axis=1, keepdims=True)
            p = jnp.exp(s - mx).astype(BF16)
            va = jnp.concatenate([vv, jnp.ones((nk, LANES), BF16)], axis=1)
            od = jnp.dot(p, va, preferred_element_type=F32)
            o_scr[bi][cur, :] = jnp.where(head0, od[:qb, :LANES], od[qb:, :LANES])
            m_scr[bi][cur, :] = jnp.where(head0, mx[:qb], mx[qb:])
            l_scr[bi][cur, :] = jnp.where(head0, od[:qb, LANES:], od[qb:, LANES:])
        return carry

    lax.fori_loop(0, n_units, step, 0, unroll=4)

    cb = 256

    def combine(c, carry):
        sl = pl.ds(pl.multiple_of(c * cb, cb), cb)
        ms = [m_scr[b][sl, :] for b in range(3)]
        mx = jnp.maximum(jnp.maximum(ms[0], ms[1]), ms[2])
        num = None
        den = None
        for b in range(3):
            w = jnp.exp(ms[b] - mx)
            tn = o_scr[b][sl, :] * w
            td = l_scr[b][sl, :] * w
            num = tn if num is None else num + tn
            den = td if den is None else den + td
        o_ref[sl, :] = (num / den).astype(o_ref.dtype)
        return carry

    lax.fori_loop(0, seq // cb, combine, 0)


def _attn_prompt(q, k, v, slopes):
    b, seq, _ = q.shape
    assert seq % (16 * LANES) == 0
    blk = pl.BlockSpec((None, seq, LANES), lambda bb, hp: (bb, 0, hp))
    body = functools.partial(_attn_prompt_body, seq=seq)
    return pl.pallas_call(
        body,
        grid=(b, D_A // LANES),
        in_specs=[pl.BlockSpec(memory_space=pltpu.SMEM), blk, blk, blk],
        out_specs=blk,
        out_shape=jax.ShapeDtypeStruct((b, seq, D_A), BF16),
        scratch_shapes=[pltpu.VMEM((seq, LANES), F32) for _ in range(9)]
        + [pltpu.VMEM((2 * LANES, 2 * LANES if seq // dil >= 2 * LANES else LANES), F32) for _, dil in BRANCHES],
        compiler_params=_cparams("parallel", "parallel"),
        name="attn_prompt",
    )(slopes, q, k, v)


def _attn_sample_body(slope_ref, q_ref, kn_ref, vn_ref, kt_ref, vt_ref, o_ref, bias_c, bias_n, *, t_new, w_buf):
    nt = (((1,), (1,)), ((), ()))

    @pl.when(pl.program_id(0) == 0)
    def _():
        def table(ref, ncols, dist0):
            rowi = lax.broadcasted_iota(jnp.int32, (t_new, ncols), 0)
            coli = lax.broadcasted_iota(jnp.int32, (t_new, ncols), 1)
            d = dist0 + rowi - coli
            mult = jnp.zeros((t_new, ncols), jnp.int32)
            for win, dil in BRANCHES:
                mult = mult + jnp.where((d >= 0) & (d <= win) & ((d & (dil - 1)) == 0), 1, 0)
            logm = jnp.where(mult == 3, math.log(3.0), jnp.where(mult == 2, math.log(2.0), 0.0))
            for h in range(H_A):
                ref[h] = jnp.where(mult > 0, logm - slope_ref[h] * d.astype(F32), NEG_BIG)

        table(bias_c, w_buf, w_buf)
        table(bias_n, t_new, 0)

    outs = []
    for h in range(H_A):
        cols = slice(h * DH_A, (h + 1) * DH_A)
        q = q_ref[:, cols].astype(BF16)
        s_c = jnp.dot(q, kt_ref[h].astype(BF16), preferred_element_type=F32) + bias_c[h]
        s_n = lax.dot_general(q, kn_ref[:, cols].astype(BF16), nt, preferred_element_type=F32) + bias_n[h]
        mx = jnp.maximum(jnp.max(s_c, axis=1, keepdims=True), jnp.max(s_n, axis=1, keepdims=True))
        p_c = jnp.exp(s_c - mx)
        p_n = jnp.exp(s_n - mx)
        den = jnp.sum(p_c, axis=1, keepdims=True) + jnp.sum(p_n, axis=1, keepdims=True)
        o = lax.dot_general(p_c.astype(BF16), vt_ref[h].astype(BF16), nt, preferred_element_type=F32)
        o = o + jnp.dot(p_n.astype(BF16), vn_ref[:, cols].astype(BF16), preferred_element_type=F32)
        outs.append(o / den)
    o_ref[...] = jnp.concatenate(outs, axis=1)


def _attn_sample(q, kn, vn, cache_k, cache_v, layer, slopes):
    b, t_new, _ = q.shape
    w_buf = cache_k.shape[2]
    assert w_buf >= WINDOW and t_new % SUBLANES == 0
    kt = jnp.transpose(cache_k, (0, 1, 3, 4, 2))
    vt = jnp.transpose(cache_v, (0, 1, 3, 4, 2))
    new_blk = pl.BlockSpec((None, t_new, D_A), lambda bb: (bb, 0, 0))
    cache_blk = pl.BlockSpec((None, None, H_A, DH_A, w_buf), lambda bb: (layer, bb, 0, 0, 0))
    body = functools.partial(_attn_sample_body, t_new=t_new, w_buf=w_buf)
    return pl.pallas_call(
        body,
        grid=(b,),
        in_specs=[pl.BlockSpec(memory_space=pltpu.SMEM), new_blk, new_blk, new_blk, cache_blk, cache_blk],
        out_specs=new_blk,
        out_shape=jax.ShapeDtypeStruct((b, t_new, D_A), F32),
        scratch_shapes=[pltpu.VMEM((H_A, t_new, w_buf), F32), pltpu.VMEM((H_A, t_new, t_new), F32)],
        compiler_params=_cparams("arbitrary"),
        name="attn_sample",
    )(slopes, q, kn, vn, kt, vt)


S5_UT = D_B // LANES
S5_TW = S5_N // S5_UT


def _gelu_tanh(x):
    c = math.sqrt(2.0 / math.pi)
    return 0.5 * x * (1.0 + jnp.tanh(c * (x + 0.044715 * (x * x * x))))


def _s5_body(u_ref, h0_ref, perm_ref, permt_ref, wb_ref, lam_ref, wc_ref, dskip_ref, wglu_ref, bglu_ref,
             o_ref, hl_ref, bu_scr, up_scr, st_scr, *, nb, tsub, nsub):
    step = pl.program_id(0)
    rsub = nb * tsub
    tstep = tsub * nsub

    @pl.when(step == 0)
    def _():
        st_scr[...] = h0_ref[...]

    for sb in range(nsub):
        u_nat = u_ref[:, sb * tsub:(sb + 1) * tsub, :].reshape(rsub, D_B)
        up_scr[sb * rsub:(sb + 1) * rsub, :] = jnp.dot(
            perm_ref[...], u_nat, preferred_element_type=F32, precision=lax.Precision.HIGHEST)
    for j in range(S5_UT):
        uj = up_scr[:, j * LANES:(j + 1) * LANES].astype(BF16)
        r = jnp.dot(uj, wb_ref[j], preferred_element_type=F32)
        bu_scr[:, j * S5_TW:(j + 1) * S5_TW] = r[:, :S5_TW]
        bu_scr[:, S5_N + j * S5_TW:S5_N + (j + 1) * S5_TW] = r[:, S5_TW:]

    tiles_per_pass = 4
    for pg in range(S5_N // LANES // tiles_per_pass):
        cols = [(pg * tiles_per_pass + i) * LANES for i in range(tiles_per_pass)]
        lr = [jnp.broadcast_to(lam_ref[0:1, c:c + LANES], (nb, LANES)) for c in cols]
        li = [jnp.broadcast_to(lam_ref[1:2, c:c + LANES], (nb, LANES)) for c in cols]
        init = tuple(st_scr[:, c:c + LANES] for c in cols) + tuple(st_scr[:, S5_N + c:S5_N + c + LANES] for c in cols)

        def tick(t, carry, cols=cols, lr=lr, li=li):
            row = pl.ds(pl.multiple_of(t * nb, nb), nb)
            hr, hi = carry[:tiles_per_pass], carry[tiles_per_pass:]
            nr, ni = [], []
            for i, c in enumerate(cols):
                br = bu_scr[row, c:c + LANES]
                bi = bu_scr[row, S5_N + c:S5_N + c + LANES]
                r_new = lr[i] * hr[i] - li[i] * hi[i] + br
                i_new = lr[i] * hi[i] + li[i] * hr[i] + bi
                bu_scr[row, c:c + LANES] = r_new
                bu_scr[row, S5_N + c:S5_N + c + LANES] = i_new
                nr.append(r_new)
                ni.append(i_new)
            return tuple(nr) + tuple(ni)

        fin = lax.fori_loop(0, tstep, tick, init)
        for i, c in enumerate(cols):
            st_scr[:, c:c + LANES] = fin[i]
            st_scr[:, S5_N + c:S5_N + c + LANES] = fin[tiles_per_pass + i]

    hl_ref[...] = st_scr[...]

    ys = []
    for j in range(S5_UT):
        hre = bu_scr[:, j * S5_TW:(j + 1) * S5_TW].astype(BF16)
        him = bu_scr[:, S5_N + j * S5_TW:S5_N + (j + 1) * S5_TW].astype(BF16)
        y = jnp.dot(hre, wc_ref[j, :S5_TW, :], preferred_element_type=F32)
        y = y + jnp.dot(him, wc_ref[j, S5_TW:, :], preferred_element_type=F32)
        ys.append(y)
    y = jnp.concatenate(ys, axis=1) + dskip_ref[...] * up_scr[...]
    g = _gelu_tanh(y)
    gate = jnp.dot(g.astype(BF16), wglu_ref[...], preferred_element_type=F32) + bglu_ref[...]
    out = (g * jax.nn.sigmoid(gate)).astype(BF16)
    for sb in range(nsub):
        o_nat = jnp.dot(permt_ref[...], out[sb * rsub:(sb + 1) * rsub, :], preferred_element_type=F32)
        o_ref[:, sb * tsub:(sb + 1) * tsub, :] = o_nat.reshape(nb, tsub, D_B).astype(o_ref.dtype)


def _s5_mixer(u, h0, prm, tsub, nsub):
    nb, seq, _ = u.shape
    tstep = tsub * nsub
    rsub = nb * tsub
    rows = nb * tstep
    assert seq % tstep == 0 and nb % SUBLANES == 0 and tsub % SUBLANES == 0
    perm = np.zeros((rsub, rsub), np.float32)
    for t in range(tsub):
        for bb in range(nb):
            perm[t * nb + bb, bb * tsub + t] = 1.0
    body = functools.partial(_s5_body, nb=nb, tsub=tsub, nsub=nsub)
    return pl.pallas_call(
        body,
        grid=(seq // tstep,),
        in_specs=[pl.BlockSpec((nb, tstep, D_B), lambda i: (0, i, 0)),
                  _const_spec((nb, 2 * S5_N)),
                  _const_spec((rsub, rsub)), _const_spec((rsub, rsub)),
                  _const_spec((S5_UT, LANES, 2 * S5_TW)),
                  _const_spec((2, S5_N)),
                  _const_spec((S5_UT, 2 * S5_TW, LANES)),
                  _const_spec((1, D_B)),
                  _const_spec((D_B, D_B)),
                  _const_spec((1, D_B))],
        out_specs=[pl.BlockSpec((nb, tstep, D_B), lambda i: (0, i, 0)),
                   pl.BlockSpec((nb, 2 * S5_N), lambda i: (0, 0))],
        out_shape=[jax.ShapeDtypeStruct((nb, seq, D_B), _act_dtype(tsub)),
                   jax.ShapeDtypeStruct((nb, 2 * S5_N), F32)],
        scratch_shapes=[pltpu.VMEM((rows, 2 * S5_N), F32),
                        pltpu.VMEM((rows, D_B), F32),
                        pltpu.VMEM((nb, 2 * S5_N), F32)],
        compiler_params=_cparams("arbitrary"),
        name="s5_mixer",
    )(u, h0, jnp.asarray(perm, F32), jnp.asarray(perm.T, BF16), prm["wb"], prm["lam"], prm["wc"],
      prm["dskip"], prm["wglu"], prm["bglu"])


def _s5_params(lam_re, lam_im, log_dt, b_re, b_im, c_re, c_im, d_skip, w_glu, b_glu):
    lam = lax.complex(lam_re.astype(F32), lam_im.astype(F32))
    dt = jnp.exp(log_dt.astype(F32))[:, None]
    lam_bar = jnp.exp(lam * dt)
    b_bar = ((lam_bar - 1.0) / lam)[..., None] * lax.complex(b_re.astype(F32), b_im.astype(F32))
    gpt = LANES // CH_B
    eye = jnp.eye(gpt, dtype=F32)

    def b_blocks(part):
        x = part.reshape(S5_UT, gpt, P_B, CH_B)
        x = jnp.einsum("jgpc,gh->jgchp", x, eye)
        return x.reshape(S5_UT, gpt * CH_B, gpt * P_B)

    def c_blocks(part):
        x = part.reshape(S5_UT, gpt, CH_B, P_B)
        x = jnp.einsum("jgcp,gh->jgphc", x, eye)
        return x.reshape(S5_UT, gpt * P_B, gpt * CH_B)

    wb = jnp.concatenate([b_blocks(b_bar.real), b_blocks(b_bar.imag)], axis=2).astype(BF16)
    wc = jnp.concatenate([c_blocks(c_re.astype(F32)), -c_blocks(c_im.astype(F32))], axis=1).astype(BF16)
    lam2 = jnp.stack([lam_bar.real.reshape(S5_N), lam_bar.imag.reshape(S5_N)])
    return dict(wb=wb, wc=wc, lam=lam2, dskip=d_skip.astype(F32).reshape(1, D_B),
                wglu=w_glu.astype(BF16), bglu=b_glu.astype(F32).reshape(1, D_B))


HIST = SUBLANES
XB0 = D_INNER
XC0 = D_INNER + G_C * N_C
GW = D_INNER // G_C


def _exact_expand(a, e_ref):
    a1 = a.astype(BF16)
    r1 = a - a1.astype(F32)
    a2 = r1.astype(BF16)
    a3 = (r1 - a2.astype(F32)).astype(BF16)
    e = e_ref[...]
    out = jnp.dot(a1, e, preferred_element_type=F32)
    out = out + jnp.dot(a2, e, preferred_element_type=F32)
    return out + jnp.dot(a3, e, preferred_element_type=F32)


def _ssd_body(z_ref, xbc_ref, dt_ref, conv0_ref, ssm0_ref, convw_ref, convb_ref, dtb_ref, a_ref, dexp_ref,
              gw_ref, e_ref, *rest, lc):
    y_ref, convst_ref, hl_ref, xpad, state = rest[-5:]
    ch = pl.program_id(1)
    cl = SSD_CHUNK

    @pl.when(ch == 0)
    def _():
        xpad[0:HIST, :] = conv0_ref[...]
        state[...] = ssm0_ref[...]

    if lc < cl:
        xpad[HIST:, :] = jnp.zeros((cl, CONV_DIM), F32)
    xpad[HIST:HIST + lc, :] = xbc_ref[...]
    conv = convb_ref[...] + convw_ref[CONV_K - 1:CONV_K, :] * xpad[HIST:HIST + cl, :]
    for kk in range(CONV_K - 1):
        sh = CONV_K - 1 - kk
        conv = conv + convw_ref[kk:kk + 1, :] * xpad[HIST - sh:HIST - sh + cl, :]
    xs = conv * jax.nn.sigmoid(conv)
    tail = xpad[lc:lc + HIST, :]
    xpad[0:HIST, :] = tail
    convst_ref[...] = tail

    dt = jax.nn.softplus(dt_ref[...] + dtb_ref[...])
    if lc < cl:
        dt = jnp.concatenate([dt, jnp.zeros((cl - lc, LANES), F32)], axis=0)
    da = dt * a_ref[...]
    ri = lax.broadcasted_iota(jnp.int32, (cl, cl), 0)
    ci = lax.broadcasted_iota(jnp.int32, (cl, cl), 1)
    tri = ri >= ci
    a_cs = jnp.dot(jnp.where(tri, 1.0, 0.0), da, preferred_element_type=F32, precision=lax.Precision.HIGHEST)
    a_cs_t = a_cs.T
    dt_t = dt.T
    e_cs = jnp.exp(a_cs)
    tot = a_cs[cl - 1:cl, :]
    w_end = jnp.exp(tot - a_cs) * dt
    scale_x = _exact_expand(e_cs, e_ref)
    wend_x = _exact_expand(w_end, e_ref)

    x = xs[:, :D_INNER]
    xw = (x * wend_x).astype(BF16)
    lane = lax.broadcasted_iota(jnp.int32, (cl, LANES), 1)
    head0 = lane < P_C
    y_parts = []
    for g in range(G_C):
        bg = xs[:, XB0 + g * N_C:XB0 + (g + 1) * N_C].astype(BF16)
        cg = xs[:, XC0 + g * N_C:XC0 + (g + 1) * N_C].astype(BF16)
        cb = lax.dot_general(cg, bg, (((1,), (1,)), ((), ())), preferred_element_type=F32)
        prev = state[g * GW:(g + 1) * GW, :]
        y_off = lax.dot_general(cg, prev.astype(BF16), (((1,), (1,)), ((), ())), preferred_element_type=F32)
        for jp in range(J_C // 2):
            ms = []
            for hh in range(2):
                h = g * J_C + 2 * jp + hh
                seg = a_cs[:, h:h + 1] - a_cs_t[h:h + 1, :]
                dec = jnp.exp(jnp.where(tri, seg, NEG_BIG))
                ms.append((cb * dec * dt_t[h:h + 1, :]).astype(BF16))
            c0 = g * GW + jp * LANES
            xp = x[:, c0:c0 + LANES].astype(BF16)
            r = jnp.dot(jnp.concatenate(ms, axis=0), xp, preferred_element_type=F32)
            yd = jnp.where(head0, r[:cl], r[cl:])
            y_parts.append(yd + y_off[:, jp * LANES:(jp + 1) * LANES] * scale_x[:, c0:c0 + LANES])
        new = lax.dot_general(xw[:, g * GW:(g + 1) * GW], bg, (((0,), (0,)), ((), ())),
                              preferred_element_type=F32)
        for j in range(J_C):
            h = g * J_C + j
            r0 = g * GW + j * P_C
            state[r0:r0 + P_C, :] = state[r0:r0 + P_C, :] * jnp.exp(tot[:, h:h + 1]) + new[j * P_C:(j + 1) * P_C, :]
    hl_ref[...] = state[...]

    y = jnp.concatenate(y_parts, axis=1) + dexp_ref[...] * x
    y = y[:lc] if lc < cl else y
    z = z_ref[...]
    gt = y * (z * jax.nn.sigmoid(z))
    outs = []
    for g in range(G_C):
        gg = gt[:, g * GW:(g + 1) * GW]
        ms = jnp.mean(gg * gg, axis=-1, keepdims=True)
        outs.append(gg * lax.rsqrt(ms + RMS_EPS))
    y_ref[...] = (jnp.concatenate(outs, axis=1) * gw_ref[...]).astype(y_ref.dtype)


def _ssd_mixer(z, xbc, dtr, conv0, ssm0, prm, stack=None, layer=0, n_layers=1):
    b, seq, _ = z.shape
    lc = SSD_CHUNK if seq % SSD_CHUNK == 0 else seq
    assert lc % SUBLANES == 0 and CONV_K - 1 <= lc <= SSD_CHUNK
    nch = seq // lc
    body = functools.partial(_ssd_body, lc=lc)

    def row_blk(n):
        return pl.BlockSpec((None, lc, n), lambda bb, c: (bb, c, 0))

    def per_b(r, n):
        return pl.BlockSpec((None, r, n), lambda bb, c: (bb, 0, 0))

    return pl.pallas_call(
        body,
        grid=(b, nch),
        in_specs=[row_blk(D_INNER), row_blk(CONV_DIM), row_blk(LANES),
                  per_b(HIST, CONV_DIM), per_b(D_INNER, N_C),
                  _const_spec((CONV_K, CONV_DIM)), _const_spec((1, CONV_DIM)),
                  _const_spec((1, LANES)), _const_spec((1, LANES)),
                  _const_spec((1, D_INNER)), _const_spec((1, D_INNER)),
                  _const_spec((LANES, D_INNER))] + ([] if stack is None else [pl.BlockSpec(memory_space=pl.ANY)]),
        out_specs=[row_blk(D_INNER), per_b(HIST, CONV_DIM),
                   pl.BlockSpec((None, D_INNER, N_C), lambda bb, c: (layer * b + bb, 0, 0))],
        out_shape=[jax.ShapeDtypeStruct((b, seq, D_INNER), _act_dtype(lc)),
                   jax.ShapeDtypeStruct((b, HIST, CONV_DIM), F32),
                   jax.ShapeDtypeStruct((n_layers * b, D_INNER, N_C), F32)],
        scratch_shapes=[pltpu.VMEM((HIST + SSD_CHUNK, CONV_DIM), F32),
                        pltpu.VMEM((D_INNER, N_C), F32)],
        input_output_aliases={} if stack is None else {12: 2},
        compiler_params=_cparams("parallel", "arbitrary"),
        name="ssd_mixer",
    )(z, xbc, dtr, conv0, ssm0, prm["convw"], prm["convb"], prm["dtb"], prm["a"], prm["dexp"],
      prm["gw"], prm["expand"], *(() if stack is None else (stack,)))


def _ssd_params(conv_w, conv_b, dt_bias, a_log, d_skip, gnorm_w):
    pad = LANES - H_C
    expand = np.zeros((LANES, D_INNER), np.float32)
    for h in range(H_C):
        expand[h, h * P_C:(h + 1) * P_C] = 1.0
    return dict(
        convw=conv_w.astype(F32), convb=conv_b.astype(F32).reshape(1, CONV_DIM),
        dtb=jnp.pad(dt_bias.astype(F32), (0, pad)).reshape(1, LANES),
        a=jnp.pad(-jnp.exp(a_log.astype(F32)), (0, pad)).reshape(1, LANES),
        dexp=jnp.repeat(d_skip.astype(F32), P_C).reshape(1, D_INNER),
        gw=gnorm_w.astype(F32).reshape(1, D_INNER),
        expand=jnp.asarray(expand, BF16))


def _trunk(x, p, slopes, caches, tm):
    b, seq, _ = x.shape
    m = b * seq
    x2 = x.reshape(m, D_MODEL)
    new_s5, new_conv = [], []
    kv_stacks, ssm_stack = None, None
    kv_transposed = seq % tm == 0
    n_even, n_odd = (DEPTH + 1) // 2, DEPTH // 2
    for i in range(DEPTH):
        j = i // 2
        wn_pre = p["norm_mix_pre"][i].reshape(1, D_MODEL)
        if i % 2 == 0:
            assert seq <= WINDOW
            q, k, v, u, *kv_stacks = _norm_proj(x2, wn_pre, p["w_in_even"][j], (D_A, D_A, D_A, D_B),
                                                (DH_A ** -0.5, 1.0, 1.0, 1.0), tm, head_splits=(1, 2),
                                                stacks=kv_stacks, layer=j, n_layers=n_even,
                                                seq=seq if kv_transposed else None)
            q3, k3, v3 = (t.reshape(b, seq, D_A) for t in (q, k, v))
            if caches is None:
                o_a = _attn_prompt(q3, k3, v3, slopes)
                h0 = jnp.zeros((b, 2 * S5_N), F32)
                tsub, nsub = 16, 4
            else:
                o_a = _attn_sample(q3, k3, v3, caches["k"], caches["v"], j, slopes)
                s0 = caches["s5"][j].astype(F32)
                h0 = jnp.concatenate([s0[..., 0].reshape(b, S5_N), s0[..., 1].reshape(b, S5_N)], axis=1)
                tsub, nsub = seq, 1
            o_b, hl = _s5_mixer(u.reshape(b, seq, D_B), h0, p["s5"][j], tsub, nsub)
            new_s5.append(jnp.stack([hl[:, :S5_N].reshape(b, G_B, P_B), hl[:, S5_N:].reshape(b, G_B, P_B)], axis=-1))
            acts = [o_a.reshape(m, D_A), o_b.reshape(m, D_B)]
            ws = [p["w_out_even"][j][:D_A], p["w_out_even"][j][D_A:]]
        else:
            z, xbc, dtr = _norm_proj(x2, wn_pre, p["w_in_odd"][j], (D_INNER, CONV_DIM, LANES),
                                     (1.0, 1.0, 1.0), tm)
            if caches is None:
                conv0 = jnp.zeros((b, HIST, CONV_DIM), F32)
                ssm0 = jnp.zeros((b, D_INNER, N_C), F32)
            else:
                conv0 = jnp.pad(caches["conv"][j].astype(F32), ((0, 0), (HIST - (CONV_K - 1), 0), (0, 0)))
                ssm0 = caches["ssm"][j].astype(F32).reshape(b, D_INNER, N_C)
            yg, convst, ssm_stack = _ssd_mixer(z.reshape(b, seq, D_INNER), xbc.reshape(b, seq, CONV_DIM),
                                               dtr.reshape(b, seq, LANES), conv0, ssm0, p["ssd"][j],
                                               stack=ssm_stack, layer=j, n_layers=n_odd)
            new_conv.append(convst[:, HIST - (CONV_K - 1):])
            acts = [yg.reshape(m, D_INNER)]
            ws = [p["w_out_odd"][j]]
        x2 = _proj_mlp(acts, ws, x2, p["norm_mix_post"][i].reshape(1, D_MODEL),
                       p["norm_mlp_pre"][i].reshape(1, D_MODEL), p["w_mlp_up"][i], p["w_mlp_down"][i],
                       p["norm_mlp_post"][i].reshape(1, D_MODEL), tm)
    if kv_transposed:
        k_all, v_all = (jnp.transpose(t, (0, 1, 4, 2, 3)) for t in kv_stacks)
    else:
        k_all, v_all = (t.reshape(n_even, b, seq, H_A, DH_A) for t in kv_stacks)
    return (x2.reshape(b, seq, D_MODEL), k_all, v_all, jnp.stack(new_s5), jnp.stack(new_conv),
            ssm_stack.reshape(n_odd, b, H_C, P_C, N_C))


def kernel(x_prompt, x_sample, cache_k, cache_v, state_s5, state_conv, state_ssm, norm_mix_pre, norm_mix_post, norm_mlp_pre, norm_mlp_post, w_mlp_up, w_mlp_down, w_in_even, w_out_even, s5_lambda_re, s5_lambda_im, s5_log_dt, s5_b_re, s5_b_im, s5_c_re, s5_c_im, s5_d, s5_w_glu, s5_b_glu, w_in_odd, conv_w, conv_b, dt_bias, a_log, d_skip, gnorm_w, w_out_odd):
    n_even, n_odd = w_in_even.shape[0], w_in_odd.shape[0]
    odd_pad = LANES - H_C
    p = dict(
        norm_mix_pre=norm_mix_pre.astype(F32), norm_mix_post=norm_mix_post.astype(F32),
        norm_mlp_pre=norm_mlp_pre.astype(F32), norm_mlp_post=norm_mlp_post.astype(F32),
        w_mlp_up=w_mlp_up.astype(BF16), w_mlp_down=w_mlp_down.astype(BF16),
        w_in_even=w_in_even.astype(BF16), w_out_even=w_out_even.astype(BF16),
        w_in_odd=jnp.pad(w_in_odd, ((0, 0), (0, 0), (0, odd_pad))).astype(BF16),
        w_out_odd=w_out_odd.astype(BF16),
        s5=[_s5_params(s5_lambda_re[j], s5_lambda_im[j], s5_log_dt[j], s5_b_re[j], s5_b_im[j], s5_c_re[j],
                       s5_c_im[j], s5_d[j], s5_w_glu[j], s5_b_glu[j]) for j in range(n_even)],
        ssd=[_ssd_params(conv_w[j], conv_b[j], dt_bias[j], a_log[j], d_skip[j], gnorm_w[j]) for j in range(n_odd)],
    )
    slopes = jnp.asarray(np.power(2.0, -8.0 * np.arange(1, H_A + 1) / H_A), dtype=F32)
    caches = dict(k=cache_k, v=cache_v, s5=state_s5, conv=state_conv, ssm=state_ssm)
    y_p, k_p, v_p, s5_p, conv_p, ssm_p = _trunk(x_prompt, p, slopes, None, tm=512)
    y_s, k_s, v_s, s5_s, conv_s, ssm_s = _trunk(x_sample, p, slopes, caches, tm=256)
    return (y_p, y_s, k_p, v_p, s5_p, conv_p, ssm_p, k_s, v_s, s5_s, conv_s, ssm_s)
```

```python
import functools
import math

import numpy as np
import jax
import jax.numpy as jnp
from jax import lax
from jax.experimental import pallas as pl
from jax.experimental.pallas import tpu as pltpu

F32 = jnp.float32
BF16 = jnp.bfloat16

D_MODEL = 1024
DEPTH = 4
PAST_LEN = 8192
RMS_EPS = 1e-6
D_FF = 4 * D_MODEL
D_A = D_MODEL // 2
DH_A = 64
H_A = D_A // DH_A
BRANCHES = ((128, 1), (512, 4), (2048, 16))
WINDOW = 2048
D_B = D_MODEL - D_A
CH_B = 16
G_B = D_B // CH_B
P_B = 64
S5_N = G_B * P_B
D_INNER = 2 * D_MODEL
P_C = 64
H_C = D_INNER // P_C
G_C = 4
J_C = H_C // G_C
N_C = 128
CONV_K = 4
CONV_DIM = D_INNER + 2 * G_C * N_C
SSD_CHUNK = 128

LANES = 128
SUBLANES = 8
VMEM_LIMIT = 56 * 1024 * 1024
NEG_BIG = -1e30


def _act_dtype(rows):
    return BF16 if rows % (2 * SUBLANES) == 0 else F32


def _cparams(*sem):
    return pltpu.CompilerParams(dimension_semantics=sem, vmem_limit_bytes=VMEM_LIMIT)


def _rms(x, w):
    ms = jnp.mean(x * x, axis=-1, keepdims=True)
    return x * lax.rsqrt(ms + RMS_EPS) * w


def _split3(a):
    a1 = a.astype(BF16)
    r1 = a - a1.astype(F32)
    a2 = r1.astype(BF16)
    a3 = (r1 - a2.astype(F32)).astype(BF16)
    return a1, a2, a3


def _const_spec(shape):
    nd = len(shape)
    return pl.BlockSpec(shape, lambda *_: (0,) * nd, pipeline_mode=pl.Buffered(1))


def _norm_proj_body(*refs, splits, scales, head_splits, n_stacks, tokens_minor):
    x_ref, wn_ref, w_ref = refs[:3]
    out_refs = refs[3 + n_stacks:3 + n_stacks + len(splits)]
    head_refs = refs[3 + n_stacks + len(splits):]
    h = _rms(x_ref[...], wn_ref[...]).astype(BF16)
    off = 0
    for idx, (o_ref, n, sc) in enumerate(zip(out_refs, splits, scales)):
        r = jnp.dot(h, w_ref[:, off:off + n], preferred_element_type=F32)
        if sc != 1.0:
            r = r * sc
        o_ref[...] = r.astype(o_ref.dtype)
        if idx in head_splits:
            h_ref = head_refs[head_splits.index(idx)]
            if tokens_minor:
                h_ref[...] = r.T.reshape(H_A, DH_A, r.shape[0])
            else:
                h_ref[...] = r.reshape(r.shape[0], H_A, DH_A)
        off += n


def _norm_proj(x2, wn, w, splits, scales, tm, head_splits=(), stacks=None, layer=0, n_layers=1, seq=None):
    m = x2.shape[0]
    n_total = w.shape[1]
    assert sum(splits) == n_total and m % tm == 0
    nblk = m // tm
    n_stacks = 0 if stacks is None else len(stacks)
    tokens_minor = seq is not None
    body = functools.partial(_norm_proj_body, splits=splits, scales=scales, head_splits=tuple(head_splits),
                             n_stacks=n_stacks, tokens_minor=tokens_minor)
    in_specs = [pl.BlockSpec((tm, D_MODEL), lambda i: (i, 0)),
                _const_spec((1, D_MODEL)),
                _const_spec((D_MODEL, n_total))]
    in_specs += [pl.BlockSpec(memory_space=pl.ANY)] * n_stacks
    out_specs = [pl.BlockSpec((tm, n), lambda i: (i, 0)) for n in splits]
    out_shape = [jax.ShapeDtypeStruct((m, n), F32) for n in splits]
    if tokens_minor:
        assert seq % tm == 0 and tm % LANES == 0
        per_seq = seq // tm
        out_specs += [pl.BlockSpec((None, None, H_A, DH_A, tm),
                                   lambda i: (layer, i // per_seq, 0, 0, i % per_seq)) for _ in head_splits]
        out_shape += [jax.ShapeDtypeStruct((n_layers, m // seq, H_A, DH_A, seq), F32) for _ in head_splits]
    else:
        out_specs += [pl.BlockSpec((tm, H_A, DH_A), lambda i: (layer * nblk + i, 0, 0)) for _ in head_splits]
        out_shape += [jax.ShapeDtypeStruct((n_layers * m, H_A, DH_A), F32) for _ in head_splits]
    aliases = {3 + s: len(splits) + s for s in range(n_stacks)}
    return pl.pallas_call(
        body,
        grid=(nblk,),
        in_specs=in_specs,
        out_specs=out_specs,
        out_shape=out_shape,
        input_output_aliases=aliases,
        compiler_params=_cparams("parallel"),
        name="norm_proj",
    )(x2, wn, w, *(stacks or ()))


CONV_COLS = 512


def _norm_proj_ssd_body(x_ref, wn_ref, w_ref, convw_ref, convb_ref, dtb_ref, zs_ref, xs_ref, dt_ref, tail_ref,
                        carry, xpad, *, tm, per_seq):
    hist = carry.shape[0]

    @pl.when(pl.program_id(0) % per_seq == 0)
    def _():
        carry[...] = jnp.zeros(carry.shape, F32)

    h = _rms(x_ref[...], wn_ref[...]).astype(BF16)
    for c in range(D_INNER // CONV_COLS):
        cols = slice(c * CONV_COLS, (c + 1) * CONV_COLS)
        z = jnp.dot(h, w_ref[:, cols], preferred_element_type=F32)
        zs_ref[:, cols] = z * jax.nn.sigmoid(z)
    for c in range(CONV_DIM // CONV_COLS):
        cols = slice(c * CONV_COLS, (c + 1) * CONV_COLS)
        r = jnp.dot(h, w_ref[:, D_INNER + c * CONV_COLS:D_INNER + (c + 1) * CONV_COLS], preferred_element_type=F32)
        xpad[0:hist, :] = carry[:, cols]
        xpad[hist:, :] = r
        conv = convb_ref[:, cols] + convw_ref[CONV_K - 1:CONV_K, cols] * r
        for kk in range(CONV_K - 1):
            sh = CONV_K - 1 - kk
            conv = conv + convw_ref[kk:kk + 1, cols] * xpad[hist - sh:hist - sh + tm, :]
        xs_ref[:, cols] = conv * jax.nn.sigmoid(conv)
        tail = xpad[tm:, :]
        carry[:, cols] = tail
        tail_ref[:, cols] = tail
    dtr = jnp.dot(h, w_ref[:, D_INNER + CONV_DIM:], preferred_element_type=F32)
    dt_ref[...] = jax.nn.softplus(dtr + dtb_ref[...])


def _norm_proj_ssd(x2, wn, w, prm, tm, seq):
    m = x2.shape[0]
    n_total = w.shape[1]
    assert m % tm == 0 and seq % tm == 0 and n_total == D_INNER + CONV_DIM + LANES
    nblk = m // tm
    body = functools.partial(_norm_proj_ssd_body, tm=tm, per_seq=seq // tm)
    return pl.pallas_call(
        body,
        grid=(nblk,),
        in_specs=[pl.BlockSpec((tm, D_MODEL), lambda i: (i, 0)),
                  _const_spec((1, D_MODEL)),
                  _const_spec((D_MODEL, n_total)),
                  _const_spec((CONV_K, CONV_DIM)), _const_spec((1, CONV_DIM)), _const_spec((1, LANES))],
        out_specs=[pl.BlockSpec((tm, D_INNER), lambda i: (i, 0)),
                   pl.BlockSpec((tm, CONV_DIM), lambda i: (i, 0)),
                   pl.BlockSpec((tm, LANES), lambda i: (i, 0)),
                   pl.BlockSpec((None, HIST, CONV_DIM), lambda i: (i, 0, 0))],
        out_shape=[jax.ShapeDtypeStruct((m, D_INNER), F32),
                   jax.ShapeDtypeStruct((m, CONV_DIM), F32),
                   jax.ShapeDtypeStruct((m, LANES), F32),
                   jax.ShapeDtypeStruct((nblk, HIST, CONV_DIM), F32)],
        scratch_shapes=[pltpu.VMEM((HIST, CONV_DIM), F32), pltpu.VMEM((HIST + tm, CONV_COLS), F32)],
        compiler_params=_cparams("arbitrary"),
        name="norm_proj_ssd",
    )(x2, wn, w, prm["convw"], prm["convb"], prm["dtb"])


def _proj_mlp_body(*refs, n_in, tf):
    a_refs = refs[:n_in]
    w_ref, x_ref, wpost_ref, wpre_ref, wup_ref, wdown_ref, wpost2_ref, o_ref = refs[n_in:]
    m = None
    row = 0
    for a_ref in a_refs:
        k = a_ref.shape[1]
        t = jnp.dot(a_ref[...].astype(BF16), w_ref[row:row + k, :], preferred_element_type=F32)
        m = t if m is None else m + t
        row += k
    x = x_ref[...] + _rms(m, wpost_ref[...])
    h = _rms(x, wpre_ref[...]).astype(BF16)
    acc = None
    for c in range(D_FF // tf):
        a = jnp.dot(h, wup_ref[:, c * tf:(c + 1) * tf], preferred_element_type=F32)
        a = jnp.maximum(a, 0.0)
        a = (a * a).astype(BF16)
        t = jnp.dot(a, wdown_ref[c * tf:(c + 1) * tf, :], preferred_element_type=F32)
        acc = t if acc is None else acc + t
    o_ref[...] = x + _rms(acc, wpost2_ref[...])


def _proj_mlp(acts, w_out, x2, wpost, wpre, wup, wdown, wpost2, tm, tf=512):
    m = x2.shape[0]
    n_in = len(acts)
    assert sum(a.shape[1] for a in acts) == w_out.shape[0]
    body = functools.partial(_proj_mlp_body, n_in=n_in, tf=tf)
    in_specs = [pl.BlockSpec((tm, a.shape[1]), lambda i: (i, 0)) for a in acts]
    in_specs += [_const_spec(w_out.shape),
                 pl.BlockSpec((tm, D_MODEL), lambda i: (i, 0)),
                 _const_spec((1, D_MODEL)), _const_spec((1, D_MODEL)),
                 _const_spec((D_MODEL, D_FF)), _const_spec((D_FF, D_MODEL)),
                 _const_spec((1, D_MODEL))]
    return pl.pallas_call(
        body,
        grid=(m // tm,),
        in_specs=in_specs,
        out_specs=pl.BlockSpec((tm, D_MODEL), lambda i: (i, 0)),
        out_shape=jax.ShapeDtypeStruct((m, D_MODEL), F32),
        compiler_params=_cparams("parallel"),
        name="proj_mlp",
    )(*acts, w_out, x2, wpost, wpre, wup, wdown, wpost2)


def _attn_prompt_body(slope_ref, q_ref, k_ref, v_ref, o_ref, *scr, seq):
    hp = pl.program_id(1)
    o_scr, m_scr, l_scr, bias_scr = scr[0:3], scr[3:6], scr[6:9], scr[9:12]
    qb = LANES
    n_units = seq // qb
    lane = lax.broadcasted_iota(jnp.int32, (qb, LANES), 1)
    head0 = lane < DH_A

    cfg = []
    for bi, (win, dil) in enumerate(BRANCHES):
        nqb = seq // dil // qb
        nk = 2 * qb if nqb >= 2 else qb
        rowi = lax.broadcasted_iota(jnp.int32, (2 * qb, nk), 0)
        coli = lax.broadcasted_iota(jnp.int32, (2 * qb, nk), 1)
        d = (nk - qb) + (rowi & (qb - 1)) - coli
        slope = jnp.where(rowi < qb, slope_ref[2 * hp], slope_ref[2 * hp + 1])
        bias_scr[bi][...] = jnp.where((d >= 0) & (d <= win // dil), -(slope * dil) * d.astype(F32), NEG_BIG)
        cfg.append((dil, nqb, nk))

    def step(u, carry):
        for bi, (dil, nqb, nk) in enumerate(cfg):
            def rows(start, dil=dil):
                if dil == 1:
                    return pl.ds(pl.multiple_of(start, qb), qb)
                return pl.ds(start, qb, stride=dil)

            r, i = (u, 0) if nqb == 1 else (u // nqb, u % nqb)
            cur = rows(r + dil * qb * i)
            q = q_ref[cur, :]
            qs = jnp.concatenate([jnp.where(head0, q, 0.0), jnp.where(head0, 0.0, q)], axis=0).astype(BF16)
            if nk == qb:
                kk = k_ref[cur, :].astype(BF16)
                vv = v_ref[cur, :].astype(BF16)
            else:
                prev = rows(r + dil * qb * jnp.maximum(i - 1, 0))
                kk = jnp.concatenate([k_ref[prev, :], k_ref[cur, :]], axis=0).astype(BF16)
                vv = jnp.concatenate([v_ref[prev, :], v_ref[cur, :]], axis=0).astype(BF16)
            s = lax.dot_general(qs, kk, (((1,), (1,)), ((), ())), preferred_element_type=F32)
            s = s + bias_scr[bi][...]
            if nk != qb:
                pen = jnp.where(i == 0, NEG_BIG, 0.0)
                s = jnp.concatenate([s[:, :qb] + pen, s[:, qb:]], axis=1)
            mx = jnp.max(s, axis=1, keepdims=True)
            p = jnp.exp(s - mx).astype(BF16)
            va = jnp.concatenate([vv, jnp.ones((nk, LANES), BF16)], axis=1)
            od = jnp.dot(p, va, preferred_element_type=F32)
            o_scr[bi][cur, :] = jnp.where(head0, od[:qb, :LANES], od[qb:, :LANES])
            m_scr[bi][cur, :] = jnp.where(head0, mx[:qb], mx[qb:])
            l_scr[bi][cur, :] = jnp.where(head0, od[:qb, LANES:], od[qb:, LANES:])
        return carry

    lax.fori_loop(0, n_units, step, 0, unroll=8)

    cb = 256

    def combine(c, carry):
        sl = pl.ds(pl.multiple_of(c * cb, cb), cb)
        ms = [m_scr[b][sl, :] for b in range(3)]
        mx = jnp.maximum(jnp.maximum(ms[0], ms[1]), ms[2])
        num = None
        den = None
        for b in range(3):
            w = jnp.exp(ms[b] - mx)
            tn = o_scr[b][sl, :] * w
            td = l_scr[b][sl, :] * w
            num = tn if num is None else num + tn
            den = td if den is None else den + td
        o_ref[sl, :] = (num / den).astype(o_ref.dtype)
        return carry

    lax.fori_loop(0, seq // cb, combine, 0)


def _attn_prompt(q, k, v, slopes):
    b, seq, _ = q.shape
    assert seq % (16 * LANES) == 0
    blk = pl.BlockSpec((None, seq, LANES), lambda bb, hp: (bb, 0, hp))
    body = functools.partial(_attn_prompt_body, seq=seq)
    return pl.pallas_call(
        body,
        grid=(b, D_A // LANES),
        in_specs=[pl.BlockSpec(memory_space=pltpu.SMEM), blk, blk, blk],
        out_specs=blk,
        out_shape=jax.ShapeDtypeStruct((b, seq, D_A), BF16),
        scratch_shapes=[pltpu.VMEM((seq, LANES), F32) for _ in range(9)]
        + [pltpu.VMEM((2 * LANES, 2 * LANES if seq // dil >= 2 * LANES else LANES), F32) for _, dil in BRANCHES],
        compiler_params=_cparams("parallel", "parallel"),
        name="attn_prompt",
    )(slopes, q, k, v)


def _attn_sample_body(slope_ref, q_ref, kn_ref, vn_ref, kt_ref, vt_ref, o_ref, bias_c, bias_n, *, t_new, w_buf):
    nt = (((1,), (1,)), ((), ()))

    @pl.when(pl.program_id(0) == 0)
    def _():
        def table(ref, ncols, dist0):
            rowi = lax.broadcasted_iota(jnp.int32, (t_new, ncols), 0)
            coli = lax.broadcasted_iota(jnp.int32, (t_new, ncols), 1)
            d = dist0 + rowi - coli
            mult = jnp.zeros((t_new, ncols), jnp.int32)
            for win, dil in BRANCHES:
                mult = mult + jnp.where((d >= 0) & (d <= win) & ((d & (dil - 1)) == 0), 1, 0)
            logm = jnp.where(mult == 3, math.log(3.0), jnp.where(mult == 2, math.log(2.0), 0.0))
            for h in range(H_A):
                ref[h] = jnp.where(mult > 0, logm - slope_ref[h] * d.astype(F32), NEG_BIG)

        table(bias_c, w_buf, w_buf)
        table(bias_n, t_new, 0)

    outs = []
    for h in range(H_A):
        cols = slice(h * DH_A, (h + 1) * DH_A)
        q = q_ref[:, cols].astype(BF16)
        s_c = jnp.dot(q, kt_ref[h].astype(BF16), preferred_element_type=F32) + bias_c[h]
        s_n = lax.dot_general(q, kn_ref[:, cols].astype(BF16), nt, preferred_element_type=F32) + bias_n[h]
        mx = jnp.maximum(jnp.max(s_c, axis=1, keepdims=True), jnp.max(s_n, axis=1, keepdims=True))
        p_c = jnp.exp(s_c - mx)
        p_n = jnp.exp(s_n - mx)
        den = jnp.sum(p_c, axis=1, keepdims=True) + jnp.sum(p_n, axis=1, keepdims=True)
        o = lax.dot_general(p_c.astype(BF16), vt_ref[h].astype(BF16), nt, preferred_element_type=F32)
        o = o + jnp.dot(p_n.astype(BF16), vn_ref[:, cols].astype(BF16), preferred_element_type=F32)
        outs.append(o / den)
    o_ref[...] = jnp.concatenate(outs, axis=1)


def _attn_sample(q, kn, vn, cache_k, cache_v, layer, slopes):
    b, t_new, _ = q.shape
    w_buf = cache_k.shape[2]
    assert w_buf >= WINDOW and t_new % SUBLANES == 0
    kt = jnp.transpose(cache_k, (0, 1, 3, 4, 2))
    vt = jnp.transpose(cache_v, (0, 1, 3, 4, 2))
    new_blk = pl.BlockSpec((None, t_new, D_A), lambda bb: (bb, 0, 0))
    cache_blk = pl.BlockSpec((None, None, H_A, DH_A, w_buf), lambda bb: (layer, bb, 0, 0, 0))
    body = functools.partial(_attn_sample_body, t_new=t_new, w_buf=w_buf)
    return pl.pallas_call(
        body,
        grid=(b,),
        in_specs=[pl.BlockSpec(memory_space=pltpu.SMEM), new_blk, new_blk, new_blk, cache_blk, cache_blk],
        out_specs=new_blk,
        out_shape=jax.ShapeDtypeStruct((b, t_new, D_A), F32),
        scratch_shapes=[pltpu.VMEM((H_A, t_new, w_buf), F32), pltpu.VMEM((H_A, t_new, t_new), F32)],
        compiler_params=_cparams("arbitrary"),
        name="attn_sample",
    )(slopes, q, kn, vn, kt, vt)


S5_UT = D_B // LANES
S5_TW = S5_N // S5_UT


def _gelu_tanh(x):
    c = math.sqrt(2.0 / math.pi)
    return 0.5 * x * (1.0 + jnp.tanh(c * (x + 0.044715 * (x * x * x))))


def _s5_body(u_ref, h0_ref, perm_ref, permt_ref, wb_ref, lam_ref, wc_ref, dskip_ref, wglu_ref, bglu_ref,
             o_ref, hl_ref, bu_scr, up_scr, st_scr, *, nb, tsub, nsub):
    step = pl.program_id(0)
    rsub = nb * tsub
    tstep = tsub * nsub

    @pl.when(step == 0)
    def _():
        st_scr[...] = h0_ref[...]

    for sb in range(nsub):
        u_nat = u_ref[:, sb * tsub:(sb + 1) * tsub, :].reshape(rsub, D_B)
        parts = [jnp.dot(perm_ref[...], p, preferred_element_type=F32) for p in _split3(u_nat)]
        up_scr[sb * rsub:(sb + 1) * rsub, :] = parts[0] + parts[1] + parts[2]
    for j in range(S5_UT):
        uj = up_scr[:, j * LANES:(j + 1) * LANES].astype(BF16)
        r = jnp.dot(uj, wb_ref[j], preferred_element_type=F32)
        bu_scr[:, j * S5_TW:(j + 1) * S5_TW] = r[:, :S5_TW]
        bu_scr[:, S5_N + j * S5_TW:S5_N + (j + 1) * S5_TW] = r[:, S5_TW:]

    tiles_per_pass = 4
    for pg in range(S5_N // LANES // tiles_per_pass):
        cols = [(pg * tiles_per_pass + i) * LANES for i in range(tiles_per_pass)]
        lr = [jnp.broadcast_to(lam_ref[0:1, c:c + LANES], (nb, LANES)) for c in cols]
        li = [jnp.broadcast_to(lam_ref[1:2, c:c + LANES], (nb, LANES)) for c in cols]
        init = tuple(st_scr[:, c:c + LANES] for c in cols) + tuple(st_scr[:, S5_N + c:S5_N + c + LANES] for c in cols)

        def tick(t, carry, cols=cols, lr=lr, li=li):
            row = pl.ds(t * nb, nb)
            hr, hi = carry[:tiles_per_pass], carry[tiles_per_pass:]
            nr, ni = [], []
            for i, c in enumerate(cols):
                br = bu_scr[row, c:c + LANES]
                bi = bu_scr[row, S5_N + c:S5_N + c + LANES]
                r_new = lr[i] * hr[i] - li[i] * hi[i] + br
                i_new = lr[i] * hi[i] + li[i] * hr[i] + bi
                bu_scr[row, c:c + LANES] = r_new
                bu_scr[row, S5_N + c:S5_N + c + LANES] = i_new
                nr.append(r_new)
                ni.append(i_new)
            return tuple(nr) + tuple(ni)

        fin = init
        for t in range(tstep):
            fin = tick(t, fin)
        for i, c in enumerate(cols):
            st_scr[:, c:c + LANES] = fin[i]
            st_scr[:, S5_N + c:S5_N + c + LANES] = fin[tiles_per_pass + i]

    hl_ref[...] = st_scr[...]

    ys = []
    for j in range(S5_UT):
        hre = bu_scr[:, j * S5_TW:(j + 1) * S5_TW].astype(BF16)
        him = bu_scr[:, S5_N + j * S5_TW:S5_N + (j + 1) * S5_TW].astype(BF16)
        y = jnp.dot(hre, wc_ref[j, :S5_TW, :], preferred_element_type=F32)
        y = y + jnp.dot(him, wc_ref[j, S5_TW:, :], preferred_element_type=F32)
        ys.append(y)
    y = jnp.concatenate(ys, axis=1) + dskip_ref[...] * up_scr[...]
    g = _gelu_tanh(y)
    gate = jnp.dot(g.astype(BF16), wglu_ref[...], preferred_element_type=F32) + bglu_ref[...]
    out = (g * jax.nn.sigmoid(gate)).astype(BF16)
    for sb in range(nsub):
        o_nat = jnp.dot(permt_ref[...], out[sb * rsub:(sb + 1) * rsub, :], preferred_element_type=F32)
        o_ref[:, sb * tsub:(sb + 1) * tsub, :] = o_nat.reshape(nb, tsub, D_B).astype(o_ref.dtype)


def _s5_mixer(u, h0, prm, tsub, nsub):
    nb, seq, _ = u.shape
    tstep = tsub * nsub
    rsub = nb * tsub
    rows = nb * tstep
    assert seq % tstep == 0 and nb % SUBLANES == 0 and tsub % SUBLANES == 0
    perm = np.zeros((rsub, rsub), np.float32)
    for t in range(tsub):
        for bb in range(nb):
            perm[t * nb + bb, bb * tsub + t] = 1.0
    body = functools.partial(_s5_body, nb=nb, tsub=tsub, nsub=nsub)
    return pl.pallas_call(
        body,
        grid=(seq // tstep,),
        in_specs=[pl.BlockSpec((nb, tstep, D_B), lambda i: (0, i, 0)),
                  _const_spec((nb, 2 * S5_N)),
                  _const_spec((rsub, rsub)), _const_spec((rsub, rsub)),
                  _const_spec((S5_UT, LANES, 2 * S5_TW)),
                  _const_spec((2, S5_N)),
                  _const_spec((S5_UT, 2 * S5_TW, LANES)),
                  _const_spec((1, D_B)),
                  _const_spec((D_B, D_B)),
                  _const_spec((1, D_B))],
        out_specs=[pl.BlockSpec((nb, tstep, D_B), lambda i: (0, i, 0)),
                   pl.BlockSpec((nb, 2 * S5_N), lambda i: (0, 0))],
        out_shape=[jax.ShapeDtypeStruct((nb, seq, D_B), _act_dtype(tsub)),
                   jax.ShapeDtypeStruct((nb, 2 * S5_N), F32)],
        scratch_shapes=[pltpu.VMEM((rows, 2 * S5_N), F32),
                        pltpu.VMEM((rows, D_B), F32),
                        pltpu.VMEM((nb, 2 * S5_N), F32)],
        compiler_params=_cparams("arbitrary"),
        name="s5_mixer",
    )(u, h0, jnp.asarray(perm, BF16), jnp.asarray(perm.T, BF16), prm["wb"], prm["lam"], prm["wc"],
      prm["dskip"], prm["wglu"], prm["bglu"])


def _s5_params(lam_re, lam_im, log_dt, b_re, b_im, c_re, c_im, d_skip, w_glu, b_glu):
    lam = lax.complex(lam_re.astype(F32), lam_im.astype(F32))
    dt = jnp.exp(log_dt.astype(F32))[:, None]
    lam_bar = jnp.exp(lam * dt)
    b_bar = ((lam_bar - 1.0) / lam)[..., None] * lax.complex(b_re.astype(F32), b_im.astype(F32))
    gpt = LANES // CH_B
    eye = jnp.eye(gpt, dtype=F32)

    def b_blocks(part):
        x = part.reshape(S5_UT, gpt, P_B, CH_B)
        x = jnp.einsum("jgpc,gh->jgchp", x, eye)
        return x.reshape(S5_UT, gpt * CH_B, gpt * P_B)

    def c_blocks(part):
        x = part.reshape(S5_UT, gpt, CH_B, P_B)
        x = jnp.einsum("jgcp,gh->jgphc", x, eye)
        return x.reshape(S5_UT, gpt * P_B, gpt * CH_B)

    wb = jnp.concatenate([b_blocks(b_bar.real), b_blocks(b_bar.imag)], axis=2).astype(BF16)
    wc = jnp.concatenate([c_blocks(c_re.astype(F32)), -c_blocks(c_im.astype(F32))], axis=1).astype(BF16)
    lam2 = jnp.stack([lam_bar.real.reshape(S5_N), lam_bar.imag.reshape(S5_N)])
    return dict(wb=wb, wc=wc, lam=lam2, dskip=d_skip.astype(F32).reshape(1, D_B),
                wglu=w_glu.astype(BF16), bglu=b_glu.astype(F32).reshape(1, D_B))


HIST = SUBLANES
XB0 = D_INNER
XC0 = D_INNER + G_C * N_C
GW = D_INNER // G_C


def _expand_heads(a, e_ref):
    e = e_ref[...]
    a1, a2, _ = _split3(a)
    return jnp.dot(a1, e, preferred_element_type=F32) + jnp.dot(a2, e, preferred_element_type=F32)


def _ssd_body(z_ref, xbc_ref, dt_ref, ssm0_ref, a_ref, dexp_ref, gw_ref, e_ref, *rest, lc, activated):
    if activated:
        y_ref, hl_ref, state = rest[-3:]
    else:
        conv0_ref, convw_ref, convb_ref, dtb_ref = rest[:4]
        y_ref, convst_ref, hl_ref, xpad, state = rest[-5:]
    ch = pl.program_id(1)
    cl = SSD_CHUNK

    @pl.when(ch == 0)
    def _():
        state[...] = ssm0_ref[...]

    if activated:
        assert lc == cl
        xs = xbc_ref[...]
        dt = dt_ref[...]
    else:
        @pl.when(ch == 0)
        def _():
            xpad[0:HIST, :] = conv0_ref[...]

        if lc < cl:
            xpad[HIST:, :] = jnp.zeros((cl, CONV_DIM), F32)
        xpad[HIST:HIST + lc, :] = xbc_ref[...]
        conv = convb_ref[...] + convw_ref[CONV_K - 1:CONV_K, :] * xpad[HIST:HIST + cl, :]
        for kk in range(CONV_K - 1):
            sh = CONV_K - 1 - kk
            conv = conv + convw_ref[kk:kk + 1, :] * xpad[HIST - sh:HIST - sh + cl, :]
        xs = conv * jax.nn.sigmoid(conv)
        tail = xpad[lc:lc + HIST, :]
        xpad[0:HIST, :] = tail
        convst_ref[...] = tail

        dt = jax.nn.softplus(dt_ref[...] + dtb_ref[...])
        if lc < cl:
            dt = jnp.concatenate([dt, jnp.zeros((cl - lc, LANES), F32)], axis=0)
    da = dt * a_ref[...]
    ri = lax.broadcasted_iota(jnp.int32, (cl, cl), 0)
    ci = lax.broadcasted_iota(jnp.int32, (cl, cl), 1)
    tri = ri >= ci
    tri01 = jnp.where(tri, 1.0, 0.0).astype(BF16)
    cs = [jnp.dot(tri01, p, preferred_element_type=F32) for p in _split3(da)]
    a_cs = cs[0] + cs[1] + cs[2]
    a_cs_t = a_cs.T
    dt_t = dt.T
    e_cs = jnp.exp(a_cs)
    tot = a_cs[cl - 1:cl, :]
    w_end = jnp.exp(tot - a_cs) * dt
    both = _expand_heads(jnp.concatenate([e_cs, w_end], axis=0), e_ref)
    scale_x, wend_x = both[:cl], both[cl:]

    x = xs[:, :D_INNER]
    xw = (x * wend_x).astype(BF16)
    lane = lax.broadcasted_iota(jnp.int32, (cl, LANES), 1)
    head0 = lane < P_C
    y_parts = []
    for g in range(G_C):
        bg = xs[:, XB0 + g * N_C:XB0 + (g + 1) * N_C].astype(BF16)
        cg = xs[:, XC0 + g * N_C:XC0 + (g + 1) * N_C].astype(BF16)
        cb = lax.dot_general(cg, bg, (((1,), (1,)), ((), ())), preferred_element_type=F32)
        prev = state[g * GW:(g + 1) * GW, :]
        y_off = lax.dot_general(cg, prev.astype(BF16), (((1,), (1,)), ((), ())), preferred_element_type=F32)
        for jp in range(J_C // 2):
            ms = []
            for hh in range(2):
                h = g * J_C + 2 * jp + hh
                seg = a_cs[:, h:h + 1] - a_cs_t[h:h + 1, :]
                dec = jnp.exp(jnp.where(tri, seg, NEG_BIG))
                ms.append((cb * dec * dt_t[h:h + 1, :]).astype(BF16))
            c0 = g * GW + jp * LANES
            xp = x[:, c0:c0 + LANES].astype(BF16)
            r = jnp.dot(jnp.concatenate(ms, axis=0), xp, preferred_element_type=F32)
            yd = jnp.where(head0, r[:cl], r[cl:])
            y_parts.append(yd + y_off[:, jp * LANES:(jp + 1) * LANES] * scale_x[:, c0:c0 + LANES])
        new = lax.dot_general(xw[:, g * GW:(g + 1) * GW], bg, (((0,), (0,)), ((), ())),
                              preferred_element_type=F32)
        for j in range(J_C):
            h = g * J_C + j
            r0 = g * GW + j * P_C
            state[r0:r0 + P_C, :] = state[r0:r0 + P_C, :] * jnp.exp(tot[:, h:h + 1]) + new[j * P_C:(j + 1) * P_C, :]
    hl_ref[...] = state[...]

    y = jnp.concatenate(y_parts, axis=1) + dexp_ref[...] * x
    y = y[:lc] if lc < cl else y
    z = z_ref[...]
    gt = y * (z if activated else z * jax.nn.sigmoid(z))
    outs = []
    for g in range(G_C):
        gg = gt[:, g * GW:(g + 1) * GW]
        ms = jnp.mean(gg * gg, axis=-1, keepdims=True)
        outs.append(gg * lax.rsqrt(ms + RMS_EPS))
    y_ref[...] = (jnp.concatenate(outs, axis=1) * gw_ref[...]).astype(y_ref.dtype)


def _ssd_mixer(z, xbc, dtr, conv0, ssm0, prm, stack=None, layer=0, n_layers=1, activated=False):
    b, seq, _ = z.shape
    lc = SSD_CHUNK if seq % SSD_CHUNK == 0 else seq
    assert lc % SUBLANES == 0 and CONV_K - 1 <= lc <= SSD_CHUNK
    nch = seq // lc
    body = functools.partial(_ssd_body, lc=lc, activated=activated)
    keep = (lambda items: [it for i, it in enumerate(items) if i != 1]) if activated else (lambda items: items)
    conv_args = [] if activated else [conv0, prm["convw"], prm["convb"], prm["dtb"]]

    def row_blk(n):
        return pl.BlockSpec((None, lc, n), lambda bb, c: (bb, c, 0))

    def per_b(r, n):
        return pl.BlockSpec((None, r, n), lambda bb, c: (bb, 0, 0))

    conv_specs = [] if activated else [per_b(HIST, CONV_DIM), _const_spec((CONV_K, CONV_DIM)),
                                       _const_spec((1, CONV_DIM)), _const_spec((1, LANES))]

    return pl.pallas_call(
        body,
        grid=(b, nch),
        in_specs=[row_blk(D_INNER), row_blk(CONV_DIM), row_blk(LANES), per_b(D_INNER, N_C),
                  _const_spec((1, LANES)), _const_spec((1, D_INNER)), _const_spec((1, D_INNER)),
                  _const_spec((LANES, D_INNER))] + conv_specs
        + ([] if stack is None else [pl.BlockSpec(memory_space=pl.ANY)]),
        out_specs=keep([row_blk(D_INNER), per_b(HIST, CONV_DIM),
                        pl.BlockSpec((None, D_INNER, N_C), lambda bb, c: (layer * b + bb, 0, 0))]),
        out_shape=keep([jax.ShapeDtypeStruct((b, seq, D_INNER), _act_dtype(lc)),
                        jax.ShapeDtypeStruct((b, HIST, CONV_DIM), F32),
                        jax.ShapeDtypeStruct((n_layers * b, D_INNER, N_C), F32)]),
        scratch_shapes=([] if activated else [pltpu.VMEM((HIST + SSD_CHUNK, CONV_DIM), F32)])
        + [pltpu.VMEM((D_INNER, N_C), F32)],
        input_output_aliases={} if stack is None else {8 + len(conv_args): 1 if activated else 2},
        compiler_params=_cparams("parallel", "arbitrary"),
        name="ssd_mixer",
    )(z, xbc, dtr, ssm0, prm["a"], prm["dexp"], prm["gw"], prm["expand"], *conv_args,
      *(() if stack is None else (stack,)))


def _ssd_params(conv_w, conv_b, dt_bias, a_log, d_skip, gnorm_w):
    pad = LANES - H_C
    expand = np.zeros((LANES, D_INNER), np.float32)
    for h in range(H_C):
        expand[h, h * P_C:(h + 1) * P_C] = 1.0
    return dict(
        convw=conv_w.astype(F32), convb=conv_b.astype(F32).reshape(1, CONV_DIM),
        dtb=jnp.pad(dt_bias.astype(F32), (0, pad)).reshape(1, LANES),
        a=jnp.pad(-jnp.exp(a_log.astype(F32)), (0, pad)).reshape(1, LANES),
        dexp=jnp.repeat(d_skip.astype(F32), P_C).reshape(1, D_INNER),
        gw=gnorm_w.astype(F32).reshape(1, D_INNER),
        expand=jnp.asarray(expand, BF16))


def _trunk(x, p, slopes, caches, tm):
    b, seq, _ = x.shape
    m = b * seq
    x2 = x.reshape(m, D_MODEL)
    new_s5, new_conv = [], []
    kv_stacks, ssm_stack = None, None
    kv_transposed = seq % tm == 0
    n_even, n_odd = (DEPTH + 1) // 2, DEPTH // 2
    for i in range(DEPTH):
        j = i // 2
        wn_pre = p["norm_mix_pre"][i].reshape(1, D_MODEL)
        if i % 2 == 0:
            assert seq <= WINDOW
            q, k, v, u, *kv_stacks = _norm_proj(x2, wn_pre, p["w_in_even"][j], (D_A, D_A, D_A, D_B),
                                                (DH_A ** -0.5, 1.0, 1.0, 1.0), tm, head_splits=(1, 2),
                                                stacks=kv_stacks, layer=j, n_layers=n_even,
                                                seq=seq if kv_transposed else None)
            q3, k3, v3 = (t.reshape(b, seq, D_A) for t in (q, k, v))
            if caches is None:
                o_a = _attn_prompt(q3, k3, v3, slopes)
                h0 = jnp.zeros((b, 2 * S5_N), F32)
                tsub, nsub = 16, 4
            else:
                o_a = _attn_sample(q3, k3, v3, caches["k"], caches["v"], j, slopes)
                s0 = caches["s5"][j].astype(F32)
                h0 = jnp.concatenate([s0[..., 0].reshape(b, S5_N), s0[..., 1].reshape(b, S5_N)], axis=1)
                tsub, nsub = seq, 1
            o_b, hl = _s5_mixer(u.reshape(b, seq, D_B), h0, p["s5"][j], tsub, nsub)
            new_s5.append(jnp.stack([hl[:, :S5_N].reshape(b, G_B, P_B), hl[:, S5_N:].reshape(b, G_B, P_B)], axis=-1))
            acts = [o_a.reshape(m, D_A), o_b.reshape(m, D_B)]
            w_out = p["w_out_even"][j]
        else:
            if caches is None and seq % tm == 0:
                z, xbc, dtr, tails = _norm_proj_ssd(x2, wn_pre, p["w_in_odd"][j], p["ssd"][j], tm, seq)
                ssm0 = jnp.zeros((b, D_INNER, N_C), F32)
                yg, ssm_stack = _ssd_mixer(z.reshape(b, seq, D_INNER), xbc.reshape(b, seq, CONV_DIM),
                                           dtr.reshape(b, seq, LANES), None, ssm0, p["ssd"][j],
                                           stack=ssm_stack, layer=j, n_layers=n_odd, activated=True)
                convst = tails.reshape(b, seq // tm, HIST, CONV_DIM)[:, -1]
            else:
                z, xbc, dtr = _norm_proj(x2, wn_pre, p["w_in_odd"][j], (D_INNER, CONV_DIM, LANES),
                                         (1.0, 1.0, 1.0), tm)
                if caches is None:
                    conv0 = jnp.zeros((b, HIST, CONV_DIM), F32)
                    ssm0 = jnp.zeros((b, D_INNER, N_C), F32)
                else:
                    conv0 = jnp.pad(caches["conv"][j].astype(F32), ((0, 0), (HIST - (CONV_K - 1), 0), (0, 0)))
                    ssm0 = caches["ssm"][j].astype(F32).reshape(b, D_INNER, N_C)
                yg, convst, ssm_stack = _ssd_mixer(z.reshape(b, seq, D_INNER), xbc.reshape(b, seq, CONV_DIM),
                                                   dtr.reshape(b, seq, LANES), conv0, ssm0, p["ssd"][j],
                                                   stack=ssm_stack, layer=j, n_layers=n_odd)
            new_conv.append(convst[:, HIST - (CONV_K - 1):])
            acts = [yg.reshape(m, D_INNER)]
            w_out = p["w_out_odd"][j]
        x2 = _proj_mlp(acts, w_out, x2, p["norm_mix_post"][i].reshape(1, D_MODEL),
                       p["norm_mlp_pre"][i].reshape(1, D_MODEL), p["w_mlp_up"][i], p["w_mlp_down"][i],
                       p["norm_mlp_post"][i].reshape(1, D_MODEL), tm)
    if kv_transposed:
        k_all, v_all = (jnp.transpose(t, (0, 1, 4, 2, 3)) for t in kv_stacks)
    else:
        k_all, v_all = (t.reshape(n_even, b, seq, H_A, DH_A) for t in kv_stacks)
    return (x2.reshape(b, seq, D_MODEL), k_all, v_all, jnp.stack(new_s5), jnp.stack(new_conv),
            ssm_stack.reshape(n_odd, b, H_C, P_C, N_C))


def kernel(x_prompt, x_sample, cache_k, cache_v, state_s5, state_conv, state_ssm, norm_mix_pre, norm_mix_post, norm_mlp_pre, norm_mlp_post, w_mlp_up, w_mlp_down, w_in_even, w_out_even, s5_lambda_re, s5_lambda_im, s5_log_dt, s5_b_re, s5_b_im, s5_c_re, s5_c_im, s5_d, s5_w_glu, s5_b_glu, w_in_odd, conv_w, conv_b, dt_bias, a_log, d_skip, gnorm_w, w_out_odd):
    n_even, n_odd = w_in_even.shape[0], w_in_odd.shape[0]
    odd_pad = LANES - H_C
    p = dict(
        norm_mix_pre=norm_mix_pre.astype(F32), norm_mix_post=norm_mix_post.astype(F32),
        norm_mlp_pre=norm_mlp_pre.astype(F32), norm_mlp_post=norm_mlp_post.astype(F32),
        w_mlp_up=w_mlp_up.astype(BF16), w_mlp_down=w_mlp_down.astype(BF16),
        w_in_even=w_in_even.astype(BF16), w_out_even=w_out_even.astype(BF16),
        w_in_odd=jnp.pad(w_in_odd, ((0, 0), (0, 0), (0, odd_pad))).astype(BF16),
        w_out_odd=w_out_odd.astype(BF16),
        s5=[_s5_params(s5_lambda_re[j], s5_lambda_im[j], s5_log_dt[j], s5_b_re[j], s5_b_im[j], s5_c_re[j],
                       s5_c_im[j], s5_d[j], s5_w_glu[j], s5_b_glu[j]) for j in range(n_even)],
        ssd=[_ssd_params(conv_w[j], conv_b[j], dt_bias[j], a_log[j], d_skip[j], gnorm_w[j]) for j in range(n_odd)],
    )
    slopes = jnp.asarray(np.power(2.0, -8.0 * np.arange(1, H_A + 1) / H_A), dtype=F32)
    caches = dict(k=cache_k, v=cache_v, s5=state_s5, conv=state_conv, ssm=state_ssm)
    y_p, k_p, v_p, s5_p, conv_p, ssm_p = _trunk(x_prompt, p, slopes, None, tm=512)
    y_s, k_s, v_s, s5_s, conv_s, ssm_s = _trunk(x_sample, p, slopes, caches, tm=256)
    return (y_p, y_s, k_p, v_p, s5_p, conv_p, ssm_p, k_s, v_s, s5_s, conv_s, ssm_s)
```

```python
import functools
import math

import numpy as np
import jax
import jax.numpy as jnp
from jax import lax
from jax.experimental import pallas as pl
from jax.experimental.pallas import tpu as pltpu

F32 = jnp.float32
BF16 = jnp.bfloat16

D_MODEL = 1024
DEPTH = 4
PAST_LEN = 8192
RMS_EPS = 1e-6
D_FF = 4 * D_MODEL
D_A = D_MODEL // 2
DH_A = 64
H_A = D_A // DH_A
BRANCHES = ((128, 1), (512, 4), (2048, 16))
WINDOW = 2048
D_B = D_MODEL - D_A
CH_B = 16
G_B = D_B // CH_B
P_B = 64
S5_N = G_B * P_B
D_INNER = 2 * D_MODEL
P_C = 64
H_C = D_INNER // P_C
G_C = 4
J_C = H_C // G_C
N_C = 128
CONV_K = 4
CONV_DIM = D_INNER + 2 * G_C * N_C
SSD_CHUNK = 128

LANES = 128
SUBLANES = 8
VMEM_LIMIT = 56 * 1024 * 1024
NEG_BIG = -1e30


def _act_dtype(rows):
    return BF16 if rows % (2 * SUBLANES) == 0 else F32


def _cparams(*sem):
    return pltpu.CompilerParams(dimension_semantics=sem, vmem_limit_bytes=VMEM_LIMIT)


def _rms(x, w):
    ms = jnp.mean(x * x, axis=-1, keepdims=True)
    return x * lax.rsqrt(ms + RMS_EPS) * w


def _split3(a):
    a1 = a.astype(BF16)
    r1 = a - a1.astype(F32)
    a2 = r1.astype(BF16)
    a3 = (r1 - a2.astype(F32)).astype(BF16)
    return a1, a2, a3


def _const_spec(shape):
    nd = len(shape)
    return pl.BlockSpec(shape, lambda *_: (0,) * nd, pipeline_mode=pl.Buffered(1))


def _layer_spec(stacked, layer):
    nd = stacked.ndim - 1
    return pl.BlockSpec((None,) + stacked.shape[1:], lambda *_: (layer,) + (0,) * nd,
                        pipeline_mode=pl.Buffered(1))


def _norm_proj_body(*refs, splits, scales, head_splits, n_stacks, tokens_minor):
    x_ref, wn_ref, w_ref = refs[:3]
    out_refs = refs[3 + n_stacks:3 + n_stacks + len(splits)]
    head_refs = refs[3 + n_stacks + len(splits):]
    h = _rms(x_ref[...], wn_ref[...]).astype(BF16)
    off = 0
    for idx, (o_ref, n, sc) in enumerate(zip(out_refs, splits, scales)):
        r = jnp.dot(h, w_ref[:, off:off + n], preferred_element_type=F32)
        if sc != 1.0:
            r = r * sc
        o_ref[...] = r.astype(o_ref.dtype)
        if idx in head_splits:
            h_ref = head_refs[head_splits.index(idx)]
            if tokens_minor:
                h_ref[...] = r.T.reshape(H_A, DH_A, r.shape[0])
            else:
                h_ref[...] = r.reshape(r.shape[0], H_A, DH_A)
        off += n


def _norm_proj(x2, wn, w, w_layer, splits, scales, tm, head_splits=(), stacks=None, layer=0, n_layers=1, seq=None):
    m = x2.shape[0]
    n_total = w.shape[2]
    assert sum(splits) == n_total and m % tm == 0
    nblk = m // tm
    n_stacks = 0 if stacks is None else len(stacks)
    tokens_minor = seq is not None
    body = functools.partial(_norm_proj_body, splits=splits, scales=scales, head_splits=tuple(head_splits),
                             n_stacks=n_stacks, tokens_minor=tokens_minor)
    in_specs = [pl.BlockSpec((tm, D_MODEL), lambda i: (i, 0)),
                _const_spec((1, D_MODEL)),
                _layer_spec(w, w_layer)]
    in_specs += [pl.BlockSpec(memory_space=pl.ANY)] * n_stacks
    out_specs = [pl.BlockSpec((tm, n), lambda i: (i, 0)) for n in splits]
    out_shape = [jax.ShapeDtypeStruct((m, n), F32) for n in splits]
    if tokens_minor:
        assert seq % tm == 0 and tm % LANES == 0
        per_seq = seq // tm
        out_specs += [pl.BlockSpec((None, None, H_A, DH_A, tm),
                                   lambda i: (layer, i // per_seq, 0, 0, i % per_seq)) for _ in head_splits]
        out_shape += [jax.ShapeDtypeStruct((n_layers, m // seq, H_A, DH_A, seq), F32) for _ in head_splits]
    else:
        out_specs += [pl.BlockSpec((tm, H_A, DH_A), lambda i: (layer * nblk + i, 0, 0)) for _ in head_splits]
        out_shape += [jax.ShapeDtypeStruct((n_layers * m, H_A, DH_A), F32) for _ in head_splits]
    aliases = {3 + s: len(splits) + s for s in range(n_stacks)}
    return pl.pallas_call(
        body,
        grid=(nblk,),
        in_specs=in_specs,
        out_specs=out_specs,
        out_shape=out_shape,
        input_output_aliases=aliases,
        compiler_params=_cparams("parallel"),
        name="norm_proj",
    )(x2, wn, w, *(stacks or ()))


CONV_COLS = 512


def _norm_proj_ssd_body(x_ref, wn_ref, w_ref, convw_ref, convb_ref, dtb_ref, zs_ref, xs_ref, dt_ref, tail_ref,
                        carry, xpad, *, tm, per_seq):
    hist = carry.shape[0]

    @pl.when(pl.program_id(0) % per_seq == 0)
    def _():
        carry[...] = jnp.zeros(carry.shape, F32)

    h = _rms(x_ref[...], wn_ref[...]).astype(BF16)
    for c in range(D_INNER // CONV_COLS):
        cols = slice(c * CONV_COLS, (c + 1) * CONV_COLS)
        z = jnp.dot(h, w_ref[:, cols], preferred_element_type=F32)
        zs_ref[:, cols] = z * jax.nn.sigmoid(z)
    for c in range(CONV_DIM // CONV_COLS):
        cols = slice(c * CONV_COLS, (c + 1) * CONV_COLS)
        r = jnp.dot(h, w_ref[:, D_INNER + c * CONV_COLS:D_INNER + (c + 1) * CONV_COLS], preferred_element_type=F32)
        xpad[0:hist, :] = carry[:, cols]
        xpad[hist:, :] = r
        conv = convb_ref[:, cols] + convw_ref[CONV_K - 1:CONV_K, cols] * r
        for kk in range(CONV_K - 1):
            sh = CONV_K - 1 - kk
            conv = conv + convw_ref[kk:kk + 1, cols] * xpad[hist - sh:hist - sh + tm, :]
        xs_ref[:, cols] = conv * jax.nn.sigmoid(conv)
        tail = xpad[tm:, :]
        carry[:, cols] = tail
        tail_ref[:, cols] = tail
    dtr = jnp.dot(h, w_ref[:, D_INNER + CONV_DIM:], preferred_element_type=F32)
    dt_ref[...] = jax.nn.softplus(dtr + dtb_ref[...])


def _norm_proj_ssd(x2, wn, w, w_layer, prm, tm, seq):
    m = x2.shape[0]
    n_total = w.shape[2]
    assert m % tm == 0 and seq % tm == 0 and n_total == D_INNER + CONV_DIM + LANES
    nblk = m // tm
    body = functools.partial(_norm_proj_ssd_body, tm=tm, per_seq=seq // tm)
    return pl.pallas_call(
        body,
        grid=(nblk,),
        in_specs=[pl.BlockSpec((tm, D_MODEL), lambda i: (i, 0)),
                  _const_spec((1, D_MODEL)),
                  _layer_spec(w, w_layer),
                  _const_spec((CONV_K, CONV_DIM)), _const_spec((1, CONV_DIM)), _const_spec((1, LANES))],
        out_specs=[pl.BlockSpec((tm, D_INNER), lambda i: (i, 0)),
                   pl.BlockSpec((tm, CONV_DIM), lambda i: (i, 0)),
                   pl.BlockSpec((tm, LANES), lambda i: (i, 0)),
                   pl.BlockSpec((None, HIST, CONV_DIM), lambda i: (i, 0, 0))],
        out_shape=[jax.ShapeDtypeStruct((m, D_INNER), F32),
                   jax.ShapeDtypeStruct((m, CONV_DIM), F32),
                   jax.ShapeDtypeStruct((m, LANES), F32),
                   jax.ShapeDtypeStruct((nblk, HIST, CONV_DIM), F32)],
        scratch_shapes=[pltpu.VMEM((HIST, CONV_DIM), F32), pltpu.VMEM((HIST + tm, CONV_COLS), F32)],
        compiler_params=_cparams("arbitrary"),
        name="norm_proj_ssd",
    )(x2, wn, w, prm["convw"], prm["convb"], prm["dtb"])


def _proj_mlp_body(*refs, n_in, tf):
    a_refs = refs[:n_in]
    w_ref, x_ref, wpost_ref, wpre_ref, wup_ref, wdown_ref, wpost2_ref, o_ref = refs[n_in:]
    m = None
    row = 0
    for a_ref in a_refs:
        k = a_ref.shape[1]
        t = jnp.dot(a_ref[...].astype(BF16), w_ref[row:row + k, :], preferred_element_type=F32)
        m = t if m is None else m + t
        row += k
    x = x_ref[...] + _rms(m, wpost_ref[...])
    h = _rms(x, wpre_ref[...]).astype(BF16)
    acc = None
    for c in range(D_FF // tf):
        a = jnp.dot(h, wup_ref[:, c * tf:(c + 1) * tf], preferred_element_type=F32)
        a = jnp.maximum(a, 0.0)
        a = (a * a).astype(BF16)
        t = jnp.dot(a, wdown_ref[c * tf:(c + 1) * tf, :], preferred_element_type=F32)
        acc = t if acc is None else acc + t
    o_ref[...] = x + _rms(acc, wpost2_ref[...])


def _proj_mlp(acts, w_out, out_layer, x2, wpost, wpre, wup, wdown, mlp_layer, wpost2, tm, tf=512):
    m = x2.shape[0]
    n_in = len(acts)
    assert sum(a.shape[1] for a in acts) == w_out.shape[1]
    body = functools.partial(_proj_mlp_body, n_in=n_in, tf=tf)
    in_specs = [pl.BlockSpec((tm, a.shape[1]), lambda i: (i, 0)) for a in acts]
    in_specs += [_layer_spec(w_out, out_layer),
                 pl.BlockSpec((tm, D_MODEL), lambda i: (i, 0)),
                 _const_spec((1, D_MODEL)), _const_spec((1, D_MODEL)),
                 _layer_spec(wup, mlp_layer), _layer_spec(wdown, mlp_layer),
                 _const_spec((1, D_MODEL))]
    return pl.pallas_call(
        body,
        grid=(m // tm,),
        in_specs=in_specs,
        out_specs=pl.BlockSpec((tm, D_MODEL), lambda i: (i, 0)),
        out_shape=jax.ShapeDtypeStruct((m, D_MODEL), F32),
        compiler_params=_cparams("parallel"),
        name="proj_mlp",
    )(*acts, w_out, x2, wpost, wpre, wup, wdown, wpost2)


def _attn_prompt_body(slope_ref, q_ref, k_ref, v_ref, o_ref, *scr, seq):
    hp = pl.program_id(1)
    o_scr, m_scr, l_scr, bias_scr = scr[0:3], scr[3:6], scr[6:9], scr[9:12]
    qb = LANES
    n_units = seq // qb
    lane = lax.broadcasted_iota(jnp.int32, (qb, LANES), 1)
    head0 = lane < DH_A

    cfg = []
    for bi, (win, dil) in enumerate(BRANCHES):
        nqb = seq // dil // qb
        nk = 2 * qb if nqb >= 2 else qb
        rowi = lax.broadcasted_iota(jnp.int32, (2 * qb, nk), 0)
        coli = lax.broadcasted_iota(jnp.int32, (2 * qb, nk), 1)
        d = (nk - qb) + (rowi & (qb - 1)) - coli
        slope = jnp.where(rowi < qb, slope_ref[2 * hp], slope_ref[2 * hp + 1])
        bias_scr[bi][...] = jnp.where((d >= 0) & (d <= win // dil), -(slope * dil) * d.astype(F32), NEG_BIG)
        cfg.append((dil, nqb, nk))

    def step(u, carry):
        for bi, (dil, nqb, nk) in enumerate(cfg):
            def rows(start, dil=dil):
                if dil == 1:
                    return pl.ds(pl.multiple_of(start, qb), qb)
                return pl.ds(start, qb, stride=dil)

            r, i = (u, 0) if nqb == 1 else (u // nqb, u % nqb)
            cur = rows(r + dil * qb * i)
            q = q_ref[cur, :]
            qs = jnp.concatenate([jnp.where(head0, q, 0.0), jnp.where(head0, 0.0, q)], axis=0).astype(BF16)
            if nk == qb:
                kk = k_ref[cur, :].astype(BF16)
                vv = v_ref[cur, :].astype(BF16)
            else:
                prev = rows(r + dil * qb * jnp.maximum(i - 1, 0))
                kk = jnp.concatenate([k_ref[prev, :], k_ref[cur, :]], axis=0).astype(BF16)
                vv = jnp.concatenate([v_ref[prev, :], v_ref[cur, :]], axis=0).astype(BF16)
            s = lax.dot_general(qs, kk, (((1,), (1,)), ((), ())), preferred_element_type=F32)
            s = s + bias_scr[bi][...]
            if nk != qb:
                pen = jnp.where(i == 0, NEG_BIG, 0.0)
                s = jnp.concatenate([s[:, :qb] + pen, s[:, qb:]], axis=1)
            mx = jnp.max(s, axis=1, keepdims=True)
            p = jnp.exp(s - mx).astype(BF16)
            va = jnp.concatenate([vv, jnp.ones((nk, LANES), BF16)], axis=1)
            od = jnp.dot(p, va, preferred_element_type=F32)
            o_scr[bi][cur, :] = jnp.where(head0, od[:qb, :LANES], od[qb:, :LANES])
            m_scr[bi][cur, :] = jnp.where(head0, mx[:qb], mx[qb:])
            l_scr[bi][cur, :] = jnp.where(head0, od[:qb, LANES:], od[qb:, LANES:])
        return carry

    lax.fori_loop(0, n_units, step, 0, unroll=8)

    cb = 256

    def combine(c, carry):
        sl = pl.ds(pl.multiple_of(c * cb, cb), cb)
        ms = [m_scr[b][sl, :] for b in range(3)]
        mx = jnp.maximum(jnp.maximum(ms[0], ms[1]), ms[2])
        num = None
        den = None
        for b in range(3):
            w = jnp.exp(ms[b] - mx)
            tn = o_scr[b][sl, :] * w
            td = l_scr[b][sl, :] * w
            num = tn if num is None else num + tn
            den = td if den is None else den + td
        o_ref[sl, :] = (num / den).astype(o_ref.dtype)
        return carry

    lax.fori_loop(0, seq // cb, combine, 0)


def _attn_prompt(q, k, v, slopes):
    b, seq, _ = q.shape
    assert seq % (16 * LANES) == 0
    blk = pl.BlockSpec((None, seq, LANES), lambda bb, hp: (bb, 0, hp))
    body = functools.partial(_attn_prompt_body, seq=seq)
    return pl.pallas_call(
        body,
        grid=(b, D_A // LANES),
        in_specs=[pl.BlockSpec(memory_space=pltpu.SMEM), blk, blk, blk],
        out_specs=blk,
        out_shape=jax.ShapeDtypeStruct((b, seq, D_A), BF16),
        scratch_shapes=[pltpu.VMEM((seq, LANES), F32) for _ in range(9)]
        + [pltpu.VMEM((2 * LANES, 2 * LANES if seq // dil >= 2 * LANES else LANES), F32) for _, dil in BRANCHES],
        compiler_params=_cparams("parallel", "parallel"),
        name="attn_prompt",
    )(slopes, q, k, v)


def _attn_sample_body(slope_ref, q_ref, kn_ref, vn_ref, kt_ref, vt_ref, o_ref, bias_c, bias_n, *, t_new, w_buf):
    nt = (((1,), (1,)), ((), ()))

    @pl.when(pl.program_id(0) == 0)
    def _():
        def table(ref, ncols, dist0):
            rowi = lax.broadcasted_iota(jnp.int32, (t_new, ncols), 0)
            coli = lax.broadcasted_iota(jnp.int32, (t_new, ncols), 1)
            d = dist0 + rowi - coli
            mult = jnp.zeros((t_new, ncols), jnp.int32)
            for win, dil in BRANCHES:
                mult = mult + jnp.where((d >= 0) & (d <= win) & ((d & (dil - 1)) == 0), 1, 0)
            logm = jnp.where(mult == 3, math.log(3.0), jnp.where(mult == 2, math.log(2.0), 0.0))
            for h in range(H_A):
                ref[h] = jnp.where(mult > 0, logm - slope_ref[h] * d.astype(F32), NEG_BIG)

        table(bias_c, w_buf, w_buf)
        table(bias_n, t_new, 0)

    outs = []
    for h in range(H_A):
        cols = slice(h * DH_A, (h + 1) * DH_A)
        q = q_ref[:, cols].astype(BF16)
        s_c = jnp.dot(q, kt_ref[h].astype(BF16), preferred_element_type=F32) + bias_c[h]
        s_n = lax.dot_general(q, kn_ref[:, cols].astype(BF16), nt, preferred_element_type=F32) + bias_n[h]
        mx = jnp.maximum(jnp.max(s_c, axis=1, keepdims=True), jnp.max(s_n, axis=1, keepdims=True))
        p_c = jnp.exp(s_c - mx)
        p_n = jnp.exp(s_n - mx)
        den = jnp.sum(p_c, axis=1, keepdims=True) + jnp.sum(p_n, axis=1, keepdims=True)
        o = lax.dot_general(p_c.astype(BF16), vt_ref[h].astype(BF16), nt, preferred_element_type=F32)
        o = o + jnp.dot(p_n.astype(BF16), vn_ref[:, cols].astype(BF16), preferred_element_type=F32)
        outs.append(o / den)
    o_ref[...] = jnp.concatenate(outs, axis=1)


def _attn_sample(q, kn, vn, cache_k, cache_v, layer, slopes):
    b, t_new, _ = q.shape
    w_buf = cache_k.shape[2]
    assert w_buf >= WINDOW and t_new % SUBLANES == 0
    kt = jnp.transpose(cache_k, (0, 1, 3, 4, 2))
    vt = jnp.transpose(cache_v, (0, 1, 3, 4, 2))
    new_blk = pl.BlockSpec((None, t_new, D_A), lambda bb: (bb, 0, 0))
    cache_blk = pl.BlockSpec((None, None, H_A, DH_A, w_buf), lambda bb: (layer, bb, 0, 0, 0))
    body = functools.partial(_attn_sample_body, t_new=t_new, w_buf=w_buf)
    return pl.pallas_call(
        body,
        grid=(b,),
        in_specs=[pl.BlockSpec(memory_space=pltpu.SMEM), new_blk, new_blk, new_blk, cache_blk, cache_blk],
        out_specs=new_blk,
        out_shape=jax.ShapeDtypeStruct((b, t_new, D_A), F32),
        scratch_shapes=[pltpu.VMEM((H_A, t_new, w_buf), F32), pltpu.VMEM((H_A, t_new, t_new), F32)],
        compiler_params=_cparams("arbitrary"),
        name="attn_sample",
    )(slopes, q, kn, vn, kt, vt)


S5_UT = D_B // LANES
S5_TW = S5_N // S5_UT


def _gelu_tanh(x):
    c = math.sqrt(2.0 / math.pi)
    return 0.5 * x * (1.0 + jnp.tanh(c * (x + 0.044715 * (x * x * x))))


def _s5_body(u_ref, h0_ref, perm_ref, permt_ref, wb_ref, lam_ref, wc_ref, dskip_ref, wglu_ref, bglu_ref,
             o_ref, hl_ref, bu_scr, up_scr, st_scr, *, nb, tsub, nsub):
    step = pl.program_id(0)
    rsub = nb * tsub
    tstep = tsub * nsub

    @pl.when(step == 0)
    def _():
        st_scr[...] = h0_ref[...]

    for sb in range(nsub):
        u_nat = u_ref[:, sb * tsub:(sb + 1) * tsub, :].reshape(rsub, D_B)
        parts = [jnp.dot(perm_ref[...], p, preferred_element_type=F32) for p in _split3(u_nat)]
        up_scr[sb * rsub:(sb + 1) * rsub, :] = parts[0] + parts[1] + parts[2]
    for j in range(S5_UT):
        uj = up_scr[:, j * LANES:(j + 1) * LANES].astype(BF16)
        r = jnp.dot(uj, wb_ref[j], preferred_element_type=F32)
        bu_scr[:, j * S5_TW:(j + 1) * S5_TW] = r[:, :S5_TW]
        bu_scr[:, S5_N + j * S5_TW:S5_N + (j + 1) * S5_TW] = r[:, S5_TW:]

    tiles_per_pass = 4
    for pg in range(S5_N // LANES // tiles_per_pass):
        cols = [(pg * tiles_per_pass + i) * LANES for i in range(tiles_per_pass)]
        lr = [jnp.broadcast_to(lam_ref[0:1, c:c + LANES], (nb, LANES)) for c in cols]
        li = [jnp.broadcast_to(lam_ref[1:2, c:c + LANES], (nb, LANES)) for c in cols]
        init = tuple(st_scr[:, c:c + LANES] for c in cols) + tuple(st_scr[:, S5_N + c:S5_N + c + LANES] for c in cols)

        def tick(t, carry, cols=cols, lr=lr, li=li):
            row = pl.ds(t * nb, nb)
            hr, hi = carry[:tiles_per_pass], carry[tiles_per_pass:]
            nr, ni = [], []
            for i, c in enumerate(cols):
                br = bu_scr[row, c:c + LANES]
                bi = bu_scr[row, S5_N + c:S5_N + c + LANES]
                r_new = lr[i] * hr[i] - li[i] * hi[i] + br
                i_new = lr[i] * hi[i] + li[i] * hr[i] + bi
                bu_scr[row, c:c + LANES] = r_new
                bu_scr[row, S5_N + c:S5_N + c + LANES] = i_new
                nr.append(r_new)
                ni.append(i_new)
            return tuple(nr) + tuple(ni)

        fin = init
        for t in range(tstep):
            fin = tick(t, fin)
        for i, c in enumerate(cols):
            st_scr[:, c:c + LANES] = fin[i]
            st_scr[:, S5_N + c:S5_N + c + LANES] = fin[tiles_per_pass + i]

    hl_ref[...] = st_scr[...]

    ys = []
    for j in range(S5_UT):
        hre = bu_scr[:, j * S5_TW:(j + 1) * S5_TW].astype(BF16)
        him = bu_scr[:, S5_N + j * S5_TW:S5_N + (j + 1) * S5_TW].astype(BF16)
        y = jnp.dot(hre, wc_ref[j, :S5_TW, :], preferred_element_type=F32)
        y = y + jnp.dot(him, wc_ref[j, S5_TW:, :], preferred_element_type=F32)
        ys.append(y)
    y = jnp.concatenate(ys, axis=1) + dskip_ref[...] * up_scr[...]
    g = _gelu_tanh(y)
    gate = jnp.dot(g.astype(BF16), wglu_ref[...], preferred_element_type=F32) + bglu_ref[...]
    out = (g * jax.nn.sigmoid(gate)).astype(BF16)
    for sb in range(nsub):
        o_nat = jnp.dot(permt_ref[...], out[sb * rsub:(sb + 1) * rsub, :], preferred_element_type=F32)
        o_ref[:, sb * tsub:(sb + 1) * tsub, :] = o_nat.reshape(nb, tsub, D_B).astype(o_ref.dtype)


def _s5_mixer(u, h0, prm, tsub, nsub):
    nb, seq, _ = u.shape
    tstep = tsub * nsub
    rsub = nb * tsub
    rows = nb * tstep
    assert seq % tstep == 0 and nb % SUBLANES == 0 and tsub % SUBLANES == 0
    perm = np.zeros((rsub, rsub), np.float32)
    for t in range(tsub):
        for bb in range(nb):
            perm[t * nb + bb, bb * tsub + t] = 1.0
    body = functools.partial(_s5_body, nb=nb, tsub=tsub, nsub=nsub)
    return pl.pallas_call(
        body,
        grid=(seq // tstep,),
        in_specs=[pl.BlockSpec((nb, tstep, D_B), lambda i: (0, i, 0)),
                  _const_spec((nb, 2 * S5_N)),
                  _const_spec((rsub, rsub)), _const_spec((rsub, rsub)),
                  _const_spec((S5_UT, LANES, 2 * S5_TW)),
                  _const_spec((2, S5_N)),
                  _const_spec((S5_UT, 2 * S5_TW, LANES)),
                  _const_spec((1, D_B)),
                  _const_spec((D_B, D_B)),
                  _const_spec((1, D_B))],
        out_specs=[pl.BlockSpec((nb, tstep, D_B), lambda i: (0, i, 0)),
                   pl.BlockSpec((nb, 2 * S5_N), lambda i: (0, 0))],
        out_shape=[jax.ShapeDtypeStruct((nb, seq, D_B), _act_dtype(tsub)),
                   jax.ShapeDtypeStruct((nb, 2 * S5_N), F32)],
        scratch_shapes=[pltpu.VMEM((rows, 2 * S5_N), F32),
                        pltpu.VMEM((rows, D_B), F32),
                        pltpu.VMEM((nb, 2 * S5_N), F32)],
        compiler_params=_cparams("arbitrary"),
        name="s5_mixer",
    )(u, h0, jnp.asarray(perm, BF16), jnp.asarray(perm.T, BF16), prm["wb"], prm["lam"], prm["wc"],
      prm["dskip"], prm["wglu"], prm["bglu"])


def _s5_params(lam_re, lam_im, log_dt, b_re, b_im, c_re, c_im, d_skip, w_glu, b_glu):
    lam = lax.complex(lam_re.astype(F32), lam_im.astype(F32))
    dt = jnp.exp(log_dt.astype(F32))[:, None]
    lam_bar = jnp.exp(lam * dt)
    b_bar = ((lam_bar - 1.0) / lam)[..., None] * lax.complex(b_re.astype(F32), b_im.astype(F32))
    gpt = LANES // CH_B
    eye = jnp.eye(gpt, dtype=F32)

    def b_blocks(part):
        x = part.reshape(S5_UT, gpt, P_B, CH_B)
        x = jnp.einsum("jgpc,gh->jgchp", x, eye)
        return x.reshape(S5_UT, gpt * CH_B, gpt * P_B)

    def c_blocks(part):
        x = part.reshape(S5_UT, gpt, CH_B, P_B)
        x = jnp.einsum("jgcp,gh->jgphc", x, eye)
        return x.reshape(S5_UT, gpt * P_B, gpt * CH_B)

    wb = jnp.concatenate([b_blocks(b_bar.real), b_blocks(b_bar.imag)], axis=2).astype(BF16)
    wc = jnp.concatenate([c_blocks(c_re.astype(F32)), -c_blocks(c_im.astype(F32))], axis=1).astype(BF16)
    lam2 = jnp.stack([lam_bar.real.reshape(S5_N), lam_bar.imag.reshape(S5_N)])
    return dict(wb=wb, wc=wc, lam=lam2, dskip=d_skip.astype(F32).reshape(1, D_B),
                wglu=w_glu.astype(BF16), bglu=b_glu.astype(F32).reshape(1, D_B))


HIST = SUBLANES
SSD_CHUNKS_PER_STEP = 4
XB0 = D_INNER
XC0 = D_INNER + G_C * N_C
GW = D_INNER // G_C


def _expand_heads(a, e_ref):
    e = e_ref[...]
    a1, a2, _ = _split3(a)
    return jnp.dot(a1, e, preferred_element_type=F32) + jnp.dot(a2, e, preferred_element_type=F32)


def _ssd_body(z_ref, xbc_ref, dt_ref, ssm0_ref, a_ref, dexp_ref, gw_ref, e_ref, *rest, lc, activated, nsub):
    if activated:
        y_ref, hl_ref, state = rest[-3:]
    else:
        conv0_ref, convw_ref, convb_ref, dtb_ref = rest[:4]
        y_ref, convst_ref, hl_ref, xpad, state = rest[-5:]
    ch = pl.program_id(1)
    cl = SSD_CHUNK

    @pl.when(ch == 0)
    def _():
        state[...] = ssm0_ref[...]

    if activated:
        assert lc == nsub * cl
    else:
        @pl.when(ch == 0)
        def _():
            xpad[0:HIST, :] = conv0_ref[...]

        if lc < cl:
            xpad[HIST:, :] = jnp.zeros((cl, CONV_DIM), F32)
        xpad[HIST:HIST + lc, :] = xbc_ref[...]
        conv = convb_ref[...] + convw_ref[CONV_K - 1:CONV_K, :] * xpad[HIST:HIST + cl, :]
        for kk in range(CONV_K - 1):
            sh = CONV_K - 1 - kk
            conv = conv + convw_ref[kk:kk + 1, :] * xpad[HIST - sh:HIST - sh + cl, :]
        xs = conv * jax.nn.sigmoid(conv)
        tail = xpad[lc:lc + HIST, :]
        xpad[0:HIST, :] = tail
        convst_ref[...] = tail

        dt = jax.nn.softplus(dt_ref[...] + dtb_ref[...])
        if lc < cl:
            dt = jnp.concatenate([dt, jnp.zeros((cl - lc, LANES), F32)], axis=0)
    def chunk(xs, dt, zact):
        x = xs[:, :D_INNER]
        bgs, cbs, y_offs = [], [], []
        for g in range(G_C):
            bg = xs[:, XB0 + g * N_C:XB0 + (g + 1) * N_C].astype(BF16)
            cg = xs[:, XC0 + g * N_C:XC0 + (g + 1) * N_C].astype(BF16)
            prev = state[g * GW:(g + 1) * GW, :]
            bgs.append(bg)
            cbs.append(lax.dot_general(cg, bg, (((1,), (1,)), ((), ())), preferred_element_type=F32))
            y_offs.append(lax.dot_general(cg, prev.astype(BF16), (((1,), (1,)), ((), ())),
                                          preferred_element_type=F32))

        da = dt * a_ref[...]
        ri = lax.broadcasted_iota(jnp.int32, (cl, cl), 0)
        ci = lax.broadcasted_iota(jnp.int32, (cl, cl), 1)
        tri = ri >= ci
        tri01 = jnp.where(tri, 1.0, 0.0).astype(BF16)
        cs = [jnp.dot(tri01, p, preferred_element_type=F32) for p in _split3(da)]
        a_cs = cs[0] + cs[1] + cs[2]
        a_cs_t = a_cs.T
        dt_t = dt.T
        e_cs = jnp.exp(a_cs)
        tot = a_cs[cl - 1:cl, :]
        w_end = jnp.exp(tot - a_cs) * dt
        both = _expand_heads(jnp.concatenate([e_cs, w_end], axis=0), e_ref)
        scale_x, wend_x = both[:cl], both[cl:]

        xw = (x * wend_x).astype(BF16)
        lane = lax.broadcasted_iota(jnp.int32, (cl, LANES), 1)
        head0 = lane < P_C
        y_parts = []
        for g in range(G_C):
            bg, cb, y_off = bgs[g], cbs[g], y_offs[g]
            for jp in range(J_C // 2):
                ms = []
                for hh in range(2):
                    h = g * J_C + 2 * jp + hh
                    seg = a_cs[:, h:h + 1] - a_cs_t[h:h + 1, :]
                    dec = jnp.exp(jnp.where(tri, seg, NEG_BIG))
                    ms.append((cb * dec * dt_t[h:h + 1, :]).astype(BF16))
                c0 = g * GW + jp * LANES
                xp = x[:, c0:c0 + LANES].astype(BF16)
                r = jnp.dot(jnp.concatenate(ms, axis=0), xp, preferred_element_type=F32)
                yd = jnp.where(head0, r[:cl], r[cl:])
                y_parts.append(yd + y_off[:, jp * LANES:(jp + 1) * LANES] * scale_x[:, c0:c0 + LANES])
            new = lax.dot_general(xw[:, g * GW:(g + 1) * GW], bg, (((0,), (0,)), ((), ())),
                                  preferred_element_type=F32)
            for j in range(J_C):
                h = g * J_C + j
                r0 = g * GW + j * P_C
                state[r0:r0 + P_C, :] = state[r0:r0 + P_C, :] * jnp.exp(tot[:, h:h + 1]) + new[j * P_C:(j + 1) * P_C, :]

        y = jnp.concatenate(y_parts, axis=1) + dexp_ref[...] * x
        y = y[:lc] if lc < cl else y
        gt = y * zact
        outs = []
        for g in range(G_C):
            gg = gt[:, g * GW:(g + 1) * GW]
            ms = jnp.mean(gg * gg, axis=-1, keepdims=True)
            outs.append(gg * lax.rsqrt(ms + RMS_EPS))
        return jnp.concatenate(outs, axis=1) * gw_ref[...]

    if activated:
        for sc in range(nsub):
            rows = slice(sc * cl, (sc + 1) * cl)
            y_ref[rows, :] = chunk(xbc_ref[rows, :], dt_ref[rows, :], z_ref[rows, :]).astype(y_ref.dtype)
    else:
        z = z_ref[...]
        y_ref[...] = chunk(xs, dt, z * jax.nn.sigmoid(z)).astype(y_ref.dtype)
    hl_ref[...] = state[...]


def _ssd_mixer(z, xbc, dtr, conv0, ssm0, ssm_layer, prm, stack=None, layer=0, n_layers=1, activated=False):
    b, seq, _ = z.shape
    lc = SSD_CHUNK if seq % SSD_CHUNK == 0 else seq
    assert lc % SUBLANES == 0 and CONV_K - 1 <= lc <= SSD_CHUNK
    nsub = 1
    if activated:
        nsub = SSD_CHUNKS_PER_STEP if seq % (SSD_CHUNKS_PER_STEP * SSD_CHUNK) == 0 else 1
        lc = nsub * SSD_CHUNK
    nch = seq // lc
    body = functools.partial(_ssd_body, lc=lc, activated=activated, nsub=nsub)
    keep = (lambda items: [it for i, it in enumerate(items) if i != 1]) if activated else (lambda items: items)
    conv_args = [] if activated else [conv0, prm["convw"], prm["convb"], prm["dtb"]]

    def row_blk(n):
        return pl.BlockSpec((None, lc, n), lambda bb, c: (bb, c, 0))

    def per_b(r, n):
        return pl.BlockSpec((None, r, n), lambda bb, c: (bb, 0, 0))

    conv_specs = [] if activated else [per_b(HIST, CONV_DIM), _const_spec((CONV_K, CONV_DIM)),
                                       _const_spec((1, CONV_DIM)), _const_spec((1, LANES))]

    return pl.pallas_call(
        body,
        grid=(b, nch),
        in_specs=[row_blk(D_INNER), row_blk(CONV_DIM), row_blk(LANES),
                  pl.BlockSpec((None, None, D_INNER, N_C), lambda bb, c: (ssm_layer, bb, 0, 0)),
                  _const_spec((1, LANES)), _const_spec((1, D_INNER)), _const_spec((1, D_INNER)),
                  _const_spec((LANES, D_INNER))] + conv_specs
        + ([] if stack is None else [pl.BlockSpec(memory_space=pl.ANY)]),
        out_specs=keep([row_blk(D_INNER), per_b(HIST, CONV_DIM),
                        pl.BlockSpec((None, D_INNER, N_C), lambda bb, c: (layer * b + bb, 0, 0))]),
        out_shape=keep([jax.ShapeDtypeStruct((b, seq, D_INNER), _act_dtype(lc)),
                        jax.ShapeDtypeStruct((b, HIST, CONV_DIM), F32),
                        jax.ShapeDtypeStruct((n_layers * b, D_INNER, N_C), F32)]),
        scratch_shapes=([] if activated else [pltpu.VMEM((HIST + SSD_CHUNK, CONV_DIM), F32)])
        + [pltpu.VMEM((D_INNER, N_C), F32)],
        input_output_aliases={} if stack is None else {8 + len(conv_args): 1 if activated else 2},
        compiler_params=_cparams("parallel", "arbitrary"),
        name="ssd_mixer",
    )(z, xbc, dtr, ssm0, prm["a"], prm["dexp"], prm["gw"], prm["expand"], *conv_args,
      *(() if stack is None else (stack,)))


def _ssd_params(conv_w, conv_b, dt_bias, a_log, d_skip, gnorm_w):
    pad = LANES - H_C
    expand = np.zeros((LANES, D_INNER), np.float32)
    for h in range(H_C):
        expand[h, h * P_C:(h + 1) * P_C] = 1.0
    return dict(
        convw=conv_w.astype(F32), convb=conv_b.astype(F32).reshape(1, CONV_DIM),
        dtb=jnp.pad(dt_bias.astype(F32), (0, pad)).reshape(1, LANES),
        a=jnp.pad(-jnp.exp(a_log.astype(F32)), (0, pad)).reshape(1, LANES),
        dexp=jnp.repeat(d_skip.astype(F32), P_C).reshape(1, D_INNER),
        gw=gnorm_w.astype(F32).reshape(1, D_INNER),
        expand=jnp.asarray(expand, BF16))


def _trunk(x, p, slopes, caches, tm):
    b, seq, _ = x.shape
    m = b * seq
    x2 = x.reshape(m, D_MODEL)
    new_s5, new_conv = [], []
    kv_stacks, ssm_stack = None, None
    kv_transposed = seq % tm == 0
    n_even, n_odd = (DEPTH + 1) // 2, DEPTH // 2
    for i in range(DEPTH):
        j = i // 2
        wn_pre = p["norm_mix_pre"][i].reshape(1, D_MODEL)
        if i % 2 == 0:
            assert seq <= WINDOW
            q, k, v, u, *kv_stacks = _norm_proj(x2, wn_pre, p["w_in_even"], j, (D_A, D_A, D_A, D_B),
                                                (DH_A ** -0.5, 1.0, 1.0, 1.0), tm, head_splits=(1, 2),
                                                stacks=kv_stacks, layer=j, n_layers=n_even,
                                                seq=seq if kv_transposed else None)
            q3, k3, v3 = (t.reshape(b, seq, D_A) for t in (q, k, v))
            if caches is None:
                o_a = _attn_prompt(q3, k3, v3, slopes)
                h0 = jnp.zeros((b, 2 * S5_N), F32)
                tsub, nsub = 16, 4
            else:
                o_a = _attn_sample(q3, k3, v3, caches["k"], caches["v"], j, slopes)
                s0 = caches["s5"][j].astype(F32)
                h0 = jnp.concatenate([s0[..., 0].reshape(b, S5_N), s0[..., 1].reshape(b, S5_N)], axis=1)
                tsub, nsub = seq, 1
            o_b, hl = _s5_mixer(u.reshape(b, seq, D_B), h0, p["s5"][j], tsub, nsub)
            new_s5.append(jnp.stack([hl[:, :S5_N].reshape(b, G_B, P_B), hl[:, S5_N:].reshape(b, G_B, P_B)], axis=-1))
            acts = [o_a.reshape(m, D_A), o_b.reshape(m, D_B)]
            w_out = p["w_out_even"]
        else:
            if caches is None and seq % tm == 0:
                z, xbc, dtr, tails = _norm_proj_ssd(x2, wn_pre, p["w_in_odd"], j, p["ssd"][j], tm, seq)
                ssm0 = jnp.zeros((1, b, D_INNER, N_C), F32)
                yg, ssm_stack = _ssd_mixer(z.reshape(b, seq, D_INNER), xbc.reshape(b, seq, CONV_DIM),
                                           dtr.reshape(b, seq, LANES), None, ssm0, 0, p["ssd"][j],
                                           stack=ssm_stack, layer=j, n_layers=n_odd, activated=True)
                convst = tails.reshape(b, seq // tm, HIST, CONV_DIM)[:, -1]
            else:
                z, xbc, dtr = _norm_proj(x2, wn_pre, p["w_in_odd"], j, (D_INNER, CONV_DIM, LANES),
                                         (1.0, 1.0, 1.0), tm)
                if caches is None:
                    conv0 = jnp.zeros((b, HIST, CONV_DIM), F32)
                    ssm0, ssm_layer = jnp.zeros((1, b, D_INNER, N_C), F32), 0
                else:
                    conv0 = jnp.pad(caches["conv"][j].astype(F32), ((0, 0), (HIST - (CONV_K - 1), 0), (0, 0)))
                    ssm0, ssm_layer = caches["ssm"].astype(F32).reshape(-1, b, D_INNER, N_C), j
                yg, convst, ssm_stack = _ssd_mixer(z.reshape(b, seq, D_INNER), xbc.reshape(b, seq, CONV_DIM),
                                                   dtr.reshape(b, seq, LANES), conv0, ssm0, ssm_layer, p["ssd"][j],
                                                   stack=ssm_stack, layer=j, n_layers=n_odd)
            new_conv.append(convst[:, HIST - (CONV_K - 1):])
            acts = [yg.reshape(m, D_INNER)]
            w_out = p["w_out_odd"]
        x2 = _proj_mlp(acts, w_out, j, x2, p["norm_mix_post"][i].reshape(1, D_MODEL),
                       p["norm_mlp_pre"][i].reshape(1, D_MODEL), p["w_mlp_up"], p["w_mlp_down"], i,
                       p["norm_mlp_post"][i].reshape(1, D_MODEL), tm)
    if kv_transposed:
        k_all, v_all = (jnp.transpose(t, (0, 1, 4, 2, 3)) for t in kv_stacks)
    else:
        k_all, v_all = (t.reshape(n_even, b, seq, H_A, DH_A) for t in kv_stacks)
    return (x2.reshape(b, seq, D_MODEL), k_all, v_all, jnp.stack(new_s5), jnp.stack(new_conv),
            ssm_stack.reshape(n_odd, b, H_C, P_C, N_C))


def kernel(x_prompt, x_sample, cache_k, cache_v, state_s5, state_conv, state_ssm, norm_mix_pre, norm_mix_post, norm_mlp_pre, norm_mlp_post, w_mlp_up, w_mlp_down, w_in_even, w_out_even, s5_lambda_re, s5_lambda_im, s5_log_dt, s5_b_re, s5_b_im, s5_c_re, s5_c_im, s5_d, s5_w_glu, s5_b_glu, w_in_odd, conv_w, conv_b, dt_bias, a_log, d_skip, gnorm_w, w_out_odd):
    n_even, n_odd = w_in_even.shape[0], w_in_odd.shape[0]
    odd_pad = LANES - H_C
    p = dict(
        norm_mix_pre=norm_mix_pre.astype(F32), norm_mix_post=norm_mix_post.astype(F32),
        norm_mlp_pre=norm_mlp_pre.astype(F32), norm_mlp_post=norm_mlp_post.astype(F32),
        w_mlp_up=w_mlp_up.astype(BF16), w_mlp_down=w_mlp_down.astype(BF16),
        w_in_even=w_in_even.astype(BF16), w_out_even=w_out_even.astype(BF16),
        w_in_odd=jnp.pad(w_in_odd, ((0, 0), (0, 0), (0, odd_pad))).astype(BF16),
        w_out_odd=w_out_odd.astype(BF16),
        s5=[_s5_params(s5_lambda_re[j], s5_lambda_im[j], s5_log_dt[j], s5_b_re[j], s5_b_im[j], s5_c_re[j],
                       s5_c_im[j], s5_d[j], s5_w_glu[j], s5_b_glu[j]) for j in range(n_even)],
        ssd=[_ssd_params(conv_w[j], conv_b[j], dt_bias[j], a_log[j], d_skip[j], gnorm_w[j]) for j in range(n_odd)],
    )
    slopes = jnp.asarray(np.power(2.0, -8.0 * np.arange(1, H_A + 1) / H_A), dtype=F32)
    caches = dict(k=cache_k, v=cache_v, s5=state_s5, conv=state_conv, ssm=state_ssm)
    y_p, k_p, v_p, s5_p, conv_p, ssm_p = _trunk(x_prompt, p, slopes, None, tm=512)
    y_s, k_s, v_s, s5_s, conv_s, ssm_s = _trunk(x_sample, p, slopes, caches, tm=256)
    return (y_p, y_s, k_p, v_p, s5_p, conv_p, ssm_p, k_s, v_s, s5_s, conv_s, ssm_s)
```

```python
import functools
import math

import numpy as np
import jax
import jax.numpy as jnp
from jax import lax
from jax.experimental import pallas as pl
from jax.experimental.pallas import tpu as pltpu

F32 = jnp.float32
BF16 = jnp.bfloat16

D_MODEL = 1024
DEPTH = 4
PAST_LEN = 8192
RMS_EPS = 1e-6
D_FF = 4 * D_MODEL
D_A = D_MODEL // 2
DH_A = 64
H_A = D_A // DH_A
BRANCHES = ((128, 1), (512, 4), (2048, 16))
WINDOW = 2048
D_B = D_MODEL - D_A
CH_B = 16
G_B = D_B // CH_B
P_B = 64
S5_N = G_B * P_B
D_INNER = 2 * D_MODEL
P_C = 64
H_C = D_INNER // P_C
G_C = 4
J_C = H_C // G_C
N_C = 128
CONV_K = 4
CONV_DIM = D_INNER + 2 * G_C * N_C
SSD_CHUNK = 128

LANES = 128
SUBLANES = 8
VMEM_LIMIT = 56 * 1024 * 1024
NEG_BIG = -1e30


def _act_dtype(rows):
    return BF16 if rows % (2 * SUBLANES) == 0 else F32


def _cparams(*sem):
    return pltpu.CompilerParams(dimension_semantics=sem, vmem_limit_bytes=VMEM_LIMIT)


def _rms(x, w):
    ms = jnp.mean(x * x, axis=-1, keepdims=True)
    return x * lax.rsqrt(ms + RMS_EPS) * w


def _split3(a):
    a1 = a.astype(BF16)
    r1 = a - a1.astype(F32)
    a2 = r1.astype(BF16)
    a3 = (r1 - a2.astype(F32)).astype(BF16)
    return a1, a2, a3


def _const_spec(shape):
    nd = len(shape)
    return pl.BlockSpec(shape, lambda *_: (0,) * nd, pipeline_mode=pl.Buffered(1))


def _layer_spec(stacked, layer):
    nd = stacked.ndim - 1
    return pl.BlockSpec((None,) + stacked.shape[1:], lambda *_: (layer,) + (0,) * nd,
                        pipeline_mode=pl.Buffered(1))


def _norm_proj_body(*refs, splits, scales, head_splits, n_stacks, tokens_minor):
    x_ref, wn_ref, w_ref = refs[:3]
    out_refs = refs[3 + n_stacks:3 + n_stacks + len(splits)]
    head_refs = refs[3 + n_stacks + len(splits):]
    h = _rms(x_ref[...], wn_ref[...]).astype(BF16)
    off = 0
    for idx, (o_ref, n, sc) in enumerate(zip(out_refs, splits, scales)):
        r = jnp.dot(h, w_ref[:, off:off + n], preferred_element_type=F32)
        if sc != 1.0:
            r = r * sc
        o_ref[...] = r.astype(o_ref.dtype)
        if idx in head_splits:
            h_ref = head_refs[head_splits.index(idx)]
            if tokens_minor:
                h_ref[...] = r.T.reshape(H_A, DH_A, r.shape[0])
            else:
                h_ref[...] = r.reshape(r.shape[0], H_A, DH_A)
        off += n


def _norm_proj(x2, wn, w, w_layer, splits, scales, tm, head_splits=(), stacks=None, layer=0, n_layers=1, seq=None):
    m = x2.shape[0]
    n_total = w.shape[2]
    assert sum(splits) == n_total and m % tm == 0
    nblk = m // tm
    n_stacks = 0 if stacks is None else len(stacks)
    tokens_minor = seq is not None
    body = functools.partial(_norm_proj_body, splits=splits, scales=scales, head_splits=tuple(head_splits),
                             n_stacks=n_stacks, tokens_minor=tokens_minor)
    in_specs = [pl.BlockSpec((tm, D_MODEL), lambda i: (i, 0)),
                _const_spec((1, D_MODEL)),
                _layer_spec(w, w_layer)]
    in_specs += [pl.BlockSpec(memory_space=pl.ANY)] * n_stacks
    out_specs = [pl.BlockSpec((tm, n), lambda i: (i, 0)) for n in splits]
    out_shape = [jax.ShapeDtypeStruct((m, n), F32) for n in splits]
    if tokens_minor:
        assert seq % tm == 0 and tm % LANES == 0
        per_seq = seq // tm
        out_specs += [pl.BlockSpec((None, None, H_A, DH_A, tm),
                                   lambda i: (layer, i // per_seq, 0, 0, i % per_seq)) for _ in head_splits]
        out_shape += [jax.ShapeDtypeStruct((n_layers, m // seq, H_A, DH_A, seq), F32) for _ in head_splits]
    else:
        out_specs += [pl.BlockSpec((tm, H_A, DH_A), lambda i: (layer * nblk + i, 0, 0)) for _ in head_splits]
        out_shape += [jax.ShapeDtypeStruct((n_layers * m, H_A, DH_A), F32) for _ in head_splits]
    aliases = {3 + s: len(splits) + s for s in range(n_stacks)}
    return pl.pallas_call(
        body,
        grid=(nblk,),
        in_specs=in_specs,
        out_specs=out_specs,
        out_shape=out_shape,
        input_output_aliases=aliases,
        compiler_params=_cparams("parallel"),
        name="norm_proj",
    )(x2, wn, w, *(stacks or ()))


CONV_COLS = 512


def _norm_proj_ssd_body(x_ref, wn_ref, w_ref, convw_ref, convb_ref, dtb_ref, zs_ref, xs_ref, dt_ref, tail_ref,
                        carry, xpad, *, tm, per_seq):
    hist = carry.shape[0]

    @pl.when(pl.program_id(0) % per_seq == 0)
    def _():
        carry[...] = jnp.zeros(carry.shape, F32)

    h = _rms(x_ref[...], wn_ref[...]).astype(BF16)
    for c in range(D_INNER // CONV_COLS):
        cols = slice(c * CONV_COLS, (c + 1) * CONV_COLS)
        z = jnp.dot(h, w_ref[:, cols], preferred_element_type=F32)
        zs_ref[:, cols] = z * jax.nn.sigmoid(z)
    for c in range(CONV_DIM // CONV_COLS):
        cols = slice(c * CONV_COLS, (c + 1) * CONV_COLS)
        r = jnp.dot(h, w_ref[:, D_INNER + c * CONV_COLS:D_INNER + (c + 1) * CONV_COLS], preferred_element_type=F32)
        xpad[0:hist, :] = carry[:, cols]
        xpad[hist:, :] = r
        conv = convb_ref[:, cols] + convw_ref[CONV_K - 1:CONV_K, cols] * r
        for kk in range(CONV_K - 1):
            sh = CONV_K - 1 - kk
            conv = conv + convw_ref[kk:kk + 1, cols] * xpad[hist - sh:hist - sh + tm, :]
        xs_ref[:, cols] = conv * jax.nn.sigmoid(conv)
        tail = xpad[tm:, :]
        carry[:, cols] = tail
        tail_ref[:, cols] = tail
    dtr = jnp.dot(h, w_ref[:, D_INNER + CONV_DIM:], preferred_element_type=F32)
    dt_ref[...] = jax.nn.softplus(dtr + dtb_ref[...])


def _norm_proj_ssd(x2, wn, w, w_layer, prm, tm, seq):
    m = x2.shape[0]
    n_total = w.shape[2]
    assert m % tm == 0 and seq % tm == 0 and n_total == D_INNER + CONV_DIM + LANES
    nblk = m // tm
    body = functools.partial(_norm_proj_ssd_body, tm=tm, per_seq=seq // tm)
    return pl.pallas_call(
        body,
        grid=(nblk,),
        in_specs=[pl.BlockSpec((tm, D_MODEL), lambda i: (i, 0)),
                  _const_spec((1, D_MODEL)),
                  _layer_spec(w, w_layer),
                  _const_spec((CONV_K, CONV_DIM)), _const_spec((1, CONV_DIM)), _const_spec((1, LANES))],
        out_specs=[pl.BlockSpec((tm, D_INNER), lambda i: (i, 0)),
                   pl.BlockSpec((tm, CONV_DIM), lambda i: (i, 0)),
                   pl.BlockSpec((tm, LANES), lambda i: (i, 0)),
                   pl.BlockSpec((None, HIST, CONV_DIM), lambda i: (i, 0, 0))],
        out_shape=[jax.ShapeDtypeStruct((m, D_INNER), F32),
                   jax.ShapeDtypeStruct((m, CONV_DIM), F32),
                   jax.ShapeDtypeStruct((m, LANES), F32),
                   jax.ShapeDtypeStruct((nblk, HIST, CONV_DIM), F32)],
        scratch_shapes=[pltpu.VMEM((HIST, CONV_DIM), F32), pltpu.VMEM((HIST + tm, CONV_COLS), F32)],
        compiler_params=_cparams("arbitrary"),
        name="norm_proj_ssd",
    )(x2, wn, w, prm["convw"], prm["convb"], prm["dtb"])


def _proj_mlp_body(*refs, n_in, tf):
    a_refs = refs[:n_in]
    w_ref, x_ref, wpost_ref, wpre_ref, wup_ref, wdown_ref, wpost2_ref, o_ref = refs[n_in:]
    m = None
    row = 0
    for a_ref in a_refs:
        k = a_ref.shape[1]
        t = jnp.dot(a_ref[...].astype(BF16), w_ref[row:row + k, :], preferred_element_type=F32)
        m = t if m is None else m + t
        row += k
    x = x_ref[...] + _rms(m, wpost_ref[...])
    h = _rms(x, wpre_ref[...]).astype(BF16)
    acc = None
    for c in range(D_FF // tf):
        a = jnp.dot(h, wup_ref[:, c * tf:(c + 1) * tf], preferred_element_type=F32)
        a = jnp.maximum(a, 0.0)
        a = (a * a).astype(BF16)
        t = jnp.dot(a, wdown_ref[c * tf:(c + 1) * tf, :], preferred_element_type=F32)
        acc = t if acc is None else acc + t
    o_ref[...] = x + _rms(acc, wpost2_ref[...])


def _proj_mlp(acts, w_out, out_layer, x2, wpost, wpre, wup, wdown, mlp_layer, wpost2, tm, tf=512):
    m = x2.shape[0]
    n_in = len(acts)
    assert sum(a.shape[1] for a in acts) == w_out.shape[1]
    body = functools.partial(_proj_mlp_body, n_in=n_in, tf=tf)
    in_specs = [pl.BlockSpec((tm, a.shape[1]), lambda i: (i, 0)) for a in acts]
    in_specs += [_layer_spec(w_out, out_layer),
                 pl.BlockSpec((tm, D_MODEL), lambda i: (i, 0)),
                 _const_spec((1, D_MODEL)), _const_spec((1, D_MODEL)),
                 _layer_spec(wup, mlp_layer), _layer_spec(wdown, mlp_layer),
                 _const_spec((1, D_MODEL))]
    return pl.pallas_call(
        body,
        grid=(m // tm,),
        in_specs=in_specs,
        out_specs=pl.BlockSpec((tm, D_MODEL), lambda i: (i, 0)),
        out_shape=jax.ShapeDtypeStruct((m, D_MODEL), F32),
        compiler_params=_cparams("parallel"),
        name="proj_mlp",
    )(*acts, w_out, x2, wpost, wpre, wup, wdown, wpost2)


def _attn_prompt_body(slope_ref, q_ref, k_ref, v_ref, o_ref, *scr, seq):
    hp = pl.program_id(1)
    o_scr, m_scr, l_scr, bias_scr = scr[0:3], scr[3:6], scr[6:9], scr[9:12]
    qb = LANES
    n_units = seq // qb
    lane = lax.broadcasted_iota(jnp.int32, (qb, LANES), 1)
    head0 = lane < DH_A

    cfg = []
    for bi, (win, dil) in enumerate(BRANCHES):
        nqb = seq // dil // qb
        nk = 2 * qb if nqb >= 2 else qb
        rowi = lax.broadcasted_iota(jnp.int32, (2 * qb, nk), 0)
        coli = lax.broadcasted_iota(jnp.int32, (2 * qb, nk), 1)
        d = (nk - qb) + (rowi & (qb - 1)) - coli
        slope = jnp.where(rowi < qb, slope_ref[2 * hp], slope_ref[2 * hp + 1])
        bias_scr[bi][...] = jnp.where((d >= 0) & (d <= win // dil), -(slope * dil) * d.astype(F32), NEG_BIG)
        cfg.append((dil, nqb, nk))

    def step(u, carry):
        for bi, (dil, nqb, nk) in enumerate(cfg):
            def rows(start, dil=dil):
                if dil == 1:
                    return pl.ds(pl.multiple_of(start, qb), qb)
                return pl.ds(start, qb, stride=dil)

            r, i = (u, 0) if nqb == 1 else (u // nqb, u % nqb)
            cur = rows(r + dil * qb * i)
            q = q_ref[cur, :]
            qs = jnp.concatenate([jnp.where(head0, q, 0.0), jnp.where(head0, 0.0, q)], axis=0).astype(BF16)
            if nk == qb:
                kk = k_ref[cur, :].astype(BF16)
                vv = v_ref[cur, :].astype(BF16)
            else:
                prev = rows(r + dil * qb * jnp.maximum(i - 1, 0))
                kk = jnp.concatenate([k_ref[prev, :], k_ref[cur, :]], axis=0).astype(BF16)
                vv = jnp.concatenate([v_ref[prev, :], v_ref[cur, :]], axis=0).astype(BF16)
            s = lax.dot_general(qs, kk, (((1,), (1,)), ((), ())), preferred_element_type=F32)
            s = s + bias_scr[bi][...]
            if nk != qb:
                pen = jnp.where(i == 0, NEG_BIG, 0.0)
                s = jnp.concatenate([s[:, :qb] + pen, s[:, qb:]], axis=1)
            mx = jnp.max(s, axis=1, keepdims=True)
            p = jnp.exp(s - mx).astype(BF16)
            va = jnp.concatenate([vv, jnp.ones((nk, LANES), BF16)], axis=1)
            od = jnp.dot(p, va, preferred_element_type=F32)
            o_scr[bi][cur, :] = jnp.where(head0, od[:qb, :LANES], od[qb:, :LANES])
            m_scr[bi][cur, :] = jnp.where(head0, mx[:qb], mx[qb:])
            l_scr[bi][cur, :] = jnp.where(head0, od[:qb, LANES:], od[qb:, LANES:])
        return carry

    lax.fori_loop(0, n_units, step, 0, unroll=8)

    cb = 256

    def combine(c, carry):
        sl = pl.ds(pl.multiple_of(c * cb, cb), cb)
        ms = [m_scr[b][sl, :] for b in range(3)]
        mx = jnp.maximum(jnp.maximum(ms[0], ms[1]), ms[2])
        num = None
        den = None
        for b in range(3):
            w = jnp.exp(ms[b] - mx)
            tn = o_scr[b][sl, :] * w
            td = l_scr[b][sl, :] * w
            num = tn if num is None else num + tn
            den = td if den is None else den + td
        o_ref[sl, :] = (num / den).astype(o_ref.dtype)
        return carry

    lax.fori_loop(0, seq // cb, combine, 0)


def _attn_prompt(q, k, v, slopes):
    b, seq, _ = q.shape
    assert seq % (16 * LANES) == 0
    blk = pl.BlockSpec((None, seq, LANES), lambda bb, hp: (bb, 0, hp))
    body = functools.partial(_attn_prompt_body, seq=seq)
    return pl.pallas_call(
        body,
        grid=(b, D_A // LANES),
        in_specs=[pl.BlockSpec(memory_space=pltpu.SMEM), blk, blk, blk],
        out_specs=blk,
        out_shape=jax.ShapeDtypeStruct((b, seq, D_A), BF16),
        scratch_shapes=[pltpu.VMEM((seq, LANES), F32) for _ in range(9)]
        + [pltpu.VMEM((2 * LANES, 2 * LANES if seq // dil >= 2 * LANES else LANES), F32) for _, dil in BRANCHES],
        compiler_params=_cparams("parallel", "parallel"),
        name="attn_prompt",
    )(slopes, q, k, v)


def _attn_sample_body(slope_ref, q_ref, kn_ref, vn_ref, kt_ref, vt_ref, o_ref, bias_c, bias_n, *, t_new, w_buf):
    nt = (((1,), (1,)), ((), ()))

    @pl.when(pl.program_id(0) == 0)
    def _():
        def table(ref, ncols, dist0):
            rowi = lax.broadcasted_iota(jnp.int32, (t_new, ncols), 0)
            coli = lax.broadcasted_iota(jnp.int32, (t_new, ncols), 1)
            d = dist0 + rowi - coli
            mult = jnp.zeros((t_new, ncols), jnp.int32)
            for win, dil in BRANCHES:
                mult = mult + jnp.where((d >= 0) & (d <= win) & ((d & (dil - 1)) == 0), 1, 0)
            logm = jnp.where(mult == 3, math.log(3.0), jnp.where(mult == 2, math.log(2.0), 0.0))
            for h in range(H_A):
                ref[h] = jnp.where(mult > 0, logm - slope_ref[h] * d.astype(F32), NEG_BIG)

        table(bias_c, w_buf, w_buf)
        table(bias_n, t_new, 0)

    outs = []
    for h in range(H_A):
        cols = slice(h * DH_A, (h + 1) * DH_A)
        q = q_ref[:, cols].astype(BF16)
        s_c = jnp.dot(q, kt_ref[h].astype(BF16), preferred_element_type=F32) + bias_c[h]
        s_n = lax.dot_general(q, kn_ref[:, cols].astype(BF16), nt, preferred_element_type=F32) + bias_n[h]
        mx = jnp.maximum(jnp.max(s_c, axis=1, keepdims=True), jnp.max(s_n, axis=1, keepdims=True))
        p_c = jnp.exp(s_c - mx)
        p_n = jnp.exp(s_n - mx)
        den = jnp.sum(p_c, axis=1, keepdims=True) + jnp.sum(p_n, axis=1, keepdims=True)
        o = lax.dot_general(p_c.astype(BF16), vt_ref[h].astype(BF16), nt, preferred_element_type=F32)
        o = o + jnp.dot(p_n.astype(BF16), vn_ref[:, cols].astype(BF16), preferred_element_type=F32)
        outs.append(o / den)
    o_ref[...] = jnp.concatenate(outs, axis=1)


def _attn_sample(q, kn, vn, cache_k, cache_v, layer, slopes):
    b, t_new, _ = q.shape
    w_buf = cache_k.shape[2]
    assert w_buf >= WINDOW and t_new % SUBLANES == 0
    kt = jnp.transpose(cache_k, (0, 1, 3, 4, 2))
    vt = jnp.transpose(cache_v, (0, 1, 3, 4, 2))
    new_blk = pl.BlockSpec((None, t_new, D_A), lambda bb: (bb, 0, 0))
    cache_blk = pl.BlockSpec((None, None, H_A, DH_A, w_buf), lambda bb: (layer, bb, 0, 0, 0))
    body = functools.partial(_attn_sample_body, t_new=t_new, w_buf=w_buf)
    return pl.pallas_call(
        body,
        grid=(b,),
        in_specs=[pl.BlockSpec(memory_space=pltpu.SMEM), new_blk, new_blk, new_blk, cache_blk, cache_blk],
        out_specs=new_blk,
        out_shape=jax.ShapeDtypeStruct((b, t_new, D_A), F32),
        scratch_shapes=[pltpu.VMEM((H_A, t_new, w_buf), F32), pltpu.VMEM((H_A, t_new, t_new), F32)],
        compiler_params=_cparams("arbitrary"),
        name="attn_sample",
    )(slopes, q, kn, vn, kt, vt)


S5_UT = D_B // LANES
S5_TW = S5_N // S5_UT


def _gelu_tanh(x):
    c = math.sqrt(2.0 / math.pi)
    return 0.5 * x * (1.0 + jnp.tanh(c * (x + 0.044715 * (x * x * x))))


def _s5_body(u_ref, h0_ref, perm_ref, permt_ref, wb_ref, lam_ref, wc_ref, dskip_ref, wglu_ref, bglu_ref,
             o_ref, hl_ref, bu_scr, up_scr, st_scr, *, nb, tsub, nsub):
    step = pl.program_id(0)
    rsub = nb * tsub
    tstep = tsub * nsub

    @pl.when(step == 0)
    def _():
        st_scr[...] = h0_ref[...]

    for sb in range(nsub):
        u_nat = u_ref[:, sb * tsub:(sb + 1) * tsub, :].reshape(rsub, D_B)
        parts = [jnp.dot(perm_ref[...], p, preferred_element_type=F32) for p in _split3(u_nat)]
        up_scr[sb * rsub:(sb + 1) * rsub, :] = parts[0] + parts[1] + parts[2]
    for j in range(S5_UT):
        uj = up_scr[:, j * LANES:(j + 1) * LANES].astype(BF16)
        r = jnp.dot(uj, wb_ref[j], preferred_element_type=F32)
        bu_scr[:, j * S5_TW:(j + 1) * S5_TW] = r[:, :S5_TW]
        bu_scr[:, S5_N + j * S5_TW:S5_N + (j + 1) * S5_TW] = r[:, S5_TW:]

    tiles_per_pass = 4
    for pg in range(S5_N // LANES // tiles_per_pass):
        cols = [(pg * tiles_per_pass + i) * LANES for i in range(tiles_per_pass)]
        lr = [jnp.broadcast_to(lam_ref[0:1, c:c + LANES], (nb, LANES)) for c in cols]
        li = [jnp.broadcast_to(lam_ref[1:2, c:c + LANES], (nb, LANES)) for c in cols]
        init = tuple(st_scr[:, c:c + LANES] for c in cols) + tuple(st_scr[:, S5_N + c:S5_N + c + LANES] for c in cols)

        def tick(t, carry, cols=cols, lr=lr, li=li):
            row = pl.ds(t * nb, nb)
            hr, hi = carry[:tiles_per_pass], carry[tiles_per_pass:]
            nr, ni = [], []
            for i, c in enumerate(cols):
                br = bu_scr[row, c:c + LANES]
                bi = bu_scr[row, S5_N + c:S5_N + c + LANES]
                r_new = lr[i] * hr[i] - li[i] * hi[i] + br
                i_new = lr[i] * hi[i] + li[i] * hr[i] + bi
                bu_scr[row, c:c + LANES] = r_new
                bu_scr[row, S5_N + c:S5_N + c + LANES] = i_new
                nr.append(r_new)
                ni.append(i_new)
            return tuple(nr) + tuple(ni)

        fin = init
        for t in range(tstep):
            fin = tick(t, fin)
        for i, c in enumerate(cols):
            st_scr[:, c:c + LANES] = fin[i]
            st_scr[:, S5_N + c:S5_N + c + LANES] = fin[tiles_per_pass + i]

    hl_ref[...] = st_scr[...]

    ys = []
    for j in range(S5_UT):
        hre = bu_scr[:, j * S5_TW:(j + 1) * S5_TW].astype(BF16)
        him = bu_scr[:, S5_N + j * S5_TW:S5_N + (j + 1) * S5_TW].astype(BF16)
        y = jnp.dot(hre, wc_ref[j, :S5_TW, :], preferred_element_type=F32)
        y = y + jnp.dot(him, wc_ref[j, S5_TW:, :], preferred_element_type=F32)
        ys.append(y)
    y = jnp.concatenate(ys, axis=1) + dskip_ref[...] * up_scr[...]
    g = _gelu_tanh(y)
    gate = jnp.dot(g.astype(BF16), wglu_ref[...], preferred_element_type=F32) + bglu_ref[...]
    out = (g * jax.nn.sigmoid(gate)).astype(BF16)
    for sb in range(nsub):
        o_nat = jnp.dot(permt_ref[...], out[sb * rsub:(sb + 1) * rsub, :], preferred_element_type=F32)
        o_ref[:, sb * tsub:(sb + 1) * tsub, :] = o_nat.reshape(nb, tsub, D_B).astype(o_ref.dtype)


def _s5_mixer(u, h0, prm, tsub, nsub):
    nb, seq, _ = u.shape
    tstep = tsub * nsub
    rsub = nb * tsub
    rows = nb * tstep
    assert seq % tstep == 0 and nb % SUBLANES == 0 and tsub % SUBLANES == 0
    perm = np.zeros((rsub, rsub), np.float32)
    for t in range(tsub):
        for bb in range(nb):
            perm[t * nb + bb, bb * tsub + t] = 1.0
    body = functools.partial(_s5_body, nb=nb, tsub=tsub, nsub=nsub)
    return pl.pallas_call(
        body,
        grid=(seq // tstep,),
        in_specs=[pl.BlockSpec((nb, tstep, D_B), lambda i: (0, i, 0)),
                  _const_spec((nb, 2 * S5_N)),
                  _const_spec((rsub, rsub)), _const_spec((rsub, rsub)),
                  _const_spec((S5_UT, LANES, 2 * S5_TW)),
                  _const_spec((2, S5_N)),
                  _const_spec((S5_UT, 2 * S5_TW, LANES)),
                  _const_spec((1, D_B)),
                  _const_spec((D_B, D_B)),
                  _const_spec((1, D_B))],
        out_specs=[pl.BlockSpec((nb, tstep, D_B), lambda i: (0, i, 0)),
                   pl.BlockSpec((nb, 2 * S5_N), lambda i: (0, 0))],
        out_shape=[jax.ShapeDtypeStruct((nb, seq, D_B), _act_dtype(tsub)),
                   jax.ShapeDtypeStruct((nb, 2 * S5_N), F32)],
        scratch_shapes=[pltpu.VMEM((rows, 2 * S5_N), F32),
                        pltpu.VMEM((rows, D_B), F32),
                        pltpu.VMEM((nb, 2 * S5_N), F32)],
        compiler_params=_cparams("arbitrary"),
        name="s5_mixer",
    )(u, h0, jnp.asarray(perm, BF16), jnp.asarray(perm.T, BF16), prm["wb"], prm["lam"], prm["wc"],
      prm["dskip"], prm["wglu"], prm["bglu"])


def _s5_params(lam_re, lam_im, log_dt, b_re, b_im, c_re, c_im, d_skip, w_glu, b_glu):
    lam = lax.complex(lam_re.astype(F32), lam_im.astype(F32))
    dt = jnp.exp(log_dt.astype(F32))[:, None]
    lam_bar = jnp.exp(lam * dt)
    b_bar = ((lam_bar - 1.0) / lam)[..., None] * lax.complex(b_re.astype(F32), b_im.astype(F32))
    gpt = LANES // CH_B
    eye = jnp.eye(gpt, dtype=F32)

    def b_blocks(part):
        x = part.reshape(S5_UT, gpt, P_B, CH_B)
        x = jnp.einsum("jgpc,gh->jgchp", x, eye)
        return x.reshape(S5_UT, gpt * CH_B, gpt * P_B)

    def c_blocks(part):
        x = part.reshape(S5_UT, gpt, CH_B, P_B)
        x = jnp.einsum("jgcp,gh->jgphc", x, eye)
        return x.reshape(S5_UT, gpt * P_B, gpt * CH_B)

    wb = jnp.concatenate([b_blocks(b_bar.real), b_blocks(b_bar.imag)], axis=2).astype(BF16)
    wc = jnp.concatenate([c_blocks(c_re.astype(F32)), -c_blocks(c_im.astype(F32))], axis=1).astype(BF16)
    lam2 = jnp.stack([lam_bar.real.reshape(S5_N), lam_bar.imag.reshape(S5_N)])
    return dict(wb=wb, wc=wc, lam=lam2, dskip=d_skip.astype(F32).reshape(1, D_B),
                wglu=w_glu.astype(BF16), bglu=b_glu.astype(F32).reshape(1, D_B))


HIST = SUBLANES
SSD_SHORT = 16
SSD_CHUNKS_PER_STEP = 4
XB0 = D_INNER
XC0 = D_INNER + G_C * N_C
GW = D_INNER // G_C


def _expand_heads(a, e_ref):
    e = e_ref[...]
    a1, a2, _ = _split3(a)
    return jnp.dot(a1, e, preferred_element_type=F32) + jnp.dot(a2, e, preferred_element_type=F32)


def _ssd_body(z_ref, xbc_ref, dt_ref, ssm0_ref, a_ref, dexp_ref, gw_ref, e_ref, *rest, lc, activated, nsub):
    if activated:
        y_ref, hl_ref, state = rest[-3:]
    else:
        conv0_ref, convw_ref, convb_ref, dtb_ref = rest[:4]
        y_ref, convst_ref, hl_ref, xpad, state = rest[-5:]
    ch = pl.program_id(1)
    cl = SSD_CHUNK if lc > SSD_SHORT else SSD_SHORT

    @pl.when(ch == 0)
    def _():
        state[...] = ssm0_ref[...]

    if activated:
        assert lc == nsub * cl
    else:
        @pl.when(ch == 0)
        def _():
            xpad[0:HIST, :] = conv0_ref[...]

        if lc < cl:
            xpad[HIST:HIST + cl, :] = jnp.zeros((cl, CONV_DIM), F32)
        xpad[HIST:HIST + lc, :] = xbc_ref[...]
        conv = convb_ref[...] + convw_ref[CONV_K - 1:CONV_K, :] * xpad[HIST:HIST + cl, :]
        for kk in range(CONV_K - 1):
            sh = CONV_K - 1 - kk
            conv = conv + convw_ref[kk:kk + 1, :] * xpad[HIST - sh:HIST - sh + cl, :]
        xs = conv * jax.nn.sigmoid(conv)
        tail = xpad[lc:lc + HIST, :]
        xpad[0:HIST, :] = tail
        convst_ref[...] = tail

        dt = jax.nn.softplus(dt_ref[...] + dtb_ref[...])
        if lc < cl:
            dt = jnp.concatenate([dt, jnp.zeros((cl - lc, LANES), F32)], axis=0)
    def chunk(xs, dt, zact):
        x = xs[:, :D_INNER]
        bgs, cbs, y_offs = [], [], []
        for g in range(G_C):
            bg = xs[:, XB0 + g * N_C:XB0 + (g + 1) * N_C].astype(BF16)
            cg = xs[:, XC0 + g * N_C:XC0 + (g + 1) * N_C].astype(BF16)
            prev = state[g * GW:(g + 1) * GW, :]
            bgs.append(bg)
            cbs.append(lax.dot_general(cg, bg, (((1,), (1,)), ((), ())), preferred_element_type=F32))
            y_offs.append(lax.dot_general(cg, prev.astype(BF16), (((1,), (1,)), ((), ())),
                                          preferred_element_type=F32))

        da = dt * a_ref[...]
        ri = lax.broadcasted_iota(jnp.int32, (cl, cl), 0)
        ci = lax.broadcasted_iota(jnp.int32, (cl, cl), 1)
        tri = ri >= ci
        tri01 = jnp.where(tri, 1.0, 0.0).astype(BF16)
        cs = [jnp.dot(tri01, p, preferred_element_type=F32) for p in _split3(da)]
        a_cs = cs[0] + cs[1] + cs[2]
        a_cs_t = a_cs.T
        dt_t = dt.T
        e_cs = jnp.exp(a_cs)
        tot = a_cs[cl - 1:cl, :]
        w_end = jnp.exp(tot - a_cs) * dt
        both = _expand_heads(jnp.concatenate([e_cs, w_end], axis=0), e_ref)
        scale_x, wend_x = both[:cl], both[cl:]

        xw = (x * wend_x).astype(BF16)
        lane = lax.broadcasted_iota(jnp.int32, (cl, LANES), 1)
        head0 = lane < P_C
        y_parts = []
        for g in range(G_C):
            bg, cb, y_off = bgs[g], cbs[g], y_offs[g]
            for jp in range(J_C // 2):
                ms = []
                for hh in range(2):
                    h = g * J_C + 2 * jp + hh
                    seg = a_cs[:, h:h + 1] - a_cs_t[h:h + 1, :]
                    dec = jnp.exp(jnp.where(tri, seg, NEG_BIG))
                    ms.append((cb * dec * dt_t[h:h + 1, :]).astype(BF16))
                c0 = g * GW + jp * LANES
                xp = x[:, c0:c0 + LANES].astype(BF16)
                r = jnp.dot(jnp.concatenate(ms, axis=0), xp, preferred_element_type=F32)
                yd = jnp.where(head0, r[:cl], r[cl:])
                y_parts.append(yd + y_off[:, jp * LANES:(jp + 1) * LANES] * scale_x[:, c0:c0 + LANES])
            new = lax.dot_general(xw[:, g * GW:(g + 1) * GW], bg, (((0,), (0,)), ((), ())),
                                  preferred_element_type=F32)
            for j in range(J_C):
                h = g * J_C + j
                r0 = g * GW + j * P_C
                state[r0:r0 + P_C, :] = state[r0:r0 + P_C, :] * jnp.exp(tot[:, h:h + 1]) + new[j * P_C:(j + 1) * P_C, :]

        y = jnp.concatenate(y_parts, axis=1) + dexp_ref[...] * x
        y = y[:lc] if lc < cl else y
        gt = y * zact
        outs = []
        for g in range(G_C):
            gg = gt[:, g * GW:(g + 1) * GW]
            ms = jnp.mean(gg * gg, axis=-1, keepdims=True)
            outs.append(gg * lax.rsqrt(ms + RMS_EPS))
        return jnp.concatenate(outs, axis=1) * gw_ref[...]

    if activated:
        for sc in range(nsub):
            rows = slice(sc * cl, (sc + 1) * cl)
            y_ref[rows, :] = chunk(xbc_ref[rows, :], dt_ref[rows, :], z_ref[rows, :]).astype(y_ref.dtype)
    else:
        z = z_ref[...]
        y_ref[...] = chunk(xs, dt, z * jax.nn.sigmoid(z)).astype(y_ref.dtype)
    hl_ref[...] = state[...]


def _ssd_mixer(z, xbc, dtr, conv0, ssm0, ssm_layer, prm, stack=None, layer=0, n_layers=1, activated=False):
    b, seq, _ = z.shape
    lc = SSD_CHUNK if seq % SSD_CHUNK == 0 else seq
    assert lc % SUBLANES == 0 and CONV_K - 1 <= lc <= SSD_CHUNK
    nsub = 1
    if activated:
        nsub = SSD_CHUNKS_PER_STEP if seq % (SSD_CHUNKS_PER_STEP * SSD_CHUNK) == 0 else 1
        lc = nsub * SSD_CHUNK
    nch = seq // lc
    body = functools.partial(_ssd_body, lc=lc, activated=activated, nsub=nsub)
    keep = (lambda items: [it for i, it in enumerate(items) if i != 1]) if activated else (lambda items: items)
    conv_args = [] if activated else [conv0, prm["convw"], prm["convb"], prm["dtb"]]

    def row_blk(n):
        return pl.BlockSpec((None, lc, n), lambda bb, c: (bb, c, 0))

    def per_b(r, n):
        return pl.BlockSpec((None, r, n), lambda bb, c: (bb, 0, 0))

    conv_specs = [] if activated else [per_b(HIST, CONV_DIM), _const_spec((CONV_K, CONV_DIM)),
                                       _const_spec((1, CONV_DIM)), _const_spec((1, LANES))]

    return pl.pallas_call(
        body,
        grid=(b, nch),
        in_specs=[row_blk(D_INNER), row_blk(CONV_DIM), row_blk(LANES),
                  pl.BlockSpec((None, None, D_INNER, N_C), lambda bb, c: (ssm_layer, bb, 0, 0)),
                  _const_spec((1, LANES)), _const_spec((1, D_INNER)), _const_spec((1, D_INNER)),
                  _const_spec((LANES, D_INNER))] + conv_specs
        + ([] if stack is None else [pl.BlockSpec(memory_space=pl.ANY)]),
        out_specs=keep([row_blk(D_INNER), per_b(HIST, CONV_DIM),
                        pl.BlockSpec((None, D_INNER, N_C), lambda bb, c: (layer * b + bb, 0, 0))]),
        out_shape=keep([jax.ShapeDtypeStruct((b, seq, D_INNER), _act_dtype(lc)),
                        jax.ShapeDtypeStruct((b, HIST, CONV_DIM), F32),
                        jax.ShapeDtypeStruct((n_layers * b, D_INNER, N_C), F32)]),
        scratch_shapes=([] if activated else [pltpu.VMEM((HIST + SSD_CHUNK, CONV_DIM), F32)])
        + [pltpu.VMEM((D_INNER, N_C), F32)],
        input_output_aliases={} if stack is None else {8 + len(conv_args): 1 if activated else 2},
        compiler_params=_cparams("parallel", "arbitrary"),
        name="ssd_mixer",
    )(z, xbc, dtr, ssm0, prm["a"], prm["dexp"], prm["gw"], prm["expand"], *conv_args,
      *(() if stack is None else (stack,)))


def _ssd_params(conv_w, conv_b, dt_bias, a_log, d_skip, gnorm_w):
    pad = LANES - H_C
    expand = np.zeros((LANES, D_INNER), np.float32)
    for h in range(H_C):
        expand[h, h * P_C:(h + 1) * P_C] = 1.0
    return dict(
        convw=conv_w.astype(F32), convb=conv_b.astype(F32).reshape(1, CONV_DIM),
        dtb=jnp.pad(dt_bias.astype(F32), (0, pad)).reshape(1, LANES),
        a=jnp.pad(-jnp.exp(a_log.astype(F32)), (0, pad)).reshape(1, LANES),
        dexp=jnp.repeat(d_skip.astype(F32), P_C).reshape(1, D_INNER),
        gw=gnorm_w.astype(F32).reshape(1, D_INNER),
        expand=jnp.asarray(expand, BF16))


def _trunk(x, p, slopes, caches, tm):
    b, seq, _ = x.shape
    m = b * seq
    x2 = x.reshape(m, D_MODEL)
    new_s5, new_conv = [], []
    kv_stacks, ssm_stack = None, None
    kv_transposed = seq % tm == 0
    n_even, n_odd = (DEPTH + 1) // 2, DEPTH // 2
    for i in range(DEPTH):
        j = i // 2
        wn_pre = p["norm_mix_pre"][i].reshape(1, D_MODEL)
        if i % 2 == 0:
            assert seq <= WINDOW
            q, k, v, u, *kv_stacks = _norm_proj(x2, wn_pre, p["w_in_even"], j, (D_A, D_A, D_A, D_B),
                                                (DH_A ** -0.5, 1.0, 1.0, 1.0), tm, head_splits=(1, 2),
                                                stacks=kv_stacks, layer=j, n_layers=n_even,
                                                seq=seq if kv_transposed else None)
            q3, k3, v3 = (t.reshape(b, seq, D_A) for t in (q, k, v))
            if caches is None:
                o_a = _attn_prompt(q3, k3, v3, slopes)
                h0 = jnp.zeros((b, 2 * S5_N), F32)
                tsub, nsub = 16, 8
            else:
                o_a = _attn_sample(q3, k3, v3, caches["k"], caches["v"], j, slopes)
                s0 = caches["s5"][j].astype(F32)
                h0 = jnp.concatenate([s0[..., 0].reshape(b, S5_N), s0[..., 1].reshape(b, S5_N)], axis=1)
                tsub, nsub = seq, 1
            o_b, hl = _s5_mixer(u.reshape(b, seq, D_B), h0, p["s5"][j], tsub, nsub)
            new_s5.append(jnp.stack([hl[:, :S5_N].reshape(b, G_B, P_B), hl[:, S5_N:].reshape(b, G_B, P_B)], axis=-1))
            acts = [o_a.reshape(m, D_A), o_b.reshape(m, D_B)]
            w_out = p["w_out_even"]
        else:
            if caches is None and seq % tm == 0:
                z, xbc, dtr, tails = _norm_proj_ssd(x2, wn_pre, p["w_in_odd"], j, p["ssd"][j], tm, seq)
                ssm0 = jnp.zeros((1, b, D_INNER, N_C), F32)
                yg, ssm_stack = _ssd_mixer(z.reshape(b, seq, D_INNER), xbc.reshape(b, seq, CONV_DIM),
                                           dtr.reshape(b, seq, LANES), None, ssm0, 0, p["ssd"][j],
                                           stack=ssm_stack, layer=j, n_layers=n_odd, activated=True)
                convst = tails.reshape(b, seq // tm, HIST, CONV_DIM)[:, -1]
            else:
                z, xbc, dtr = _norm_proj(x2, wn_pre, p["w_in_odd"], j, (D_INNER, CONV_DIM, LANES),
                                         (1.0, 1.0, 1.0), tm)
                if caches is None:
                    conv0 = jnp.zeros((b, HIST, CONV_DIM), F32)
                    ssm0, ssm_layer = jnp.zeros((1, b, D_INNER, N_C), F32), 0
                else:
                    conv0 = jnp.pad(caches["conv"][j].astype(F32), ((0, 0), (HIST - (CONV_K - 1), 0), (0, 0)))
                    ssm0, ssm_layer = caches["ssm"].astype(F32).reshape(-1, b, D_INNER, N_C), j
                yg, convst, ssm_stack = _ssd_mixer(z.reshape(b, seq, D_INNER), xbc.reshape(b, seq, CONV_DIM),
                                                   dtr.reshape(b, seq, LANES), conv0, ssm0, ssm_layer, p["ssd"][j],
                                                   stack=ssm_stack, layer=j, n_layers=n_odd)
            new_conv.append(convst[:, HIST - (CONV_K - 1):])
            acts = [yg.reshape(m, D_INNER)]
            w_out = p["w_out_odd"]
        x2 = _proj_mlp(acts, w_out, j, x2, p["norm_mix_post"][i].reshape(1, D_MODEL),
                       p["norm_mlp_pre"][i].reshape(1, D_MODEL), p["w_mlp_up"], p["w_mlp_down"], i,
                       p["norm_mlp_post"][i].reshape(1, D_MODEL), tm)
    if kv_transposed:
        k_all, v_all = (jnp.transpose(t, (0, 1, 4, 2, 3)) for t in kv_stacks)
    else:
        k_all, v_all = (t.reshape(n_even, b, seq, H_A, DH_A) for t in kv_stacks)
    return (x2.reshape(b, seq, D_MODEL), k_all, v_all, jnp.stack(new_s5), jnp.stack(new_conv),
            ssm_stack.reshape(n_odd, b, H_C, P_C, N_C))


def kernel(x_prompt, x_sample, cache_k, cache_v, state_s5, state_conv, state_ssm, norm_mix_pre, norm_mix_post, norm_mlp_pre, norm_mlp_post, w_mlp_up, w_mlp_down, w_in_even, w_out_even, s5_lambda_re, s5_lambda_im, s5_log_dt, s5_b_re, s5_b_im, s5_c_re, s5_c_im, s5_d, s5_w_glu, s5_b_glu, w_in_odd, conv_w, conv_b, dt_bias, a_log, d_skip, gnorm_w, w_out_odd):
    n_even, n_odd = w_in_even.shape[0], w_in_odd.shape[0]
    odd_pad = LANES - H_C
    p = dict(
        norm_mix_pre=norm_mix_pre.astype(F32), norm_mix_post=norm_mix_post.astype(F32),
        norm_mlp_pre=norm_mlp_pre.astype(F32), norm_mlp_post=norm_mlp_post.astype(F32),
        w_mlp_up=w_mlp_up.astype(BF16), w_mlp_down=w_mlp_down.astype(BF16),
        w_in_even=w_in_even.astype(BF16), w_out_even=w_out_even.astype(BF16),
        w_in_odd=jnp.pad(w_in_odd, ((0, 0), (0, 0), (0, odd_pad))).astype(BF16),
        w_out_odd=w_out_odd.astype(BF16),
        s5=[_s5_params(s5_lambda_re[j], s5_lambda_im[j], s5_log_dt[j], s5_b_re[j], s5_b_im[j], s5_c_re[j],
                       s5_c_im[j], s5_d[j], s5_w_glu[j], s5_b_glu[j]) for j in range(n_even)],
        ssd=[_ssd_params(conv_w[j], conv_b[j], dt_bias[j], a_log[j], d_skip[j], gnorm_w[j]) for j in range(n_odd)],
    )
    slopes = jnp.asarray(np.power(2.0, -8.0 * np.arange(1, H_A + 1) / H_A), dtype=F32)
    caches = dict(k=cache_k, v=cache_v, s5=state_s5, conv=state_conv, ssm=state_ssm)
    y_p, k_p, v_p, s5_p, conv_p, ssm_p = _trunk(x_prompt, p, slopes, None, tm=512)
    y_s, k_s, v_s, s5_s, conv_s, ssm_s = _trunk(x_sample, p, slopes, caches, tm=256)
    return (y_p, y_s, k_p, v_p, s5_p, conv_p, ssm_p, k_s, v_s, s5_s, conv_s, ssm_s)
```

```python
import functools
import math

import numpy as np
import jax
import jax.numpy as jnp
from jax import lax
from jax.experimental import pallas as pl
from jax.experimental.pallas import tpu as pltpu

F32 = jnp.float32
BF16 = jnp.bfloat16

D_MODEL = 1024
DEPTH = 4
PAST_LEN = 8192
RMS_EPS = 1e-6
D_FF = 4 * D_MODEL
D_A = D_MODEL // 2
DH_A = 64
H_A = D_A // DH_A
BRANCHES = ((128, 1), (512, 4), (2048, 16))
WINDOW = 2048
D_B = D_MODEL - D_A
CH_B = 16
G_B = D_B // CH_B
P_B = 64
S5_N = G_B * P_B
D_INNER = 2 * D_MODEL
P_C = 64
H_C = D_INNER // P_C
G_C = 4
J_C = H_C // G_C
N_C = 128
CONV_K = 4
CONV_DIM = D_INNER + 2 * G_C * N_C
SSD_CHUNK = 128

LANES = 128
SUBLANES = 8
VMEM_LIMIT = 56 * 1024 * 1024
NEG_BIG = -1e30


def _act_dtype(rows):
    return BF16 if rows % (2 * SUBLANES) == 0 else F32


def _cparams(*sem):
    return pltpu.CompilerParams(dimension_semantics=sem, vmem_limit_bytes=VMEM_LIMIT)


def _rms(x, w):
    ms = jnp.mean(x * x, axis=-1, keepdims=True)
    return x * lax.rsqrt(ms + RMS_EPS) * w


def _split3(a):
    a1 = a.astype(BF16)
    r1 = a - a1.astype(F32)
    a2 = r1.astype(BF16)
    a3 = (r1 - a2.astype(F32)).astype(BF16)
    return a1, a2, a3


def _const_spec(shape):
    nd = len(shape)
    return pl.BlockSpec(shape, lambda *_: (0,) * nd, pipeline_mode=pl.Buffered(1))


def _layer_spec(stacked, layer):
    nd = stacked.ndim - 1
    return pl.BlockSpec((None,) + stacked.shape[1:], lambda *_: (layer,) + (0,) * nd,
                        pipeline_mode=pl.Buffered(1))


def _norm_proj_body(*refs, splits, scales, head_splits, n_stacks, tokens_minor):
    x_ref, wn_ref, w_ref = refs[:3]
    out_refs = refs[3 + n_stacks:3 + n_stacks + len(splits)]
    head_refs = refs[3 + n_stacks + len(splits):]
    h = _rms(x_ref[...], wn_ref[...]).astype(BF16)
    off = 0
    for idx, (o_ref, n, sc) in enumerate(zip(out_refs, splits, scales)):
        r = jnp.dot(h, w_ref[:, off:off + n], preferred_element_type=F32)
        if sc != 1.0:
            r = r * sc
        o_ref[...] = r.astype(o_ref.dtype)
        if idx in head_splits:
            h_ref = head_refs[head_splits.index(idx)]
            if tokens_minor:
                h_ref[...] = r.T.reshape(H_A, DH_A, r.shape[0])
            else:
                h_ref[...] = r.reshape(r.shape[0], H_A, DH_A)
        off += n


def _norm_proj(x2, wn, w, w_layer, splits, scales, tm, head_splits=(), stacks=None, layer=0, n_layers=1, seq=None):
    m = x2.shape[0]
    n_total = w.shape[2]
    assert sum(splits) == n_total and m % tm == 0
    nblk = m // tm
    n_stacks = 0 if stacks is None else len(stacks)
    tokens_minor = seq is not None
    body = functools.partial(_norm_proj_body, splits=splits, scales=scales, head_splits=tuple(head_splits),
                             n_stacks=n_stacks, tokens_minor=tokens_minor)
    in_specs = [pl.BlockSpec((tm, D_MODEL), lambda i: (i, 0)),
                _const_spec((1, D_MODEL)),
                _layer_spec(w, w_layer)]
    in_specs += [pl.BlockSpec(memory_space=pl.ANY)] * n_stacks
    out_specs = [pl.BlockSpec((tm, n), lambda i: (i, 0)) for n in splits]
    out_shape = [jax.ShapeDtypeStruct((m, n), F32) for n in splits]
    if tokens_minor:
        assert seq % tm == 0 and tm % LANES == 0
        per_seq = seq // tm
        out_specs += [pl.BlockSpec((None, None, H_A, DH_A, tm),
                                   lambda i: (layer, i // per_seq, 0, 0, i % per_seq)) for _ in head_splits]
        out_shape += [jax.ShapeDtypeStruct((n_layers, m // seq, H_A, DH_A, seq), F32) for _ in head_splits]
    else:
        out_specs += [pl.BlockSpec((tm, H_A, DH_A), lambda i: (layer * nblk + i, 0, 0)) for _ in head_splits]
        out_shape += [jax.ShapeDtypeStruct((n_layers * m, H_A, DH_A), F32) for _ in head_splits]
    aliases = {3 + s: len(splits) + s for s in range(n_stacks)}
    return pl.pallas_call(
        body,
        grid=(nblk,),
        in_specs=in_specs,
        out_specs=out_specs,
        out_shape=out_shape,
        input_output_aliases=aliases,
        compiler_params=_cparams("parallel"),
        name="norm_proj",
    )(x2, wn, w, *(stacks or ()))


CONV_COLS = 512


def _norm_proj_ssd_body(x_ref, wn_ref, w_ref, convw_ref, convb_ref, dtb_ref, zs_ref, xs_ref, dt_ref, tail_ref,
                        carry, xpad, *, tm, per_seq):
    hist = carry.shape[0]

    @pl.when(pl.program_id(0) % per_seq == 0)
    def _():
        carry[...] = jnp.zeros(carry.shape, F32)

    h = _rms(x_ref[...], wn_ref[...]).astype(BF16)
    for c in range(D_INNER // CONV_COLS):
        cols = slice(c * CONV_COLS, (c + 1) * CONV_COLS)
        z = jnp.dot(h, w_ref[:, cols], preferred_element_type=F32)
        zs_ref[:, cols] = z * jax.nn.sigmoid(z)
    for c in range(CONV_DIM // CONV_COLS):
        cols = slice(c * CONV_COLS, (c + 1) * CONV_COLS)
        r = jnp.dot(h, w_ref[:, D_INNER + c * CONV_COLS:D_INNER + (c + 1) * CONV_COLS], preferred_element_type=F32)
        xpad[0:hist, :] = carry[:, cols]
        xpad[hist:, :] = r
        conv = convb_ref[:, cols] + convw_ref[CONV_K - 1:CONV_K, cols] * r
        for kk in range(CONV_K - 1):
            sh = CONV_K - 1 - kk
            conv = conv + convw_ref[kk:kk + 1, cols] * xpad[hist - sh:hist - sh + tm, :]
        xs_ref[:, cols] = conv * jax.nn.sigmoid(conv)
        tail = xpad[tm:, :]
        carry[:, cols] = tail
        tail_ref[:, cols] = tail
    dtr = jnp.dot(h, w_ref[:, D_INNER + CONV_DIM:], preferred_element_type=F32)
    dt_ref[...] = jax.nn.softplus(dtr + dtb_ref[...])


def _norm_proj_ssd(x2, wn, w, w_layer, prm, tm, seq):
    m = x2.shape[0]
    n_total = w.shape[2]
    assert m % tm == 0 and seq % tm == 0 and n_total == D_INNER + CONV_DIM + LANES
    nblk = m // tm
    body = functools.partial(_norm_proj_ssd_body, tm=tm, per_seq=seq // tm)
    return pl.pallas_call(
        body,
        grid=(nblk,),
        in_specs=[pl.BlockSpec((tm, D_MODEL), lambda i: (i, 0)),
                  _const_spec((1, D_MODEL)),
                  _layer_spec(w, w_layer),
                  _const_spec((CONV_K, CONV_DIM)), _const_spec((1, CONV_DIM)), _const_spec((1, LANES))],
        out_specs=[pl.BlockSpec((tm, D_INNER), lambda i: (i, 0)),
                   pl.BlockSpec((tm, CONV_DIM), lambda i: (i, 0)),
                   pl.BlockSpec((tm, LANES), lambda i: (i, 0)),
                   pl.BlockSpec((None, HIST, CONV_DIM), lambda i: (i, 0, 0))],
        out_shape=[jax.ShapeDtypeStruct((m, D_INNER), F32),
                   jax.ShapeDtypeStruct((m, CONV_DIM), F32),
                   jax.ShapeDtypeStruct((m, LANES), F32),
                   jax.ShapeDtypeStruct((nblk, HIST, CONV_DIM), F32)],
        scratch_shapes=[pltpu.VMEM((HIST, CONV_DIM), F32), pltpu.VMEM((HIST + tm, CONV_COLS), F32)],
        compiler_params=_cparams("arbitrary"),
        name="norm_proj_ssd",
    )(x2, wn, w, prm["convw"], prm["convb"], prm["dtb"])


def _proj_mlp_body(*refs, n_in, tf):
    a_refs = refs[:n_in]
    w_ref, x_ref, wpost_ref, wpre_ref, wup_ref, wdown_ref, wpost2_ref, o_ref = refs[n_in:]
    m = None
    row = 0
    for a_ref in a_refs:
        k = a_ref.shape[1]
        t = jnp.dot(a_ref[...].astype(BF16), w_ref[row:row + k, :], preferred_element_type=F32)
        m = t if m is None else m + t
        row += k
    x = x_ref[...] + _rms(m, wpost_ref[...])
    h = _rms(x, wpre_ref[...]).astype(BF16)
    acc = None
    for c in range(D_FF // tf):
        a = jnp.dot(h, wup_ref[:, c * tf:(c + 1) * tf], preferred_element_type=F32)
        a = jnp.maximum(a, 0.0)
        a = (a * a).astype(BF16)
        t = jnp.dot(a, wdown_ref[c * tf:(c + 1) * tf, :], preferred_element_type=F32)
        acc = t if acc is None else acc + t
    o_ref[...] = x + _rms(acc, wpost2_ref[...])


def _proj_mlp(acts, w_out, out_layer, x2, wpost, wpre, wup, wdown, mlp_layer, wpost2, tm, tf=512):
    m = x2.shape[0]
    n_in = len(acts)
    assert sum(a.shape[1] for a in acts) == w_out.shape[1]
    body = functools.partial(_proj_mlp_body, n_in=n_in, tf=tf)
    in_specs = [pl.BlockSpec((tm, a.shape[1]), lambda i: (i, 0)) for a in acts]
    in_specs += [_layer_spec(w_out, out_layer),
                 pl.BlockSpec((tm, D_MODEL), lambda i: (i, 0)),
                 _const_spec((1, D_MODEL)), _const_spec((1, D_MODEL)),
                 _layer_spec(wup, mlp_layer), _layer_spec(wdown, mlp_layer),
                 _const_spec((1, D_MODEL))]
    return pl.pallas_call(
        body,
        grid=(m // tm,),
        in_specs=in_specs,
        out_specs=pl.BlockSpec((tm, D_MODEL), lambda i: (i, 0)),
        out_shape=jax.ShapeDtypeStruct((m, D_MODEL), F32),
        compiler_params=_cparams("parallel"),
        name="proj_mlp",
    )(*acts, w_out, x2, wpost, wpre, wup, wdown, wpost2)


def _attn_prompt_body(slope_ref, q_ref, k_ref, v_ref, o_ref, *scr, seq):
    hp = pl.program_id(1)
    o_scr, m_scr, l_scr, bias_scr = scr[0:3], scr[3:6], scr[6:9], scr[9:12]
    qb = LANES
    n_units = seq // qb
    lane = lax.broadcasted_iota(jnp.int32, (qb, LANES), 1)
    head0 = lane < DH_A

    cfg = []
    for bi, (win, dil) in enumerate(BRANCHES):
        nqb = seq // dil // qb
        nk = 2 * qb if nqb >= 2 else qb
        rowi = lax.broadcasted_iota(jnp.int32, (2 * qb, nk), 0)
        coli = lax.broadcasted_iota(jnp.int32, (2 * qb, nk), 1)
        d = (nk - qb) + (rowi & (qb - 1)) - coli
        slope = jnp.where(rowi < qb, slope_ref[2 * hp], slope_ref[2 * hp + 1])
        bias_scr[bi][...] = jnp.where((d >= 0) & (d <= win // dil), -(slope * dil) * d.astype(F32), NEG_BIG)
        cfg.append((dil, nqb, nk))

    def step(u, carry):
        for bi, (dil, nqb, nk) in enumerate(cfg):
            def rows(start, dil=dil):
                if dil == 1:
                    return pl.ds(pl.multiple_of(start, qb), qb)
                return pl.ds(start, qb, stride=dil)

            r, i = (u, 0) if nqb == 1 else (u // nqb, u % nqb)
            cur = rows(r + dil * qb * i)
            q = q_ref[cur, :]
            qs = jnp.concatenate([jnp.where(head0, q, 0.0), jnp.where(head0, 0.0, q)], axis=0).astype(BF16)
            if nk == qb:
                kk = k_ref[cur, :].astype(BF16)
                vv = v_ref[cur, :].astype(BF16)
            else:
                prev = rows(r + dil * qb * jnp.maximum(i - 1, 0))
                kk = jnp.concatenate([k_ref[prev, :], k_ref[cur, :]], axis=0).astype(BF16)
                vv = jnp.concatenate([v_ref[prev, :], v_ref[cur, :]], axis=0).astype(BF16)
            s = lax.dot_general(qs, kk, (((1,), (1,)), ((), ())), preferred_element_type=F32)
            s = s + bias_scr[bi][...]
            if nk != qb:
                pen = jnp.where(i == 0, NEG_BIG, 0.0)
                s = jnp.concatenate([s[:, :qb] + pen, s[:, qb:]], axis=1)
            mx = jnp.max(s, axis=1, keepdims=True)
            p = jnp.exp(s - mx).astype(BF16)
            va = jnp.concatenate([vv, jnp.ones((nk, LANES), BF16)], axis=1)
            od = jnp.dot(p, va, preferred_element_type=F32)
            o_scr[bi][cur, :] = jnp.where(head0, od[:qb, :LANES], od[qb:, :LANES])
            m_scr[bi][cur, :] = jnp.where(head0, mx[:qb], mx[qb:])
            l_scr[bi][cur, :] = jnp.where(head0, od[:qb, LANES:], od[qb:, LANES:])
        return carry

    lax.fori_loop(0, n_units, step, 0, unroll=16)

    cb = 256

    def combine(c, carry):
        sl = pl.ds(pl.multiple_of(c * cb, cb), cb)
        ms = [m_scr[b][sl, :] for b in range(3)]
        mx = jnp.maximum(jnp.maximum(ms[0], ms[1]), ms[2])
        num = None
        den = None
        for b in range(3):
            w = jnp.exp(ms[b] - mx)
            tn = o_scr[b][sl, :] * w
            td = l_scr[b][sl, :] * w
            num = tn if num is None else num + tn
            den = td if den is None else den + td
        o_ref[sl, :] = (num / den).astype(o_ref.dtype)
        return carry

    lax.fori_loop(0, seq // cb, combine, 0)


def _attn_prompt(q, k, v, slopes):
    b, seq, _ = q.shape
    assert seq % (16 * LANES) == 0
    blk = pl.BlockSpec((None, seq, LANES), lambda bb, hp: (bb, 0, hp))
    body = functools.partial(_attn_prompt_body, seq=seq)
    return pl.pallas_call(
        body,
        grid=(b, D_A // LANES),
        in_specs=[pl.BlockSpec(memory_space=pltpu.SMEM), blk, blk, blk],
        out_specs=blk,
        out_shape=jax.ShapeDtypeStruct((b, seq, D_A), BF16),
        scratch_shapes=[pltpu.VMEM((seq, LANES), F32) for _ in range(9)]
        + [pltpu.VMEM((2 * LANES, 2 * LANES if seq // dil >= 2 * LANES else LANES), F32) for _, dil in BRANCHES],
        compiler_params=_cparams("parallel", "parallel"),
        name="attn_prompt",
    )(slopes, q, k, v)


def _attn_sample_body(slope_ref, q_ref, kn_ref, vn_ref, kt_ref, vt_ref, o_ref, bias_c, bias_n, *, t_new, w_buf):
    nt = (((1,), (1,)), ((), ()))

    @pl.when(pl.program_id(0) == 0)
    def _():
        def table(ref, ncols, dist0):
            rowi = lax.broadcasted_iota(jnp.int32, (t_new, ncols), 0)
            coli = lax.broadcasted_iota(jnp.int32, (t_new, ncols), 1)
            d = dist0 + rowi - coli
            mult = jnp.zeros((t_new, ncols), jnp.int32)
            for win, dil in BRANCHES:
                mult = mult + jnp.where((d >= 0) & (d <= win) & ((d & (dil - 1)) == 0), 1, 0)
            logm = jnp.where(mult == 3, math.log(3.0), jnp.where(mult == 2, math.log(2.0), 0.0))
            for h in range(H_A):
                ref[h] = jnp.where(mult > 0, logm - slope_ref[h] * d.astype(F32), NEG_BIG)

        table(bias_c, w_buf, w_buf)
        table(bias_n, t_new, 0)

    outs = []
    for h in range(H_A):
        cols = slice(h * DH_A, (h + 1) * DH_A)
        q = q_ref[:, cols].astype(BF16)
        s_c = jnp.dot(q, kt_ref[h].astype(BF16), preferred_element_type=F32) + bias_c[h]
        s_n = lax.dot_general(q, kn_ref[:, cols].astype(BF16), nt, preferred_element_type=F32) + bias_n[h]
        mx = jnp.maximum(jnp.max(s_c, axis=1, keepdims=True), jnp.max(s_n, axis=1, keepdims=True))
        p_c = jnp.exp(s_c - mx)
        p_n = jnp.exp(s_n - mx)
        den = jnp.sum(p_c, axis=1, keepdims=True) + jnp.sum(p_n, axis=1, keepdims=True)
        o = lax.dot_general(p_c.astype(BF16), vt_ref[h].astype(BF16), nt, preferred_element_type=F32)
        o = o + jnp.dot(p_n.astype(BF16), vn_ref[:, cols].astype(BF16), preferred_element_type=F32)
        outs.append(o / den)
    o_ref[...] = jnp.concatenate(outs, axis=1)


def _attn_sample(q, kn, vn, cache_k, cache_v, layer, slopes):
    b, t_new, _ = q.shape
    w_buf = cache_k.shape[2]
    assert w_buf >= WINDOW and t_new % SUBLANES == 0
    kt = jnp.transpose(cache_k, (0, 1, 3, 4, 2))
    vt = jnp.transpose(cache_v, (0, 1, 3, 4, 2))
    new_blk = pl.BlockSpec((None, t_new, D_A), lambda bb: (bb, 0, 0))
    cache_blk = pl.BlockSpec((None, None, H_A, DH_A, w_buf), lambda bb: (layer, bb, 0, 0, 0))
    body = functools.partial(_attn_sample_body, t_new=t_new, w_buf=w_buf)
    return pl.pallas_call(
        body,
        grid=(b,),
        in_specs=[pl.BlockSpec(memory_space=pltpu.SMEM), new_blk, new_blk, new_blk, cache_blk, cache_blk],
        out_specs=new_blk,
        out_shape=jax.ShapeDtypeStruct((b, t_new, D_A), F32),
        scratch_shapes=[pltpu.VMEM((H_A, t_new, w_buf), F32), pltpu.VMEM((H_A, t_new, t_new), F32)],
        compiler_params=_cparams("arbitrary"),
        name="attn_sample",
    )(slopes, q, kn, vn, kt, vt)


S5_UT = D_B // LANES
S5_TW = S5_N // S5_UT


def _gelu_tanh(x):
    c = math.sqrt(2.0 / math.pi)
    return 0.5 * x * (1.0 + jnp.tanh(c * (x + 0.044715 * (x * x * x))))


def _s5_body(u_ref, h0_ref, perm_ref, permt_ref, wb_ref, lam_ref, wc_ref, dskip_ref, wglu_ref, bglu_ref,
             o_ref, hl_ref, bu_scr, up_scr, st_scr, *, nb, tsub, nsub):
    step = pl.program_id(0)
    rsub = nb * tsub
    tstep = tsub * nsub

    @pl.when(step == 0)
    def _():
        st_scr[...] = h0_ref[...]

    for sb in range(nsub):
        u_nat = u_ref[:, sb * tsub:(sb + 1) * tsub, :].reshape(rsub, D_B)
        parts = [jnp.dot(perm_ref[...], p, preferred_element_type=F32) for p in _split3(u_nat)]
        up_scr[sb * rsub:(sb + 1) * rsub, :] = parts[0] + parts[1] + parts[2]
    for j in range(S5_UT):
        uj = up_scr[:, j * LANES:(j + 1) * LANES].astype(BF16)
        r = jnp.dot(uj, wb_ref[j], preferred_element_type=F32)
        bu_scr[:, j * S5_TW:(j + 1) * S5_TW] = r[:, :S5_TW]
        bu_scr[:, S5_N + j * S5_TW:S5_N + (j + 1) * S5_TW] = r[:, S5_TW:]

    tiles_per_pass = 4
    for pg in range(S5_N // LANES // tiles_per_pass):
        cols = [(pg * tiles_per_pass + i) * LANES for i in range(tiles_per_pass)]
        lr = [jnp.broadcast_to(lam_ref[0:1, c:c + LANES], (nb, LANES)) for c in cols]
        li = [jnp.broadcast_to(lam_ref[1:2, c:c + LANES], (nb, LANES)) for c in cols]
        init = tuple(st_scr[:, c:c + LANES] for c in cols) + tuple(st_scr[:, S5_N + c:S5_N + c + LANES] for c in cols)

        def tick(t, carry, cols=cols, lr=lr, li=li):
            row = pl.ds(t * nb, nb)
            hr, hi = carry[:tiles_per_pass], carry[tiles_per_pass:]
            nr, ni = [], []
            for i, c in enumerate(cols):
                br = bu_scr[row, c:c + LANES]
                bi = bu_scr[row, S5_N + c:S5_N + c + LANES]
                r_new = lr[i] * hr[i] - li[i] * hi[i] + br
                i_new = lr[i] * hi[i] + li[i] * hr[i] + bi
                bu_scr[row, c:c + LANES] = r_new
                bu_scr[row, S5_N + c:S5_N + c + LANES] = i_new
                nr.append(r_new)
                ni.append(i_new)
            return tuple(nr) + tuple(ni)

        fin = init
        for t in range(tstep):
            fin = tick(t, fin)
        for i, c in enumerate(cols):
            st_scr[:, c:c + LANES] = fin[i]
            st_scr[:, S5_N + c:S5_N + c + LANES] = fin[tiles_per_pass + i]

    hl_ref[...] = st_scr[...]

    ys = []
    for j in range(S5_UT):
        hre = bu_scr[:, j * S5_TW:(j + 1) * S5_TW].astype(BF16)
        him = bu_scr[:, S5_N + j * S5_TW:S5_N + (j + 1) * S5_TW].astype(BF16)
        y = jnp.dot(hre, wc_ref[j, :S5_TW, :], preferred_element_type=F32)
        y = y + jnp.dot(him, wc_ref[j, S5_TW:, :], preferred_element_type=F32)
        ys.append(y)
    y = jnp.concatenate(ys, axis=1) + dskip_ref[...] * up_scr[...]
    g = _gelu_tanh(y)
    gate = jnp.dot(g.astype(BF16), wglu_ref[...], preferred_element_type=F32) + bglu_ref[...]
    out = (g * jax.nn.sigmoid(gate)).astype(BF16)
    for sb in range(nsub):
        o_nat = jnp.dot(permt_ref[...], out[sb * rsub:(sb + 1) * rsub, :], preferred_element_type=F32)
        o_ref[:, sb * tsub:(sb + 1) * tsub, :] = o_nat.reshape(nb, tsub, D_B).astype(o_ref.dtype)


def _s5_mixer(u, h0, prm, tsub, nsub):
    nb, seq, _ = u.shape
    tstep = tsub * nsub
    rsub = nb * tsub
    rows = nb * tstep
    assert seq % tstep == 0 and nb % SUBLANES == 0 and tsub % SUBLANES == 0
    perm = np.zeros((rsub, rsub), np.float32)
    for t in range(tsub):
        for bb in range(nb):
            perm[t * nb + bb, bb * tsub + t] = 1.0
    body = functools.partial(_s5_body, nb=nb, tsub=tsub, nsub=nsub)
    return pl.pallas_call(
        body,
        grid=(seq // tstep,),
        in_specs=[pl.BlockSpec((nb, tstep, D_B), lambda i: (0, i, 0)),
                  _const_spec((nb, 2 * S5_N)),
                  _const_spec((rsub, rsub)), _const_spec((rsub, rsub)),
                  _const_spec((S5_UT, LANES, 2 * S5_TW)),
                  _const_spec((2, S5_N)),
                  _const_spec((S5_UT, 2 * S5_TW, LANES)),
                  _const_spec((1, D_B)),
                  _const_spec((D_B, D_B)),
                  _const_spec((1, D_B))],
        out_specs=[pl.BlockSpec((nb, tstep, D_B), lambda i: (0, i, 0)),
                   pl.BlockSpec((nb, 2 * S5_N), lambda i: (0, 0))],
        out_shape=[jax.ShapeDtypeStruct((nb, seq, D_B), _act_dtype(tsub)),
                   jax.ShapeDtypeStruct((nb, 2 * S5_N), F32)],
        scratch_shapes=[pltpu.VMEM((rows, 2 * S5_N), F32),
                        pltpu.VMEM((rows, D_B), F32),
                        pltpu.VMEM((nb, 2 * S5_N), F32)],
        compiler_params=_cparams("arbitrary"),
        name="s5_mixer",
    )(u, h0, jnp.asarray(perm, BF16), jnp.asarray(perm.T, BF16), prm["wb"], prm["lam"], prm["wc"],
      prm["dskip"], prm["wglu"], prm["bglu"])


def _s5_params(lam_re, lam_im, log_dt, b_re, b_im, c_re, c_im, d_skip, w_glu, b_glu):
    lam = lax.complex(lam_re.astype(F32), lam_im.astype(F32))
    dt = jnp.exp(log_dt.astype(F32))[:, None]
    lam_bar = jnp.exp(lam * dt)
    b_bar = ((lam_bar - 1.0) / lam)[..., None] * lax.complex(b_re.astype(F32), b_im.astype(F32))
    gpt = LANES // CH_B
    eye = jnp.eye(gpt, dtype=F32)

    def b_blocks(part):
        x = part.reshape(S5_UT, gpt, P_B, CH_B)
        x = jnp.einsum("jgpc,gh->jgchp", x, eye)
        return x.reshape(S5_UT, gpt * CH_B, gpt * P_B)

    def c_blocks(part):
        x = part.reshape(S5_UT, gpt, CH_B, P_B)
        x = jnp.einsum("jgcp,gh->jgphc", x, eye)
        return x.reshape(S5_UT, gpt * P_B, gpt * CH_B)

    wb = jnp.concatenate([b_blocks(b_bar.real), b_blocks(b_bar.imag)], axis=2).astype(BF16)
    wc = jnp.concatenate([c_blocks(c_re.astype(F32)), -c_blocks(c_im.astype(F32))], axis=1).astype(BF16)
    lam2 = jnp.stack([lam_bar.real.reshape(S5_N), lam_bar.imag.reshape(S5_N)])
    return dict(wb=wb, wc=wc, lam=lam2, dskip=d_skip.astype(F32).reshape(1, D_B),
                wglu=w_glu.astype(BF16), bglu=b_glu.astype(F32).reshape(1, D_B))


HIST = SUBLANES
SSD_SHORT = 16
SSD_CHUNKS_PER_STEP = 4
XB0 = D_INNER
XC0 = D_INNER + G_C * N_C
GW = D_INNER // G_C


def _expand_heads(a, e_ref):
    e = e_ref[...]
    a1, a2, _ = _split3(a)
    return jnp.dot(a1, e, preferred_element_type=F32) + jnp.dot(a2, e, preferred_element_type=F32)


def _ssd_body(z_ref, xbc_ref, dt_ref, ssm0_ref, a_ref, dexp_ref, gw_ref, e_ref, *rest, lc, activated, nsub):
    if activated:
        y_ref, hl_ref, state = rest[-3:]
    else:
        conv0_ref, convw_ref, convb_ref, dtb_ref = rest[:4]
        y_ref, convst_ref, hl_ref, xpad, state = rest[-5:]
    ch = pl.program_id(1)
    cl = SSD_CHUNK if lc > SSD_SHORT else SSD_SHORT

    @pl.when(ch == 0)
    def _():
        state[...] = ssm0_ref[...]

    if activated:
        assert lc == nsub * cl
    else:
        @pl.when(ch == 0)
        def _():
            xpad[0:HIST, :] = conv0_ref[...]

        if lc < cl:
            xpad[HIST:HIST + cl, :] = jnp.zeros((cl, CONV_DIM), F32)
        xpad[HIST:HIST + lc, :] = xbc_ref[...]
        conv = convb_ref[...] + convw_ref[CONV_K - 1:CONV_K, :] * xpad[HIST:HIST + cl, :]
        for kk in range(CONV_K - 1):
            sh = CONV_K - 1 - kk
            conv = conv + convw_ref[kk:kk + 1, :] * xpad[HIST - sh:HIST - sh + cl, :]
        xs = conv * jax.nn.sigmoid(conv)
        tail = xpad[lc:lc + HIST, :]
        xpad[0:HIST, :] = tail
        convst_ref[...] = tail

        dt = jax.nn.softplus(dt_ref[...] + dtb_ref[...])
        if lc < cl:
            dt = jnp.concatenate([dt, jnp.zeros((cl - lc, LANES), F32)], axis=0)
    def chunk(xs, dt, zact):
        x = xs[:, :D_INNER]
        bgs, cbs, y_offs = [], [], []
        for g in range(G_C):
            bg = xs[:, XB0 + g * N_C:XB0 + (g + 1) * N_C].astype(BF16)
            cg = xs[:, XC0 + g * N_C:XC0 + (g + 1) * N_C].astype(BF16)
            prev = state[g * GW:(g + 1) * GW, :]
            bgs.append(bg)
            cbs.append(lax.dot_general(cg, bg, (((1,), (1,)), ((), ())), preferred_element_type=F32))
            y_offs.append(lax.dot_general(cg, prev.astype(BF16), (((1,), (1,)), ((), ())),
                                          preferred_element_type=F32))

        da = dt * a_ref[...]
        ri = lax.broadcasted_iota(jnp.int32, (cl, cl), 0)
        ci = lax.broadcasted_iota(jnp.int32, (cl, cl), 1)
        tri = ri >= ci
        tri01 = jnp.where(tri, 1.0, 0.0).astype(BF16)
        cs = [jnp.dot(tri01, p, preferred_element_type=F32) for p in _split3(da)]
        a_cs = cs[0] + cs[1] + cs[2]
        a_cs_t = a_cs.T
        dt_t = dt.T
        e_cs = jnp.exp(a_cs)
        tot = a_cs[cl - 1:cl, :]
        w_end = jnp.exp(tot - a_cs) * dt
        both = _expand_heads(jnp.concatenate([e_cs, w_end], axis=0), e_ref)
        scale_x, wend_x = both[:cl], both[cl:]

        xw = (x * wend_x).astype(BF16)
        lane = lax.broadcasted_iota(jnp.int32, (cl, LANES), 1)
        head0 = lane < P_C
        y_parts = []
        for g in range(G_C):
            bg, cb, y_off = bgs[g], cbs[g], y_offs[g]
            for jp in range(J_C // 2):
                ms = []
                for hh in range(2):
                    h = g * J_C + 2 * jp + hh
                    seg = a_cs[:, h:h + 1] - a_cs_t[h:h + 1, :]
                    dec = jnp.exp(jnp.where(tri, seg, NEG_BIG))
                    ms.append((cb * dec * dt_t[h:h + 1, :]).astype(BF16))
                c0 = g * GW + jp * LANES
                xp = x[:, c0:c0 + LANES].astype(BF16)
                r = jnp.dot(jnp.concatenate(ms, axis=0), xp, preferred_element_type=F32)
                yd = jnp.where(head0, r[:cl], r[cl:])
                y_parts.append(yd + y_off[:, jp * LANES:(jp + 1) * LANES] * scale_x[:, c0:c0 + LANES])
            new = lax.dot_general(xw[:, g * GW:(g + 1) * GW], bg, (((0,), (0,)), ((), ())),
                                  preferred_element_type=F32)
            for j in range(J_C):
                h = g * J_C + j
                r0 = g * GW + j * P_C
                state[r0:r0 + P_C, :] = state[r0:r0 + P_C, :] * jnp.exp(tot[:, h:h + 1]) + new[j * P_C:(j + 1) * P_C, :]

        y = jnp.concatenate(y_parts, axis=1) + dexp_ref[...] * x
        y = y[:lc] if lc < cl else y
        gt = y * zact
        outs = []
        for g in range(G_C):
            gg = gt[:, g * GW:(g + 1) * GW]
            ms = jnp.mean(gg * gg, axis=-1, keepdims=True)
            outs.append(gg * lax.rsqrt(ms + RMS_EPS))
        return jnp.concatenate(outs, axis=1) * gw_ref[...]

    if activated:
        for sc in range(nsub):
            rows = slice(sc * cl, (sc + 1) * cl)
            y_ref[rows, :] = chunk(xbc_ref[rows, :], dt_ref[rows, :], z_ref[rows, :]).astype(y_ref.dtype)
    else:
        z = z_ref[...]
        y_ref[...] = chunk(xs, dt, z * jax.nn.sigmoid(z)).astype(y_ref.dtype)
    hl_ref[...] = state[...]


def _ssd_mixer(z, xbc, dtr, conv0, ssm0, ssm_layer, prm, stack=None, layer=0, n_layers=1, activated=False):
    b, seq, _ = z.shape
    lc = SSD_CHUNK if seq % SSD_CHUNK == 0 else seq
    assert lc % SUBLANES == 0 and CONV_K - 1 <= lc <= SSD_CHUNK
    nsub = 1
    if activated:
        nsub = SSD_CHUNKS_PER_STEP if seq % (SSD_CHUNKS_PER_STEP * SSD_CHUNK) == 0 else 1
        lc = nsub * SSD_CHUNK
    nch = seq // lc
    body = functools.partial(_ssd_body, lc=lc, activated=activated, nsub=nsub)
    keep = (lambda items: [it for i, it in enumerate(items) if i != 1]) if activated else (lambda items: items)
    conv_args = [] if activated else [conv0, prm["convw"], prm["convb"], prm["dtb"]]

    def row_blk(n):
        return pl.BlockSpec((None, lc, n), lambda bb, c: (bb, c, 0))

    def per_b(r, n):
        return pl.BlockSpec((None, r, n), lambda bb, c: (bb, 0, 0))

    conv_specs = [] if activated else [per_b(HIST, CONV_DIM), _const_spec((CONV_K, CONV_DIM)),
                                       _const_spec((1, CONV_DIM)), _const_spec((1, LANES))]

    return pl.pallas_call(
        body,
        grid=(b, nch),
        in_specs=[row_blk(D_INNER), row_blk(CONV_DIM), row_blk(LANES),
                  pl.BlockSpec((None, None, D_INNER, N_C), lambda bb, c: (ssm_layer, bb, 0, 0)),
                  _const_spec((1, LANES)), _const_spec((1, D_INNER)), _const_spec((1, D_INNER)),
                  _const_spec((LANES, D_INNER))] + conv_specs
        + ([] if stack is None else [pl.BlockSpec(memory_space=pl.ANY)]),
        out_specs=keep([row_blk(D_INNER), per_b(HIST, CONV_DIM),
                        pl.BlockSpec((None, D_INNER, N_C), lambda bb, c: (layer * b + bb, 0, 0))]),
        out_shape=keep([jax.ShapeDtypeStruct((b, seq, D_INNER), _act_dtype(lc)),
                        jax.ShapeDtypeStruct((b, HIST, CONV_DIM), F32),
                        jax.ShapeDtypeStruct((n_layers * b, D_INNER, N_C), F32)]),
        scratch_shapes=([] if activated else [pltpu.VMEM((HIST + SSD_CHUNK, CONV_DIM), F32)])
        + [pltpu.VMEM((D_INNER, N_C), F32)],
        input_output_aliases={} if stack is None else {8 + len(conv_args): 1 if activated else 2},
        compiler_params=_cparams("parallel", "arbitrary"),
        name="ssd_mixer",
    )(z, xbc, dtr, ssm0, prm["a"], prm["dexp"], prm["gw"], prm["expand"], *conv_args,
      *(() if stack is None else (stack,)))


def _ssd_params(conv_w, conv_b, dt_bias, a_log, d_skip, gnorm_w):
    pad = LANES - H_C
    expand = np.zeros((LANES, D_INNER), np.float32)
    for h in range(H_C):
        expand[h, h * P_C:(h + 1) * P_C] = 1.0
    return dict(
        convw=conv_w.astype(F32), convb=conv_b.astype(F32).reshape(1, CONV_DIM),
        dtb=jnp.pad(dt_bias.astype(F32), (0, pad)).reshape(1, LANES),
        a=jnp.pad(-jnp.exp(a_log.astype(F32)), (0, pad)).reshape(1, LANES),
        dexp=jnp.repeat(d_skip.astype(F32), P_C).reshape(1, D_INNER),
        gw=gnorm_w.astype(F32).reshape(1, D_INNER),
        expand=jnp.asarray(expand, BF16))


def _trunk(x, p, slopes, caches, tm):
    b, seq, _ = x.shape
    m = b * seq
    x2 = x.reshape(m, D_MODEL)
    new_s5, new_conv = [], []
    kv_stacks, ssm_stack = None, None
    kv_transposed = seq % tm == 0
    n_even, n_odd = (DEPTH + 1) // 2, DEPTH // 2
    for i in range(DEPTH):
        j = i // 2
        wn_pre = p["norm_mix_pre"][i].reshape(1, D_MODEL)
        if i % 2 == 0:
            assert seq <= WINDOW
            q, k, v, u, *kv_stacks = _norm_proj(x2, wn_pre, p["w_in_even"], j, (D_A, D_A, D_A, D_B),
                                                (DH_A ** -0.5, 1.0, 1.0, 1.0), tm, head_splits=(1, 2),
                                                stacks=kv_stacks, layer=j, n_layers=n_even,
                                                seq=seq if kv_transposed else None)
            q3, k3, v3 = (t.reshape(b, seq, D_A) for t in (q, k, v))
            if caches is None:
                o_a = _attn_prompt(q3, k3, v3, slopes)
                h0 = jnp.zeros((b, 2 * S5_N), F32)
                tsub, nsub = 16, 8
            else:
                o_a = _attn_sample(q3, k3, v3, caches["k"], caches["v"], j, slopes)
                s0 = caches["s5"][j].astype(F32)
                h0 = jnp.concatenate([s0[..., 0].reshape(b, S5_N), s0[..., 1].reshape(b, S5_N)], axis=1)
                tsub, nsub = seq, 1
            o_b, hl = _s5_mixer(u.reshape(b, seq, D_B), h0, p["s5"][j], tsub, nsub)
            new_s5.append(jnp.stack([hl[:, :S5_N].reshape(b, G_B, P_B), hl[:, S5_N:].reshape(b, G_B, P_B)], axis=-1))
            acts = [o_a.reshape(m, D_A), o_b.reshape(m, D_B)]
            w_out = p["w_out_even"]
        else:
            if caches is None and seq % tm == 0:
                tmo = tm
                z, xbc, dtr, tails = _norm_proj_ssd(x2, wn_pre, p["w_in_odd"], j, p["ssd"][j], tmo, seq)
                ssm0 = jnp.zeros((1, b, D_INNER, N_C), F32)
                yg, ssm_stack = _ssd_mixer(z.reshape(b, seq, D_INNER), xbc.reshape(b, seq, CONV_DIM),
                                           dtr.reshape(b, seq, LANES), None, ssm0, 0, p["ssd"][j],
                                           stack=ssm_stack, layer=j, n_layers=n_odd, activated=True)
                convst = tails.reshape(b, seq // tmo, HIST, CONV_DIM)[:, -1]
            else:
                z, xbc, dtr = _norm_proj(x2, wn_pre, p["w_in_odd"], j, (D_INNER, CONV_DIM, LANES),
                                         (1.0, 1.0, 1.0), tm)
                if caches is None:
                    conv0 = jnp.zeros((b, HIST, CONV_DIM), F32)
                    ssm0, ssm_layer = jnp.zeros((1, b, D_INNER, N_C), F32), 0
                else:
                    conv0 = jnp.pad(caches["conv"][j].astype(F32), ((0, 0), (HIST - (CONV_K - 1), 0), (0, 0)))
                    ssm0, ssm_layer = caches["ssm"].astype(F32).reshape(-1, b, D_INNER, N_C), j
                yg, convst, ssm_stack = _ssd_mixer(z.reshape(b, seq, D_INNER), xbc.reshape(b, seq, CONV_DIM),
                                                   dtr.reshape(b, seq, LANES), conv0, ssm0, ssm_layer, p["ssd"][j],
                                                   stack=ssm_stack, layer=j, n_layers=n_odd)
            new_conv.append(convst[:, HIST - (CONV_K - 1):])
            acts = [yg.reshape(m, D_INNER)]
            w_out = p["w_out_odd"]
        x2 = _proj_mlp(acts, w_out, j, x2, p["norm_mix_post"][i].reshape(1, D_MODEL),
                       p["norm_mlp_pre"][i].reshape(1, D_MODEL), p["w_mlp_up"], p["w_mlp_down"], i,
                       p["norm_mlp_post"][i].reshape(1, D_MODEL), tm)
    if kv_transposed:
        k_all, v_all = (jnp.transpose(t, (0, 1, 4, 2, 3)) for t in kv_stacks)
    else:
        k_all, v_all = (t.reshape(n_even, b, seq, H_A, DH_A) for t in kv_stacks)
    return (x2.reshape(b, seq, D_MODEL), k_all, v_all, jnp.stack(new_s5), jnp.stack(new_conv),
            ssm_stack.reshape(n_odd, b, H_C, P_C, N_C))


def kernel(x_prompt, x_sample, cache_k, cache_v, state_s5, state_conv, state_ssm, norm_mix_pre, norm_mix_post, norm_mlp_pre, norm_mlp_post, w_mlp_up, w_mlp_down, w_in_even, w_out_even, s5_lambda_re, s5_lambda_im, s5_log_dt, s5_b_re, s5_b_im, s5_c_re, s5_c_im, s5_d, s5_w_glu, s5_b_glu, w_in_odd, conv_w, conv_b, dt_bias, a_log, d_skip, gnorm_w, w_out_odd):
    n_even, n_odd = w_in_even.shape[0], w_in_odd.shape[0]
    odd_pad = LANES - H_C
    p = dict(
        norm_mix_pre=norm_mix_pre.astype(F32), norm_mix_post=norm_mix_post.astype(F32),
        norm_mlp_pre=norm_mlp_pre.astype(F32), norm_mlp_post=norm_mlp_post.astype(F32),
        w_mlp_up=w_mlp_up.astype(BF16), w_mlp_down=w_mlp_down.astype(BF16),
        w_in_even=w_in_even.astype(BF16), w_out_even=w_out_even.astype(BF16),
        w_in_odd=jnp.pad(w_in_odd.astype(BF16), ((0, 0), (0, 0), (0, odd_pad))),
        w_out_odd=w_out_odd.astype(BF16),
        s5=[_s5_params(s5_lambda_re[j], s5_lambda_im[j], s5_log_dt[j], s5_b_re[j], s5_b_im[j], s5_c_re[j],
                       s5_c_im[j], s5_d[j], s5_w_glu[j], s5_b_glu[j]) for j in range(n_even)],
        ssd=[_ssd_params(conv_w[j], conv_b[j], dt_bias[j], a_log[j], d_skip[j], gnorm_w[j]) for j in range(n_odd)],
    )
    slopes = jnp.asarray(np.power(2.0, -8.0 * np.arange(1, H_A + 1) / H_A), dtype=F32)
    caches = dict(k=cache_k, v=cache_v, s5=state_s5, conv=state_conv, ssm=state_ssm)
    y_p, k_p, v_p, s5_p, conv_p, ssm_p = _trunk(x_prompt, p, slopes, None, tm=512)
    y_s, k_s, v_s, s5_s, conv_s, ssm_s = _trunk(x_sample, p, slopes, caches, tm=256)
    return (y_p, y_s, k_p, v_p, s5_p, conv_p, ssm_p, k_s, v_s, s5_s, conv_s, ssm_s)
```

```python
import functools
import math

import numpy as np
import jax
import jax.numpy as jnp
from jax import lax
from jax.experimental import pallas as pl
from jax.experimental.pallas import tpu as pltpu

F32 = jnp.float32
BF16 = jnp.bfloat16

D_MODEL = 1024
DEPTH = 4
PAST_LEN = 8192
RMS_EPS = 1e-6
D_FF = 4 * D_MODEL
D_A = D_MODEL // 2
DH_A = 64
H_A = D_A // DH_A
BRANCHES = ((128, 1), (512, 4), (2048, 16))
WINDOW = 2048
D_B = D_MODEL - D_A
CH_B = 16
G_B = D_B // CH_B
P_B = 64
S5_N = G_B * P_B
D_INNER = 2 * D_MODEL
P_C = 64
H_C = D_INNER // P_C
G_C = 4
J_C = H_C // G_C
N_C = 128
CONV_K = 4
CONV_DIM = D_INNER + 2 * G_C * N_C
SSD_CHUNK = 128

LANES = 128
SUBLANES = 8
VMEM_LIMIT = 56 * 1024 * 1024
NEG_BIG = -1e30

TM_LONG = 512
TM_SHORT = 256
S5_TSUB = 2 * SUBLANES
S5_NSUB = 8


def _act_dtype(rows):
    return BF16 if rows % (2 * SUBLANES) == 0 else F32


def _cparams(*sem):
    return pltpu.CompilerParams(dimension_semantics=sem, vmem_limit_bytes=VMEM_LIMIT)


def _rms(x, w):
    ms = jnp.mean(x * x, axis=-1, keepdims=True)
    return x * lax.rsqrt(ms + RMS_EPS) * w


def _split3(a):
    a1 = a.astype(BF16)
    r1 = a - a1.astype(F32)
    a2 = r1.astype(BF16)
    a3 = (r1 - a2.astype(F32)).astype(BF16)
    return a1, a2, a3


def _const_spec(shape):
    nd = len(shape)
    return pl.BlockSpec(shape, lambda *_: (0,) * nd, pipeline_mode=pl.Buffered(1))


def _layer_spec(stacked, layer):
    nd = stacked.ndim - 1
    return pl.BlockSpec((None,) + stacked.shape[1:], lambda *_: (layer,) + (0,) * nd,
                        pipeline_mode=pl.Buffered(1))


def _norm_proj_body(*refs, splits, scales, head_splits, n_stacks, tokens_minor):
    x_ref, wn_ref, w_ref = refs[:3]
    out_refs = refs[3 + n_stacks:3 + n_stacks + len(splits)]
    head_refs = refs[3 + n_stacks + len(splits):]
    h = _rms(x_ref[...], wn_ref[...]).astype(BF16)
    off = 0
    for idx, (o_ref, n, sc) in enumerate(zip(out_refs, splits, scales)):
        r = jnp.dot(h, w_ref[:, off:off + n], preferred_element_type=F32)
        if sc != 1.0:
            r = r * sc
        o_ref[...] = r.astype(o_ref.dtype)
        if idx in head_splits:
            h_ref = head_refs[head_splits.index(idx)]
            if tokens_minor:
                h_ref[...] = r.T.reshape(H_A, DH_A, r.shape[0])
            else:
                h_ref[...] = r.reshape(r.shape[0], H_A, DH_A)
        off += n


def _norm_proj(x2, wn, w, w_layer, splits, scales, tm, head_splits=(), stacks=None, layer=0, n_layers=1, seq=None):
    m = x2.shape[0]
    n_total = w.shape[2]
    assert sum(splits) == n_total and m % tm == 0
    nblk = m // tm
    n_stacks = 0 if stacks is None else len(stacks)
    tokens_minor = seq is not None
    body = functools.partial(_norm_proj_body, splits=splits, scales=scales, head_splits=tuple(head_splits),
                             n_stacks=n_stacks, tokens_minor=tokens_minor)
    in_specs = [pl.BlockSpec((tm, D_MODEL), lambda i: (i, 0)),
                _const_spec((1, D_MODEL)),
                _layer_spec(w, w_layer)]
    in_specs += [pl.BlockSpec(memory_space=pl.ANY)] * n_stacks
    out_specs = [pl.BlockSpec((tm, n), lambda i: (i, 0)) for n in splits]
    out_shape = [jax.ShapeDtypeStruct((m, n), F32) for n in splits]
    if tokens_minor:
        assert seq % tm == 0 and tm % LANES == 0
        per_seq = seq // tm
        out_specs += [pl.BlockSpec((None, None, H_A, DH_A, tm),
                                   lambda i: (layer, i // per_seq, 0, 0, i % per_seq)) for _ in head_splits]
        out_shape += [jax.ShapeDtypeStruct((n_layers, m // seq, H_A, DH_A, seq), F32) for _ in head_splits]
    else:
        out_specs += [pl.BlockSpec((tm, H_A, DH_A), lambda i: (layer * nblk + i, 0, 0)) for _ in head_splits]
        out_shape += [jax.ShapeDtypeStruct((n_layers * m, H_A, DH_A), F32) for _ in head_splits]
    aliases = {3 + s: len(splits) + s for s in range(n_stacks)}
    return pl.pallas_call(
        body,
        grid=(nblk,),
        in_specs=in_specs,
        out_specs=out_specs,
        out_shape=out_shape,
        input_output_aliases=aliases,
        compiler_params=_cparams("parallel"),
        name="norm_proj",
    )(x2, wn, w, *(stacks or ()))


CONV_COLS = 512


def _norm_proj_ssd_body(x_ref, wn_ref, w_ref, convw_ref, convb_ref, dtb_ref, zs_ref, xs_ref, dt_ref, tail_ref,
                        carry, xpad, *, tm, per_seq):
    hist = carry.shape[0]

    @pl.when(pl.program_id(0) % per_seq == 0)
    def _():
        carry[...] = jnp.zeros(carry.shape, F32)

    h = _rms(x_ref[...], wn_ref[...]).astype(BF16)
    for c in range(D_INNER // CONV_COLS):
        cols = slice(c * CONV_COLS, (c + 1) * CONV_COLS)
        z = jnp.dot(h, w_ref[:, cols], preferred_element_type=F32)
        zs_ref[:, cols] = z * jax.nn.sigmoid(z)
    for c in range(CONV_DIM // CONV_COLS):
        cols = slice(c * CONV_COLS, (c + 1) * CONV_COLS)
        r = jnp.dot(h, w_ref[:, D_INNER + c * CONV_COLS:D_INNER + (c + 1) * CONV_COLS], preferred_element_type=F32)
        xpad[0:hist, :] = carry[:, cols]
        xpad[hist:, :] = r
        conv = convb_ref[:, cols] + convw_ref[CONV_K - 1:CONV_K, cols] * r
        for kk in range(CONV_K - 1):
            sh = CONV_K - 1 - kk
            conv = conv + convw_ref[kk:kk + 1, cols] * xpad[hist - sh:hist - sh + tm, :]
        xs_ref[:, cols] = conv * jax.nn.sigmoid(conv)
        tail = xpad[tm:, :]
        carry[:, cols] = tail
        tail_ref[:, cols] = tail
    dtr = jnp.dot(h, w_ref[:, D_INNER + CONV_DIM:], preferred_element_type=F32)
    dt_ref[...] = jax.nn.softplus(dtr + dtb_ref[...])


def _norm_proj_ssd(x2, wn, w, w_layer, prm, tm, seq):
    m = x2.shape[0]
    n_total = w.shape[2]
    assert m % tm == 0 and seq % tm == 0 and n_total == D_INNER + CONV_DIM + LANES
    nblk = m // tm
    body = functools.partial(_norm_proj_ssd_body, tm=tm, per_seq=seq // tm)
    return pl.pallas_call(
        body,
        grid=(nblk,),
        in_specs=[pl.BlockSpec((tm, D_MODEL), lambda i: (i, 0)),
                  _const_spec((1, D_MODEL)),
                  _layer_spec(w, w_layer),
                  _const_spec((CONV_K, CONV_DIM)), _const_spec((1, CONV_DIM)), _const_spec((1, LANES))],
        out_specs=[pl.BlockSpec((tm, D_INNER), lambda i: (i, 0)),
                   pl.BlockSpec((tm, CONV_DIM), lambda i: (i, 0)),
                   pl.BlockSpec((tm, LANES), lambda i: (i, 0)),
                   pl.BlockSpec((None, HIST, CONV_DIM), lambda i: (i, 0, 0))],
        out_shape=[jax.ShapeDtypeStruct((m, D_INNER), F32),
                   jax.ShapeDtypeStruct((m, CONV_DIM), F32),
                   jax.ShapeDtypeStruct((m, LANES), F32),
                   jax.ShapeDtypeStruct((nblk, HIST, CONV_DIM), F32)],
        scratch_shapes=[pltpu.VMEM((HIST, CONV_DIM), F32), pltpu.VMEM((HIST + tm, CONV_COLS), F32)],
        compiler_params=_cparams("arbitrary"),
        name="norm_proj_ssd",
    )(x2, wn, w, prm["convw"], prm["convb"], prm["dtb"])


def _proj_mlp_body(*refs, n_in, tf):
    a_refs = refs[:n_in]
    w_ref, x_ref, wpost_ref, wpre_ref, wup_ref, wdown_ref, wpost2_ref, o_ref = refs[n_in:]
    m = None
    row = 0
    for a_ref in a_refs:
        k = a_ref.shape[1]
        t = jnp.dot(a_ref[...].astype(BF16), w_ref[row:row + k, :], preferred_element_type=F32)
        m = t if m is None else m + t
        row += k
    x = x_ref[...] + _rms(m, wpost_ref[...])
    h = _rms(x, wpre_ref[...]).astype(BF16)
    acc = None
    for c in range(D_FF // tf):
        a = jnp.dot(h, wup_ref[:, c * tf:(c + 1) * tf], preferred_element_type=F32)
        a = jnp.maximum(a, 0.0)
        a = (a * a).astype(BF16)
        t = jnp.dot(a, wdown_ref[c * tf:(c + 1) * tf, :], preferred_element_type=F32)
        acc = t if acc is None else acc + t
    o_ref[...] = x + _rms(acc, wpost2_ref[...])


def _proj_mlp(acts, w_out, out_layer, x2, wpost, wpre, wup, wdown, mlp_layer, wpost2, tm, tf=512):
    m = x2.shape[0]
    n_in = len(acts)
    assert sum(a.shape[1] for a in acts) == w_out.shape[1]
    body = functools.partial(_proj_mlp_body, n_in=n_in, tf=tf)
    in_specs = [pl.BlockSpec((tm, a.shape[1]), lambda i: (i, 0)) for a in acts]
    in_specs += [_layer_spec(w_out, out_layer),
                 pl.BlockSpec((tm, D_MODEL), lambda i: (i, 0)),
                 _const_spec((1, D_MODEL)), _const_spec((1, D_MODEL)),
                 _layer_spec(wup, mlp_layer), _layer_spec(wdown, mlp_layer),
                 _const_spec((1, D_MODEL))]
    return pl.pallas_call(
        body,
        grid=(m // tm,),
        in_specs=in_specs,
        out_specs=pl.BlockSpec((tm, D_MODEL), lambda i: (i, 0)),
        out_shape=jax.ShapeDtypeStruct((m, D_MODEL), F32),
        compiler_params=_cparams("parallel"),
        name="proj_mlp",
    )(*acts, w_out, x2, wpost, wpre, wup, wdown, wpost2)


def _attn_prompt_body(slope_ref, q_ref, k_ref, v_ref, o_ref, *scr, seq):
    hp = pl.program_id(1)
    o_scr, m_scr, l_scr, bias_scr = scr[0:3], scr[3:6], scr[6:9], scr[9:12]
    qb = LANES
    n_units = seq // qb
    lane = lax.broadcasted_iota(jnp.int32, (qb, LANES), 1)
    head0 = lane < DH_A

    cfg = []
    for bi, (win, dil) in enumerate(BRANCHES):
        nqb = seq // dil // qb
        nk = 2 * qb if nqb >= 2 else qb
        rowi = lax.broadcasted_iota(jnp.int32, (2 * qb, nk), 0)
        coli = lax.broadcasted_iota(jnp.int32, (2 * qb, nk), 1)
        d = (nk - qb) + (rowi & (qb - 1)) - coli
        slope = jnp.where(rowi < qb, slope_ref[2 * hp], slope_ref[2 * hp + 1])
        bias_scr[bi][...] = jnp.where((d >= 0) & (d <= win // dil), -(slope * dil) * d.astype(F32), NEG_BIG)
        cfg.append((dil, nqb, nk))

    def step(u, carry):
        for bi, (dil, nqb, nk) in enumerate(cfg):
            def rows(start, dil=dil):
                if dil == 1:
                    return pl.ds(pl.multiple_of(start, qb), qb)
                return pl.ds(start, qb, stride=dil)

            r, i = (u, 0) if nqb == 1 else (u // nqb, u % nqb)
            cur = rows(r + dil * qb * i)
            q = q_ref[cur, :]
            qs = jnp.concatenate([jnp.where(head0, q, 0.0), jnp.where(head0, 0.0, q)], axis=0).astype(BF16)
            if nk == qb:
                kk = k_ref[cur, :].astype(BF16)
                vv = v_ref[cur, :].astype(BF16)
            else:
                prev = rows(r + dil * qb * jnp.maximum(i - 1, 0))
                kk = jnp.concatenate([k_ref[prev, :], k_ref[cur, :]], axis=0).astype(BF16)
                vv = jnp.concatenate([v_ref[prev, :], v_ref[cur, :]], axis=0).astype(BF16)
            s = lax.dot_general(qs, kk, (((1,), (1,)), ((), ())), preferred_element_type=F32)
            s = s + bias_scr[bi][...]
            if nk != qb:
                pen = jnp.where(i == 0, NEG_BIG, 0.0)
                s = jnp.concatenate([s[:, :qb] + pen, s[:, qb:]], axis=1)
            mx = jnp.max(s, axis=1, keepdims=True)
            p = jnp.exp(s - mx).astype(BF16)
            va = jnp.concatenate([vv, jnp.ones((nk, LANES), BF16)], axis=1)
            od = jnp.dot(p, va, preferred_element_type=F32)
            o_scr[bi][cur, :] = jnp.where(head0, od[:qb, :LANES], od[qb:, :LANES])
            m_scr[bi][cur, :] = jnp.where(head0, mx[:qb], mx[qb:])
            l_scr[bi][cur, :] = jnp.where(head0, od[:qb, LANES:], od[qb:, LANES:])
        return carry

    lax.fori_loop(0, n_units, step, 0, unroll=16)

    cb = 256

    def combine(c, carry):
        sl = pl.ds(pl.multiple_of(c * cb, cb), cb)
        ms = [m_scr[b][sl, :] for b in range(3)]
        mx = jnp.maximum(jnp.maximum(ms[0], ms[1]), ms[2])
        num = None
        den = None
        for b in range(3):
            w = jnp.exp(ms[b] - mx)
            tn = o_scr[b][sl, :] * w
            td = l_scr[b][sl, :] * w
            num = tn if num is None else num + tn
            den = td if den is None else den + td
        o_ref[sl, :] = (num / den).astype(o_ref.dtype)
        return carry

    lax.fori_loop(0, seq // cb, combine, 0)


def _attn_prompt(q, k, v, slopes):
    b, seq, _ = q.shape
    assert seq % (16 * LANES) == 0
    blk = pl.BlockSpec((None, seq, LANES), lambda bb, hp: (bb, 0, hp))
    body = functools.partial(_attn_prompt_body, seq=seq)
    return pl.pallas_call(
        body,
        grid=(b, D_A // LANES),
        in_specs=[pl.BlockSpec(memory_space=pltpu.SMEM), blk, blk, blk],
        out_specs=blk,
        out_shape=jax.ShapeDtypeStruct((b, seq, D_A), BF16),
        scratch_shapes=[pltpu.VMEM((seq, LANES), F32) for _ in range(9)]
        + [pltpu.VMEM((2 * LANES, 2 * LANES if seq // dil >= 2 * LANES else LANES), F32) for _, dil in BRANCHES],
        compiler_params=_cparams("parallel", "parallel"),
        name="attn_prompt",
    )(slopes, q, k, v)


def _attn_sample_body(slope_ref, q_ref, kn_ref, vn_ref, kt_ref, vt_ref, o_ref, bias_c, bias_n, *, t_new, w_buf):
    nt = (((1,), (1,)), ((), ()))

    @pl.when(pl.program_id(0) == 0)
    def _():
        def table(ref, ncols, dist0):
            rowi = lax.broadcasted_iota(jnp.int32, (t_new, ncols), 0)
            coli = lax.broadcasted_iota(jnp.int32, (t_new, ncols), 1)
            d = dist0 + rowi - coli
            mult = jnp.zeros((t_new, ncols), jnp.int32)
            for win, dil in BRANCHES:
                mult = mult + jnp.where((d >= 0) & (d <= win) & ((d & (dil - 1)) == 0), 1, 0)
            logm = jnp.where(mult == 3, math.log(3.0), jnp.where(mult == 2, math.log(2.0), 0.0))
            for h in range(H_A):
                ref[h] = jnp.where(mult > 0, logm - slope_ref[h] * d.astype(F32), NEG_BIG)

        table(bias_c, w_buf, w_buf)
        table(bias_n, t_new, 0)

    outs = []
    for h in range(H_A):
        cols = slice(h * DH_A, (h + 1) * DH_A)
        q = q_ref[:, cols].astype(BF16)
        s_c = jnp.dot(q, kt_ref[h].astype(BF16), preferred_element_type=F32) + bias_c[h]
        s_n = lax.dot_general(q, kn_ref[:, cols].astype(BF16), nt, preferred_element_type=F32) + bias_n[h]
        mx = jnp.maximum(jnp.max(s_c, axis=1, keepdims=True), jnp.max(s_n, axis=1, keepdims=True))
        p_c = jnp.exp(s_c - mx)
        p_n = jnp.exp(s_n - mx)
        den = jnp.sum(p_c, axis=1, keepdims=True) + jnp.sum(p_n, axis=1, keepdims=True)
        o = lax.dot_general(p_c.astype(BF16), vt_ref[h].astype(BF16), nt, preferred_element_type=F32)
        o = o + jnp.dot(p_n.astype(BF16), vn_ref[:, cols].astype(BF16), preferred_element_type=F32)
        outs.append(o / den)
    o_ref[...] = jnp.concatenate(outs, axis=1)


def _attn_sample(q, kn, vn, cache_k, cache_v, layer, slopes):
    b, t_new, _ = q.shape
    w_buf = cache_k.shape[2]
    assert w_buf >= WINDOW and t_new % SUBLANES == 0
    kt = jnp.transpose(cache_k, (0, 1, 3, 4, 2))
    vt = jnp.transpose(cache_v, (0, 1, 3, 4, 2))
    new_blk = pl.BlockSpec((None, t_new, D_A), lambda bb: (bb, 0, 0))
    cache_blk = pl.BlockSpec((None, None, H_A, DH_A, w_buf), lambda bb: (layer, bb, 0, 0, 0))
    body = functools.partial(_attn_sample_body, t_new=t_new, w_buf=w_buf)
    return pl.pallas_call(
        body,
        grid=(b,),
        in_specs=[pl.BlockSpec(memory_space=pltpu.SMEM), new_blk, new_blk, new_blk, cache_blk, cache_blk],
        out_specs=new_blk,
        out_shape=jax.ShapeDtypeStruct((b, t_new, D_A), F32),
        scratch_shapes=[pltpu.VMEM((H_A, t_new, w_buf), F32), pltpu.VMEM((H_A, t_new, t_new), F32)],
        compiler_params=_cparams("arbitrary"),
        name="attn_sample",
    )(slopes, q, kn, vn, kt, vt)


S5_UT = D_B // LANES
S5_TW = S5_N // S5_UT


def _gelu_tanh(x):
    c = math.sqrt(2.0 / math.pi)
    return 0.5 * x * (1.0 + jnp.tanh(c * (x + 0.044715 * (x * x * x))))


def _s5_body(u_ref, h0_ref, perm_ref, permt_ref, wb_ref, lam_ref, wc_ref, dskip_ref, wglu_ref, bglu_ref,
             o_ref, hl_ref, bu_scr, up_scr, st_scr, *, nb, tsub, nsub):
    step = pl.program_id(0)
    rsub = nb * tsub
    tstep = tsub * nsub

    @pl.when(step == 0)
    def _():
        st_scr[...] = h0_ref[...]

    for sb in range(nsub):
        u_nat = u_ref[:, sb * tsub:(sb + 1) * tsub, :].reshape(rsub, D_B)
        parts = [jnp.dot(perm_ref[...], p, preferred_element_type=F32) for p in _split3(u_nat)]
        up_scr[sb * rsub:(sb + 1) * rsub, :] = parts[0] + parts[1] + parts[2]
    for j in range(S5_UT):
        uj = up_scr[:, j * LANES:(j + 1) * LANES].astype(BF16)
        r = jnp.dot(uj, wb_ref[j], preferred_element_type=F32)
        bu_scr[:, j * S5_TW:(j + 1) * S5_TW] = r[:, :S5_TW]
        bu_scr[:, S5_N + j * S5_TW:S5_N + (j + 1) * S5_TW] = r[:, S5_TW:]

    tiles_per_pass = 4
    for pg in range(S5_N // LANES // tiles_per_pass):
        cols = [(pg * tiles_per_pass + i) * LANES for i in range(tiles_per_pass)]
        lr = [jnp.broadcast_to(lam_ref[0:1, c:c + LANES], (nb, LANES)) for c in cols]
        li = [jnp.broadcast_to(lam_ref[1:2, c:c + LANES], (nb, LANES)) for c in cols]
        init = tuple(st_scr[:, c:c + LANES] for c in cols) + tuple(st_scr[:, S5_N + c:S5_N + c + LANES] for c in cols)

        def tick(t, carry, cols=cols, lr=lr, li=li):
            row = pl.ds(t * nb, nb)
            hr, hi = carry[:tiles_per_pass], carry[tiles_per_pass:]
            nr, ni = [], []
            for i, c in enumerate(cols):
                br = bu_scr[row, c:c + LANES]
                bi = bu_scr[row, S5_N + c:S5_N + c + LANES]
                r_new = lr[i] * hr[i] - li[i] * hi[i] + br
                i_new = lr[i] * hi[i] + li[i] * hr[i] + bi
                bu_scr[row, c:c + LANES] = r_new
                bu_scr[row, S5_N + c:S5_N + c + LANES] = i_new
                nr.append(r_new)
                ni.append(i_new)
            return tuple(nr) + tuple(ni)

        fin = init
        for t in range(tstep):
            fin = tick(t, fin)
        for i, c in enumerate(cols):
            st_scr[:, c:c + LANES] = fin[i]
            st_scr[:, S5_N + c:S5_N + c + LANES] = fin[tiles_per_pass + i]

    hl_ref[...] = st_scr[...]

    ys = []
    for j in range(S5_UT):
        hre = bu_scr[:, j * S5_TW:(j + 1) * S5_TW].astype(BF16)
        him = bu_scr[:, S5_N + j * S5_TW:S5_N + (j + 1) * S5_TW].astype(BF16)
        y = jnp.dot(hre, wc_ref[j, :S5_TW, :], preferred_element_type=F32)
        y = y + jnp.dot(him, wc_ref[j, S5_TW:, :], preferred_element_type=F32)
        ys.append(y)
    y = jnp.concatenate(ys, axis=1) + dskip_ref[...] * up_scr[...]
    g = _gelu_tanh(y)
    gate = jnp.dot(g.astype(BF16), wglu_ref[...], preferred_element_type=F32) + bglu_ref[...]
    out = (g * jax.nn.sigmoid(gate)).astype(BF16)
    for sb in range(nsub):
        o_nat = jnp.dot(permt_ref[...], out[sb * rsub:(sb + 1) * rsub, :], preferred_element_type=F32)
        o_ref[:, sb * tsub:(sb + 1) * tsub, :] = o_nat.reshape(nb, tsub, D_B).astype(o_ref.dtype)


def _s5_mixer(u, h0, prm, tsub, nsub):
    nb, seq, _ = u.shape
    tstep = tsub * nsub
    rsub = nb * tsub
    rows = nb * tstep
    assert seq % tstep == 0 and nb % SUBLANES == 0 and tsub % SUBLANES == 0
    perm = np.zeros((rsub, rsub), np.float32)
    for t in range(tsub):
        for bb in range(nb):
            perm[t * nb + bb, bb * tsub + t] = 1.0
    body = functools.partial(_s5_body, nb=nb, tsub=tsub, nsub=nsub)
    return pl.pallas_call(
        body,
        grid=(seq // tstep,),
        in_specs=[pl.BlockSpec((nb, tstep, D_B), lambda i: (0, i, 0)),
                  _const_spec((nb, 2 * S5_N)),
                  _const_spec((rsub, rsub)), _const_spec((rsub, rsub)),
                  _const_spec((S5_UT, LANES, 2 * S5_TW)),
                  _const_spec((2, S5_N)),
                  _const_spec((S5_UT, 2 * S5_TW, LANES)),
                  _const_spec((1, D_B)),
                  _const_spec((D_B, D_B)),
                  _const_spec((1, D_B))],
        out_specs=[pl.BlockSpec((nb, tstep, D_B), lambda i: (0, i, 0)),
                   pl.BlockSpec((nb, 2 * S5_N), lambda i: (0, 0))],
        out_shape=[jax.ShapeDtypeStruct((nb, seq, D_B), _act_dtype(tsub)),
                   jax.ShapeDtypeStruct((nb, 2 * S5_N), F32)],
        scratch_shapes=[pltpu.VMEM((rows, 2 * S5_N), F32),
                        pltpu.VMEM((rows, D_B), F32),
                        pltpu.VMEM((nb, 2 * S5_N), F32)],
        compiler_params=_cparams("arbitrary"),
        name="s5_mixer",
    )(u, h0, jnp.asarray(perm, BF16), jnp.asarray(perm.T, BF16), prm["wb"], prm["lam"], prm["wc"],
      prm["dskip"], prm["wglu"], prm["bglu"])


def _s5_params(lam_re, lam_im, log_dt, b_re, b_im, c_re, c_im, d_skip, w_glu, b_glu):
    lam = lax.complex(lam_re.astype(F32), lam_im.astype(F32))
    dt = jnp.exp(log_dt.astype(F32))[:, None]
    lam_bar = jnp.exp(lam * dt)
    b_bar = ((lam_bar - 1.0) / lam)[..., None] * lax.complex(b_re.astype(F32), b_im.astype(F32))
    gpt = LANES // CH_B
    eye = jnp.eye(gpt, dtype=F32)

    def b_blocks(part):
        x = part.reshape(S5_UT, gpt, P_B, CH_B)
        x = jnp.einsum("jgpc,gh->jgchp", x, eye)
        return x.reshape(S5_UT, gpt * CH_B, gpt * P_B)

    def c_blocks(part):
        x = part.reshape(S5_UT, gpt, CH_B, P_B)
        x = jnp.einsum("jgcp,gh->jgphc", x, eye)
        return x.reshape(S5_UT, gpt * P_B, gpt * CH_B)

    wb = jnp.concatenate([b_blocks(b_bar.real), b_blocks(b_bar.imag)], axis=2).astype(BF16)
    wc = jnp.concatenate([c_blocks(c_re.astype(F32)), -c_blocks(c_im.astype(F32))], axis=1).astype(BF16)
    lam2 = jnp.stack([lam_bar.real.reshape(S5_N), lam_bar.imag.reshape(S5_N)])
    return dict(wb=wb, wc=wc, lam=lam2, dskip=d_skip.astype(F32).reshape(1, D_B),
                wglu=w_glu.astype(BF16), bglu=b_glu.astype(F32).reshape(1, D_B))


HIST = SUBLANES
SSD_SHORT = 16
SSD_CHUNKS_PER_STEP = 4
XB0 = D_INNER
XC0 = D_INNER + G_C * N_C
GW = D_INNER // G_C


def _expand_heads(a, e_ref):
    e = e_ref[...]
    a1, a2, _ = _split3(a)
    return jnp.dot(a1, e, preferred_element_type=F32) + jnp.dot(a2, e, preferred_element_type=F32)


def _ssd_body(z_ref, xbc_ref, dt_ref, ssm0_ref, a_ref, dexp_ref, gw_ref, e_ref, *rest, lc, activated, nsub):
    if activated:
        y_ref, hl_ref, state = rest[-3:]
    else:
        conv0_ref, convw_ref, convb_ref, dtb_ref = rest[:4]
        y_ref, convst_ref, hl_ref, xpad, state = rest[-5:]
    ch = pl.program_id(1)
    cl = SSD_CHUNK if lc > SSD_SHORT else SSD_SHORT

    @pl.when(ch == 0)
    def _():
        state[...] = ssm0_ref[...]

    if activated:
        assert lc == nsub * cl
    else:
        @pl.when(ch == 0)
        def _():
            xpad[0:HIST, :] = conv0_ref[...]

        if lc < cl:
            xpad[HIST:HIST + cl, :] = jnp.zeros((cl, CONV_DIM), F32)
        xpad[HIST:HIST + lc, :] = xbc_ref[...]
        conv = convb_ref[...] + convw_ref[CONV_K - 1:CONV_K, :] * xpad[HIST:HIST + cl, :]
        for kk in range(CONV_K - 1):
            sh = CONV_K - 1 - kk
            conv = conv + convw_ref[kk:kk + 1, :] * xpad[HIST - sh:HIST - sh + cl, :]
        xs = conv * jax.nn.sigmoid(conv)
        tail = xpad[lc:lc + HIST, :]
        xpad[0:HIST, :] = tail
        convst_ref[...] = tail

        dt = jax.nn.softplus(dt_ref[...] + dtb_ref[...])
        if lc < cl:
            dt = jnp.concatenate([dt, jnp.zeros((cl - lc, LANES), F32)], axis=0)
    def chunk(xs, dt, zact):
        x = xs[:, :D_INNER]
        bgs, cbs, y_offs = [], [], []
        for g in range(G_C):
            bg = xs[:, XB0 + g * N_C:XB0 + (g + 1) * N_C].astype(BF16)
            cg = xs[:, XC0 + g * N_C:XC0 + (g + 1) * N_C].astype(BF16)
            prev = state[g * GW:(g + 1) * GW, :]
            bgs.append(bg)
            cbs.append(lax.dot_general(cg, bg, (((1,), (1,)), ((), ())), preferred_element_type=F32))
            y_offs.append(lax.dot_general(cg, prev.astype(BF16), (((1,), (1,)), ((), ())),
                                          preferred_element_type=F32))

        da = dt * a_ref[...]
        ri = lax.broadcasted_iota(jnp.int32, (cl, cl), 0)
        ci = lax.broadcasted_iota(jnp.int32, (cl, cl), 1)
        tri = ri >= ci
        tri01 = jnp.where(tri, 1.0, 0.0).astype(BF16)
        cs = [jnp.dot(tri01, p, preferred_element_type=F32) for p in _split3(da)]
        a_cs = cs[0] + cs[1] + cs[2]
        a_cs_t = a_cs.T
        dt_t = dt.T
        e_cs = jnp.exp(a_cs)
        tot = a_cs[cl - 1:cl, :]
        w_end = jnp.exp(tot - a_cs) * dt
        both = _expand_heads(jnp.concatenate([e_cs, w_end], axis=0), e_ref)
        scale_x, wend_x = both[:cl], both[cl:]

        xw = (x * wend_x).astype(BF16)
        lane = lax.broadcasted_iota(jnp.int32, (cl, LANES), 1)
        head0 = lane < P_C
        y_parts = []
        for g in range(G_C):
            bg, cb, y_off = bgs[g], cbs[g], y_offs[g]
            for jp in range(J_C // 2):
                ms = []
                for hh in range(2):
                    h = g * J_C + 2 * jp + hh
                    seg = a_cs[:, h:h + 1] - a_cs_t[h:h + 1, :]
                    dec = jnp.exp(jnp.where(tri, seg, NEG_BIG))
                    ms.append((cb * dec * dt_t[h:h + 1, :]).astype(BF16))
                c0 = g * GW + jp * LANES
                xp = x[:, c0:c0 + LANES].astype(BF16)
                r = jnp.dot(jnp.concatenate(ms, axis=0), xp, preferred_element_type=F32)
                yd = jnp.where(head0, r[:cl], r[cl:])
                y_parts.append(yd + y_off[:, jp * LANES:(jp + 1) * LANES] * scale_x[:, c0:c0 + LANES])
            new = lax.dot_general(xw[:, g * GW:(g + 1) * GW], bg, (((0,), (0,)), ((), ())),
                                  preferred_element_type=F32)
            for j in range(J_C):
                h = g * J_C + j
                r0 = g * GW + j * P_C
                state[r0:r0 + P_C, :] = state[r0:r0 + P_C, :] * jnp.exp(tot[:, h:h + 1]) + new[j * P_C:(j + 1) * P_C, :]

        y = jnp.concatenate(y_parts, axis=1) + dexp_ref[...] * x
        y = y[:lc] if lc < cl else y
        gt = y * zact
        outs = []
        for g in range(G_C):
            gg = gt[:, g * GW:(g + 1) * GW]
            ms = jnp.mean(gg * gg, axis=-1, keepdims=True)
            outs.append(gg * lax.rsqrt(ms + RMS_EPS))
        return jnp.concatenate(outs, axis=1) * gw_ref[...]

    if activated:
        for sc in range(nsub):
            rows = slice(sc * cl, (sc + 1) * cl)
            y_ref[rows, :] = chunk(xbc_ref[rows, :], dt_ref[rows, :], z_ref[rows, :]).astype(y_ref.dtype)
    else:
        z = z_ref[...]
        y_ref[...] = chunk(xs, dt, z * jax.nn.sigmoid(z)).astype(y_ref.dtype)
    hl_ref[...] = state[...]


def _ssd_mixer(z, xbc, dtr, conv0, ssm0, ssm_layer, prm, stack=None, layer=0, n_layers=1, activated=False):
    b, seq, _ = z.shape
    lc = SSD_CHUNK if seq % SSD_CHUNK == 0 else seq
    assert lc % SUBLANES == 0 and CONV_K - 1 <= lc <= SSD_CHUNK
    nsub = 1
    if activated:
        nsub = SSD_CHUNKS_PER_STEP if seq % (SSD_CHUNKS_PER_STEP * SSD_CHUNK) == 0 else 1
        lc = nsub * SSD_CHUNK
    nch = seq // lc
    body = functools.partial(_ssd_body, lc=lc, activated=activated, nsub=nsub)
    keep = (lambda items: [it for i, it in enumerate(items) if i != 1]) if activated else (lambda items: items)
    conv_args = [] if activated else [conv0, prm["convw"], prm["convb"], prm["dtb"]]

    def row_blk(n):
        return pl.BlockSpec((None, lc, n), lambda bb, c: (bb, c, 0))

    def per_b(r, n):
        return pl.BlockSpec((None, r, n), lambda bb, c: (bb, 0, 0))

    conv_specs = [] if activated else [per_b(HIST, CONV_DIM), _const_spec((CONV_K, CONV_DIM)),
                                       _const_spec((1, CONV_DIM)), _const_spec((1, LANES))]

    return pl.pallas_call(
        body,
        grid=(b, nch),
        in_specs=[row_blk(D_INNER), row_blk(CONV_DIM), row_blk(LANES),
                  pl.BlockSpec((None, None, D_INNER, N_C), lambda bb, c: (ssm_layer, bb, 0, 0)),
                  _const_spec((1, LANES)), _const_spec((1, D_INNER)), _const_spec((1, D_INNER)),
                  _const_spec((LANES, D_INNER))] + conv_specs
        + ([] if stack is None else [pl.BlockSpec(memory_space=pl.ANY)]),
        out_specs=keep([row_blk(D_INNER), per_b(HIST, CONV_DIM),
                        pl.BlockSpec((None, D_INNER, N_C), lambda bb, c: (layer * b + bb, 0, 0))]),
        out_shape=keep([jax.ShapeDtypeStruct((b, seq, D_INNER), _act_dtype(lc)),
                        jax.ShapeDtypeStruct((b, HIST, CONV_DIM), F32),
                        jax.ShapeDtypeStruct((n_layers * b, D_INNER, N_C), F32)]),
        scratch_shapes=([] if activated else [pltpu.VMEM((HIST + SSD_CHUNK, CONV_DIM), F32)])
        + [pltpu.VMEM((D_INNER, N_C), F32)],
        input_output_aliases={} if stack is None else {8 + len(conv_args): 1 if activated else 2},
        compiler_params=_cparams("parallel", "arbitrary"),
        name="ssd_mixer",
    )(z, xbc, dtr, ssm0, prm["a"], prm["dexp"], prm["gw"], prm["expand"], *conv_args,
      *(() if stack is None else (stack,)))


def _ssd_params(conv_w, conv_b, dt_bias, a_log, d_skip, gnorm_w):
    pad = LANES - H_C
    expand = np.zeros((LANES, D_INNER), np.float32)
    for h in range(H_C):
        expand[h, h * P_C:(h + 1) * P_C] = 1.0
    return dict(
        convw=conv_w.astype(F32), convb=conv_b.astype(F32).reshape(1, CONV_DIM),
        dtb=jnp.pad(dt_bias.astype(F32), (0, pad)).reshape(1, LANES),
        a=jnp.pad(-jnp.exp(a_log.astype(F32)), (0, pad)).reshape(1, LANES),
        dexp=jnp.repeat(d_skip.astype(F32), P_C).reshape(1, D_INNER),
        gw=gnorm_w.astype(F32).reshape(1, D_INNER),
        expand=jnp.asarray(expand, BF16))


def _trunk(x, p, slopes, caches, tm):
    b, seq, _ = x.shape
    m = b * seq
    x2 = x.reshape(m, D_MODEL)
    new_s5, new_conv = [], []
    kv_stacks, ssm_stack = None, None
    kv_transposed = seq % tm == 0
    n_even, n_odd = (DEPTH + 1) // 2, DEPTH // 2
    for i in range(DEPTH):
        j = i // 2
        wn_pre = p["norm_mix_pre"][i].reshape(1, D_MODEL)
        if i % 2 == 0:
            assert seq <= WINDOW
            q, k, v, u, *kv_stacks = _norm_proj(x2, wn_pre, p["w_in_even"], j, (D_A, D_A, D_A, D_B),
                                                (DH_A ** -0.5, 1.0, 1.0, 1.0), tm, head_splits=(1, 2),
                                                stacks=kv_stacks, layer=j, n_layers=n_even,
                                                seq=seq if kv_transposed else None)
            q3, k3, v3 = (t.reshape(b, seq, D_A) for t in (q, k, v))
            if caches is None:
                o_a = _attn_prompt(q3, k3, v3, slopes)
                h0 = jnp.zeros((b, 2 * S5_N), F32)
                tsub, nsub = S5_TSUB, S5_NSUB
            else:
                o_a = _attn_sample(q3, k3, v3, caches["k"], caches["v"], j, slopes)
                s0 = caches["s5"][j].astype(F32)
                h0 = jnp.concatenate([s0[..., 0].reshape(b, S5_N), s0[..., 1].reshape(b, S5_N)], axis=1)
                tsub, nsub = seq, 1
            o_b, hl = _s5_mixer(u.reshape(b, seq, D_B), h0, p["s5"][j], tsub, nsub)
            new_s5.append(jnp.stack([hl[:, :S5_N].reshape(b, G_B, P_B), hl[:, S5_N:].reshape(b, G_B, P_B)], axis=-1))
            acts = [o_a.reshape(m, D_A), o_b.reshape(m, D_B)]
            w_out = p["w_out_even"]
        else:
            if caches is None and seq % tm == 0:
                tmo = tm
                z, xbc, dtr, tails = _norm_proj_ssd(x2, wn_pre, p["w_in_odd"], j, p["ssd"][j], tmo, seq)
                ssm0 = jnp.zeros((1, b, D_INNER, N_C), F32)
                yg, ssm_stack = _ssd_mixer(z.reshape(b, seq, D_INNER), xbc.reshape(b, seq, CONV_DIM),
                                           dtr.reshape(b, seq, LANES), None, ssm0, 0, p["ssd"][j],
                                           stack=ssm_stack, layer=j, n_layers=n_odd, activated=True)
                convst = tails.reshape(b, seq // tmo, HIST, CONV_DIM)[:, -1]
            else:
                z, xbc, dtr = _norm_proj(x2, wn_pre, p["w_in_odd"], j, (D_INNER, CONV_DIM, LANES),
                                         (1.0, 1.0, 1.0), tm)
                if caches is None:
                    conv0 = jnp.zeros((b, HIST, CONV_DIM), F32)
                    ssm0, ssm_layer = jnp.zeros((1, b, D_INNER, N_C), F32), 0
                else:
                    conv0 = jnp.pad(caches["conv"][j].astype(F32), ((0, 0), (HIST - (CONV_K - 1), 0), (0, 0)))
                    ssm0, ssm_layer = caches["ssm"].astype(F32).reshape(-1, b, D_INNER, N_C), j
                yg, convst, ssm_stack = _ssd_mixer(z.reshape(b, seq, D_INNER), xbc.reshape(b, seq, CONV_DIM),
                                                   dtr.reshape(b, seq, LANES), conv0, ssm0, ssm_layer, p["ssd"][j],
                                                   stack=ssm_stack, layer=j, n_layers=n_odd)
            new_conv.append(convst[:, HIST - (CONV_K - 1):])
            acts = [yg.reshape(m, D_INNER)]
            w_out = p["w_out_odd"]
        x2 = _proj_mlp(acts, w_out, j, x2, p["norm_mix_post"][i].reshape(1, D_MODEL),
                       p["norm_mlp_pre"][i].reshape(1, D_MODEL), p["w_mlp_up"], p["w_mlp_down"], i,
                       p["norm_mlp_post"][i].reshape(1, D_MODEL), tm)
    if kv_transposed:
        k_all, v_all = (jnp.transpose(t, (0, 1, 4, 2, 3)) for t in kv_stacks)
    else:
        k_all, v_all = (t.reshape(n_even, b, seq, H_A, DH_A) for t in kv_stacks)
    return (x2.reshape(b, seq, D_MODEL), k_all, v_all, jnp.stack(new_s5), jnp.stack(new_conv),
            ssm_stack.reshape(n_odd, b, H_C, P_C, N_C))


def kernel(x_prompt, x_sample, cache_k, cache_v, state_s5, state_conv, state_ssm, norm_mix_pre, norm_mix_post, norm_mlp_pre, norm_mlp_post, w_mlp_up, w_mlp_down, w_in_even, w_out_even, s5_lambda_re, s5_lambda_im, s5_log_dt, s5_b_re, s5_b_im, s5_c_re, s5_c_im, s5_d, s5_w_glu, s5_b_glu, w_in_odd, conv_w, conv_b, dt_bias, a_log, d_skip, gnorm_w, w_out_odd):
    n_even, n_odd = w_in_even.shape[0], w_in_odd.shape[0]
    odd_pad = LANES - H_C
    p = dict(
        norm_mix_pre=norm_mix_pre.astype(F32), norm_mix_post=norm_mix_post.astype(F32),
        norm_mlp_pre=norm_mlp_pre.astype(F32), norm_mlp_post=norm_mlp_post.astype(F32),
        w_mlp_up=w_mlp_up.astype(BF16), w_mlp_down=w_mlp_down.astype(BF16),
        w_in_even=w_in_even.astype(BF16), w_out_even=w_out_even.astype(BF16),
        w_in_odd=jnp.pad(w_in_odd.astype(BF16), ((0, 0), (0, 0), (0, odd_pad))),
        w_out_odd=w_out_odd.astype(BF16),
        s5=[_s5_params(s5_lambda_re[j], s5_lambda_im[j], s5_log_dt[j], s5_b_re[j], s5_b_im[j], s5_c_re[j],
                       s5_c_im[j], s5_d[j], s5_w_glu[j], s5_b_glu[j]) for j in range(n_even)],
        ssd=[_ssd_params(conv_w[j], conv_b[j], dt_bias[j], a_log[j], d_skip[j], gnorm_w[j]) for j in range(n_odd)],
    )
    slopes = jnp.asarray(np.power(2.0, -8.0 * np.arange(1, H_A + 1) / H_A), dtype=F32)
    caches = dict(k=cache_k, v=cache_v, s5=state_s5, conv=state_conv, ssm=state_ssm)
    y_p, k_p, v_p, s5_p, conv_p, ssm_p = _trunk(x_prompt, p, slopes, None, tm=TM_LONG)
    y_s, k_s, v_s, s5_s, conv_s, ssm_s = _trunk(x_sample, p, slopes, caches, tm=TM_SHORT)
    return (y_p, y_s, k_p, v_p, s5_p, conv_p, ssm_p, k_s, v_s, s5_s, conv_s, ssm_s)
```

```python
import functools
import math

import numpy as np
import jax
import jax.numpy as jnp
from jax import lax
from jax.experimental import pallas as pl
from jax.experimental.pallas import tpu as pltpu

F32 = jnp.float32
BF16 = jnp.bfloat16

D_MODEL = 1024
DEPTH = 4
PAST_LEN = 8192
RMS_EPS = 1e-6
D_FF = 4 * D_MODEL
D_A = D_MODEL // 2
DH_A = 64
H_A = D_A // DH_A
BRANCHES = ((128, 1), (512, 4), (2048, 16))
WINDOW = 2048
D_B = D_MODEL - D_A
CH_B = 16
G_B = D_B // CH_B
P_B = 64
S5_N = G_B * P_B
D_INNER = 2 * D_MODEL
P_C = 64
H_C = D_INNER // P_C
G_C = 4
J_C = H_C // G_C
N_C = 128
CONV_K = 4
CONV_DIM = D_INNER + 2 * G_C * N_C
SSD_CHUNK = 128

LANES = 128
SUBLANES = 8
VMEM_LIMIT = 56 * 1024 * 1024
NEG_BIG = -1e30

TM_LONG = 512
TM_SHORT = 256
S5_TSUB = 2 * SUBLANES
S5_NSUB = 8


def _act_dtype(rows):
    return BF16 if rows % (2 * SUBLANES) == 0 else F32


def _cparams(*sem):
    return pltpu.CompilerParams(dimension_semantics=sem, vmem_limit_bytes=VMEM_LIMIT)


def _rms(x, w):
    ms = jnp.mean(x * x, axis=-1, keepdims=True)
    return x * lax.rsqrt(ms + RMS_EPS) * w


def _split3(a):
    a1 = a.astype(BF16)
    r1 = a - a1.astype(F32)
    a2 = r1.astype(BF16)
    a3 = (r1 - a2.astype(F32)).astype(BF16)
    return a1, a2, a3


def _const_spec(shape):
    nd = len(shape)
    return pl.BlockSpec(shape, lambda *_: (0,) * nd, pipeline_mode=pl.Buffered(1))


def _layer_spec(stacked, layer):
    nd = stacked.ndim - 1
    return pl.BlockSpec((None,) + stacked.shape[1:], lambda *_: (layer,) + (0,) * nd,
                        pipeline_mode=pl.Buffered(1))


def _norm_proj_body(*refs, splits, scales, head_splits, n_stacks, tokens_minor):
    x_ref, wn_ref, w_ref = refs[:3]
    out_refs = refs[3 + n_stacks:3 + n_stacks + len(splits)]
    head_refs = refs[3 + n_stacks + len(splits):]
    h = _rms(x_ref[...], wn_ref[...]).astype(BF16)
    off = 0
    for idx, (o_ref, n, sc) in enumerate(zip(out_refs, splits, scales)):
        r = jnp.dot(h, w_ref[:, off:off + n].astype(BF16), preferred_element_type=F32)
        if sc != 1.0:
            r = r * sc
        o_ref[...] = r.astype(o_ref.dtype)
        if idx in head_splits:
            h_ref = head_refs[head_splits.index(idx)]
            if tokens_minor:
                h_ref[...] = r.T.reshape(H_A, DH_A, r.shape[0])
            else:
                h_ref[...] = r.reshape(r.shape[0], H_A, DH_A)
        off += n


def _norm_proj(x2, wn, w, w_layer, splits, scales, tm, head_splits=(), stacks=None, layer=0, n_layers=1, seq=None):
    m = x2.shape[0]
    n_total = w.shape[2]
    assert sum(splits) == n_total and m % tm == 0
    nblk = m // tm
    n_stacks = 0 if stacks is None else len(stacks)
    tokens_minor = seq is not None
    body = functools.partial(_norm_proj_body, splits=splits, scales=scales, head_splits=tuple(head_splits),
                             n_stacks=n_stacks, tokens_minor=tokens_minor)
    in_specs = [pl.BlockSpec((tm, D_MODEL), lambda i: (i, 0)),
                _const_spec((1, D_MODEL)),
                _layer_spec(w, w_layer)]
    in_specs += [pl.BlockSpec(memory_space=pl.ANY)] * n_stacks
    out_specs = [pl.BlockSpec((tm, n), lambda i: (i, 0)) for n in splits]
    out_shape = [jax.ShapeDtypeStruct((m, n), F32) for n in splits]
    if tokens_minor:
        assert seq % tm == 0 and tm % LANES == 0
        per_seq = seq // tm
        out_specs += [pl.BlockSpec((None, None, H_A, DH_A, tm),
                                   lambda i: (layer, i // per_seq, 0, 0, i % per_seq)) for _ in head_splits]
        out_shape += [jax.ShapeDtypeStruct((n_layers, m // seq, H_A, DH_A, seq), F32) for _ in head_splits]
    else:
        out_specs += [pl.BlockSpec((tm, H_A, DH_A), lambda i: (layer * nblk + i, 0, 0)) for _ in head_splits]
        out_shape += [jax.ShapeDtypeStruct((n_layers * m, H_A, DH_A), F32) for _ in head_splits]
    aliases = {3 + s: len(splits) + s for s in range(n_stacks)}
    return pl.pallas_call(
        body,
        grid=(nblk,),
        in_specs=in_specs,
        out_specs=out_specs,
        out_shape=out_shape,
        input_output_aliases=aliases,
        compiler_params=_cparams("parallel"),
        name="norm_proj",
    )(x2, wn, w, *(stacks or ()))


CONV_COLS = 512


def _norm_proj_ssd_body(x_ref, wn_ref, w_ref, convw_ref, convb_ref, dtb_ref, zs_ref, xs_ref, dt_ref, tail_ref,
                        carry, xpad, *, tm, per_seq):
    hist = carry.shape[0]

    @pl.when(pl.program_id(0) % per_seq == 0)
    def _():
        carry[...] = jnp.zeros(carry.shape, F32)

    h = _rms(x_ref[...], wn_ref[...]).astype(BF16)
    for c in range(D_INNER // CONV_COLS):
        cols = slice(c * CONV_COLS, (c + 1) * CONV_COLS)
        z = jnp.dot(h, w_ref[:, cols], preferred_element_type=F32)
        zs_ref[:, cols] = z * jax.nn.sigmoid(z)
    for c in range(CONV_DIM // CONV_COLS):
        cols = slice(c * CONV_COLS, (c + 1) * CONV_COLS)
        r = jnp.dot(h, w_ref[:, D_INNER + c * CONV_COLS:D_INNER + (c + 1) * CONV_COLS], preferred_element_type=F32)
        xpad[0:hist, :] = carry[:, cols]
        xpad[hist:, :] = r
        conv = convb_ref[:, cols] + convw_ref[CONV_K - 1:CONV_K, cols] * r
        for kk in range(CONV_K - 1):
            sh = CONV_K - 1 - kk
            conv = conv + convw_ref[kk:kk + 1, cols] * xpad[hist - sh:hist - sh + tm, :]
        xs_ref[:, cols] = conv * jax.nn.sigmoid(conv)
        tail = xpad[tm:, :]
        carry[:, cols] = tail
        tail_ref[:, cols] = tail
    dtr = jnp.dot(h, w_ref[:, D_INNER + CONV_DIM:], preferred_element_type=F32)
    dt_ref[...] = jax.nn.softplus(dtr + dtb_ref[...])


def _norm_proj_ssd(x2, wn, w, w_layer, prm, tm, seq):
    m = x2.shape[0]
    n_total = w.shape[2]
    assert m % tm == 0 and seq % tm == 0 and n_total == D_INNER + CONV_DIM + LANES
    nblk = m // tm
    body = functools.partial(_norm_proj_ssd_body, tm=tm, per_seq=seq // tm)
    return pl.pallas_call(
        body,
        grid=(nblk,),
        in_specs=[pl.BlockSpec((tm, D_MODEL), lambda i: (i, 0)),
                  _const_spec((1, D_MODEL)),
                  _layer_spec(w, w_layer),
                  _const_spec((CONV_K, CONV_DIM)), _const_spec((1, CONV_DIM)), _const_spec((1, LANES))],
        out_specs=[pl.BlockSpec((tm, D_INNER), lambda i: (i, 0)),
                   pl.BlockSpec((tm, CONV_DIM), lambda i: (i, 0)),
                   pl.BlockSpec((tm, LANES), lambda i: (i, 0)),
                   pl.BlockSpec((None, HIST, CONV_DIM), lambda i: (i, 0, 0))],
        out_shape=[jax.ShapeDtypeStruct((m, D_INNER), F32),
                   jax.ShapeDtypeStruct((m, CONV_DIM), F32),
                   jax.ShapeDtypeStruct((m, LANES), F32),
                   jax.ShapeDtypeStruct((nblk, HIST, CONV_DIM), F32)],
        scratch_shapes=[pltpu.VMEM((HIST, CONV_DIM), F32), pltpu.VMEM((HIST + tm, CONV_COLS), F32)],
        compiler_params=_cparams("arbitrary"),
        name="norm_proj_ssd",
    )(x2, wn, w, prm["convw"], prm["convb"], prm["dtb"])


def _proj_mlp_body(*refs, n_in, tf):
    a_refs = refs[:n_in]
    w_ref, x_ref, wpost_ref, wpre_ref, wup_ref, wdown_ref, wpost2_ref, o_ref = refs[n_in:]
    m = None
    row = 0
    for a_ref in a_refs:
        k = a_ref.shape[1]
        t = jnp.dot(a_ref[...].astype(BF16), w_ref[row:row + k, :].astype(BF16), preferred_element_type=F32)
        m = t if m is None else m + t
        row += k
    x = x_ref[...] + _rms(m, wpost_ref[...])
    h = _rms(x, wpre_ref[...]).astype(BF16)
    acc = None
    for c in range(D_FF // tf):
        a = jnp.dot(h, wup_ref[:, c * tf:(c + 1) * tf], preferred_element_type=F32)
        a = jnp.maximum(a, 0.0)
        a = (a * a).astype(BF16)
        t = jnp.dot(a, wdown_ref[c * tf:(c + 1) * tf, :], preferred_element_type=F32)
        acc = t if acc is None else acc + t
    o_ref[...] = x + _rms(acc, wpost2_ref[...])


def _proj_mlp(acts, w_out, out_layer, x2, wpost, wpre, wup, wdown, mlp_layer, wpost2, tm, tf=512):
    m = x2.shape[0]
    n_in = len(acts)
    assert sum(a.shape[1] for a in acts) == w_out.shape[1]
    body = functools.partial(_proj_mlp_body, n_in=n_in, tf=tf)
    in_specs = [pl.BlockSpec((tm, a.shape[1]), lambda i: (i, 0)) for a in acts]
    in_specs += [_layer_spec(w_out, out_layer),
                 pl.BlockSpec((tm, D_MODEL), lambda i: (i, 0)),
                 _const_spec((1, D_MODEL)), _const_spec((1, D_MODEL)),
                 _layer_spec(wup, mlp_layer), _layer_spec(wdown, mlp_layer),
                 _const_spec((1, D_MODEL))]
    return pl.pallas_call(
        body,
        grid=(m // tm,),
        in_specs=in_specs,
        out_specs=pl.BlockSpec((tm, D_MODEL), lambda i: (i, 0)),
        out_shape=jax.ShapeDtypeStruct((m, D_MODEL), F32),
        compiler_params=_cparams("parallel"),
        name="proj_mlp",
    )(*acts, w_out, x2, wpost, wpre, wup, wdown, wpost2)


def _attn_prompt_body(slope_ref, q_ref, k_ref, v_ref, o_ref, *scr, seq):
    hp = pl.program_id(1)
    o_scr, m_scr, l_scr, bias_scr = scr[0:3], scr[3:6], scr[6:9], scr[9:12]
    qb = LANES
    n_units = seq // qb
    lane = lax.broadcasted_iota(jnp.int32, (qb, LANES), 1)
    head0 = lane < DH_A

    cfg = []
    for bi, (win, dil) in enumerate(BRANCHES):
        nqb = seq // dil // qb
        nk = 2 * qb if nqb >= 2 else qb
        rowi = lax.broadcasted_iota(jnp.int32, (2 * qb, nk), 0)
        coli = lax.broadcasted_iota(jnp.int32, (2 * qb, nk), 1)
        d = (nk - qb) + (rowi & (qb - 1)) - coli
        slope = jnp.where(rowi < qb, slope_ref[2 * hp], slope_ref[2 * hp + 1])
        bias_scr[bi][...] = jnp.where((d >= 0) & (d <= win // dil), -(slope * dil) * d.astype(F32), NEG_BIG)
        cfg.append((dil, nqb, nk))

    def step(u, carry):
        for bi, (dil, nqb, nk) in enumerate(cfg):
            def rows(start, dil=dil):
                if dil == 1:
                    return pl.ds(pl.multiple_of(start, qb), qb)
                return pl.ds(start, qb, stride=dil)

            r, i = (u, 0) if nqb == 1 else (u // nqb, u % nqb)
            cur = rows(r + dil * qb * i)
            q = q_ref[cur, :]
            qs = jnp.concatenate([jnp.where(head0, q, 0.0), jnp.where(head0, 0.0, q)], axis=0).astype(BF16)
            if nk == qb:
                kk = k_ref[cur, :].astype(BF16)
                vv = v_ref[cur, :].astype(BF16)
            else:
                prev = rows(r + dil * qb * jnp.maximum(i - 1, 0))
                kk = jnp.concatenate([k_ref[prev, :], k_ref[cur, :]], axis=0).astype(BF16)
                vv = jnp.concatenate([v_ref[prev, :], v_ref[cur, :]], axis=0).astype(BF16)
            s = lax.dot_general(qs, kk, (((1,), (1,)), ((), ())), preferred_element_type=F32)
            s = s + bias_scr[bi][...]
            if nk != qb:
                pen = jnp.where(i == 0, NEG_BIG, 0.0)
                s = jnp.concatenate([s[:, :qb] + pen, s[:, qb:]], axis=1)
            mx = jnp.max(s, axis=1, keepdims=True)
            p = jnp.exp(s - mx).astype(BF16)
            va = jnp.concatenate([vv, jnp.ones((nk, LANES), BF16)], axis=1)
            od = jnp.dot(p, va, preferred_element_type=F32)
            o_scr[bi][cur, :] = jnp.where(head0, od[:qb, :LANES], od[qb:, :LANES])
            m_scr[bi][cur, :] = jnp.where(head0, mx[:qb], mx[qb:])
            l_scr[bi][cur, :] = jnp.where(head0, od[:qb, LANES:], od[qb:, LANES:])
        return carry

    lax.fori_loop(0, n_units, step, 0, unroll=16)

    cb = 256

    def combine(c, carry):
        sl = pl.ds(pl.multiple_of(c * cb, cb), cb)
        ms = [m_scr[b][sl, :] for b in range(3)]
        mx = jnp.maximum(jnp.maximum(ms[0], ms[1]), ms[2])
        num = None
        den = None
        for b in range(3):
            w = jnp.exp(ms[b] - mx)
            tn = o_scr[b][sl, :] * w
            td = l_scr[b][sl, :] * w
            num = tn if num is None else num + tn
            den = td if den is None else den + td
        o_ref[sl, :] = (num / den).astype(o_ref.dtype)
        return carry

    lax.fori_loop(0, seq // cb, combine, 0)


def _attn_prompt(q, k, v, slopes):
    b, seq, _ = q.shape
    assert seq % (16 * LANES) == 0
    blk = pl.BlockSpec((None, seq, LANES), lambda bb, hp: (bb, 0, hp))
    body = functools.partial(_attn_prompt_body, seq=seq)
    return pl.pallas_call(
        body,
        grid=(b, D_A // LANES),
        in_specs=[pl.BlockSpec(memory_space=pltpu.SMEM), blk, blk, blk],
        out_specs=blk,
        out_shape=jax.ShapeDtypeStruct((b, seq, D_A), BF16),
        scratch_shapes=[pltpu.VMEM((seq, LANES), F32) for _ in range(9)]
        + [pltpu.VMEM((2 * LANES, 2 * LANES if seq // dil >= 2 * LANES else LANES), F32) for _, dil in BRANCHES],
        compiler_params=_cparams("parallel", "parallel"),
        name="attn_prompt",
    )(slopes, q, k, v)


def _attn_sample_body(slope_ref, q_ref, kn_ref, vn_ref, kt_ref, vt_ref, o_ref, bias_c, bias_n, *, t_new, w_buf):
    nt = (((1,), (1,)), ((), ()))

    @pl.when(pl.program_id(0) == 0)
    def _():
        def table(ref, ncols, dist0):
            rowi = lax.broadcasted_iota(jnp.int32, (t_new, ncols), 0)
            coli = lax.broadcasted_iota(jnp.int32, (t_new, ncols), 1)
            d = dist0 + rowi - coli
            mult = jnp.zeros((t_new, ncols), jnp.int32)
            for win, dil in BRANCHES:
                mult = mult + jnp.where((d >= 0) & (d <= win) & ((d & (dil - 1)) == 0), 1, 0)
            logm = jnp.where(mult == 3, math.log(3.0), jnp.where(mult == 2, math.log(2.0), 0.0))
            for h in range(H_A):
                ref[h] = jnp.where(mult > 0, logm - slope_ref[h] * d.astype(F32), NEG_BIG)

        table(bias_c, w_buf, w_buf)
        table(bias_n, t_new, 0)

    outs = []
    for h in range(H_A):
        cols = slice(h * DH_A, (h + 1) * DH_A)
        q = q_ref[:, cols].astype(BF16)
        s_c = jnp.dot(q, kt_ref[h].astype(BF16), preferred_element_type=F32) + bias_c[h]
        s_n = lax.dot_general(q, kn_ref[:, cols].astype(BF16), nt, preferred_element_type=F32) + bias_n[h]
        mx = jnp.maximum(jnp.max(s_c, axis=1, keepdims=True), jnp.max(s_n, axis=1, keepdims=True))
        p_c = jnp.exp(s_c - mx)
        p_n = jnp.exp(s_n - mx)
        den = jnp.sum(p_c, axis=1, keepdims=True) + jnp.sum(p_n, axis=1, keepdims=True)
        o = lax.dot_general(p_c.astype(BF16), vt_ref[h].astype(BF16), nt, preferred_element_type=F32)
        o = o + jnp.dot(p_n.astype(BF16), vn_ref[:, cols].astype(BF16), preferred_element_type=F32)
        outs.append(o / den)
    o_ref[...] = jnp.concatenate(outs, axis=1)


def _attn_sample(q, kn, vn, cache_k, cache_v, layer, slopes):
    b, t_new, _ = q.shape
    w_buf = cache_k.shape[2]
    assert w_buf >= WINDOW and t_new % SUBLANES == 0
    kt = jnp.transpose(cache_k, (0, 1, 3, 4, 2))
    vt = jnp.transpose(cache_v, (0, 1, 3, 4, 2))
    new_blk = pl.BlockSpec((None, t_new, D_A), lambda bb: (bb, 0, 0))
    cache_blk = pl.BlockSpec((None, None, H_A, DH_A, w_buf), lambda bb: (layer, bb, 0, 0, 0))
    body = functools.partial(_attn_sample_body, t_new=t_new, w_buf=w_buf)
    return pl.pallas_call(
        body,
        grid=(b,),
        in_specs=[pl.BlockSpec(memory_space=pltpu.SMEM), new_blk, new_blk, new_blk, cache_blk, cache_blk],
        out_specs=new_blk,
        out_shape=jax.ShapeDtypeStruct((b, t_new, D_A), F32),
        scratch_shapes=[pltpu.VMEM((H_A, t_new, w_buf), F32), pltpu.VMEM((H_A, t_new, t_new), F32)],
        compiler_params=_cparams("arbitrary"),
        name="attn_sample",
    )(slopes, q, kn, vn, kt, vt)


S5_UT = D_B // LANES
S5_TW = S5_N // S5_UT


def _gelu_tanh(x):
    c = math.sqrt(2.0 / math.pi)
    return 0.5 * x * (1.0 + jnp.tanh(c * (x + 0.044715 * (x * x * x))))


def _s5_body(u_ref, h0_ref, perm_ref, permt_ref, wb_ref, lam_ref, wc_ref, dskip_ref, wglu_ref, bglu_ref,
             o_ref, hl_ref, bu_scr, up_scr, st_scr, *, nb, tsub, nsub):
    step = pl.program_id(0)
    rsub = nb * tsub
    tstep = tsub * nsub

    @pl.when(step == 0)
    def _():
        st_scr[...] = h0_ref[...]

    for sb in range(nsub):
        u_nat = u_ref[:, sb * tsub:(sb + 1) * tsub, :].reshape(rsub, D_B)
        parts = [jnp.dot(perm_ref[...], p, preferred_element_type=F32) for p in _split3(u_nat)]
        up_scr[sb * rsub:(sb + 1) * rsub, :] = parts[0] + parts[1] + parts[2]
    for j in range(S5_UT):
        uj = up_scr[:, j * LANES:(j + 1) * LANES].astype(BF16)
        r = jnp.dot(uj, wb_ref[j], preferred_element_type=F32)
        bu_scr[:, j * S5_TW:(j + 1) * S5_TW] = r[:, :S5_TW]
        bu_scr[:, S5_N + j * S5_TW:S5_N + (j + 1) * S5_TW] = r[:, S5_TW:]

    tiles_per_pass = 4
    for pg in range(S5_N // LANES // tiles_per_pass):
        cols = [(pg * tiles_per_pass + i) * LANES for i in range(tiles_per_pass)]
        lr = [jnp.broadcast_to(lam_ref[0:1, c:c + LANES], (nb, LANES)) for c in cols]
        li = [jnp.broadcast_to(lam_ref[1:2, c:c + LANES], (nb, LANES)) for c in cols]
        init = tuple(st_scr[:, c:c + LANES] for c in cols) + tuple(st_scr[:, S5_N + c:S5_N + c + LANES] for c in cols)

        def tick(t, carry, cols=cols, lr=lr, li=li):
            row = pl.ds(t * nb, nb)
            hr, hi = carry[:tiles_per_pass], carry[tiles_per_pass:]
            nr, ni = [], []
            for i, c in enumerate(cols):
                br = bu_scr[row, c:c + LANES]
                bi = bu_scr[row, S5_N + c:S5_N + c + LANES]
                r_new = lr[i] * hr[i] - li[i] * hi[i] + br
                i_new = lr[i] * hi[i] + li[i] * hr[i] + bi
                bu_scr[row, c:c + LANES] = r_new
                bu_scr[row, S5_N + c:S5_N + c + LANES] = i_new
                nr.append(r_new)
                ni.append(i_new)
            return tuple(nr) + tuple(ni)

        fin = init
        for t in range(tstep):
            fin = tick(t, fin)
        for i, c in enumerate(cols):
            st_scr[:, c:c + LANES] = fin[i]
            st_scr[:, S5_N + c:S5_N + c + LANES] = fin[tiles_per_pass + i]

    hl_ref[...] = st_scr[...]

    ys = []
    for j in range(S5_UT):
        hre = bu_scr[:, j * S5_TW:(j + 1) * S5_TW].astype(BF16)
        him = bu_scr[:, S5_N + j * S5_TW:S5_N + (j + 1) * S5_TW].astype(BF16)
        y = jnp.dot(hre, wc_ref[j, :S5_TW, :], preferred_element_type=F32)
        y = y + jnp.dot(him, wc_ref[j, S5_TW:, :], preferred_element_type=F32)
        ys.append(y)
    y = jnp.concatenate(ys, axis=1) + dskip_ref[...] * up_scr[...]
    g = _gelu_tanh(y)
    gate = jnp.dot(g.astype(BF16), wglu_ref[...], preferred_element_type=F32) + bglu_ref[...]
    out = (g * jax.nn.sigmoid(gate)).astype(BF16)
    for sb in range(nsub):
        o_nat = jnp.dot(permt_ref[...], out[sb * rsub:(sb + 1) * rsub, :], preferred_element_type=F32)
        o_ref[:, sb * tsub:(sb + 1) * tsub, :] = o_nat.reshape(nb, tsub, D_B).astype(o_ref.dtype)


def _s5_mixer(u, h0, prm, tsub, nsub):
    nb, seq, _ = u.shape
    tstep = tsub * nsub
    rsub = nb * tsub
    rows = nb * tstep
    assert seq % tstep == 0 and nb % SUBLANES == 0 and tsub % SUBLANES == 0
    perm = np.zeros((rsub, rsub), np.float32)
    for t in range(tsub):
        for bb in range(nb):
            perm[t * nb + bb, bb * tsub + t] = 1.0
    body = functools.partial(_s5_body, nb=nb, tsub=tsub, nsub=nsub)
    return pl.pallas_call(
        body,
        grid=(seq // tstep,),
        in_specs=[pl.BlockSpec((nb, tstep, D_B), lambda i: (0, i, 0)),
                  _const_spec((nb, 2 * S5_N)),
                  _const_spec((rsub, rsub)), _const_spec((rsub, rsub)),
                  _const_spec((S5_UT, LANES, 2 * S5_TW)),
                  _const_spec((2, S5_N)),
                  _const_spec((S5_UT, 2 * S5_TW, LANES)),
                  _const_spec((1, D_B)),
                  _const_spec((D_B, D_B)),
                  _const_spec((1, D_B))],
        out_specs=[pl.BlockSpec((nb, tstep, D_B), lambda i: (0, i, 0)),
                   pl.BlockSpec((nb, 2 * S5_N), lambda i: (0, 0))],
        out_shape=[jax.ShapeDtypeStruct((nb, seq, D_B), _act_dtype(tsub)),
                   jax.ShapeDtypeStruct((nb, 2 * S5_N), F32)],
        scratch_shapes=[pltpu.VMEM((rows, 2 * S5_N), F32),
                        pltpu.VMEM((rows, D_B), F32),
                        pltpu.VMEM((nb, 2 * S5_N), F32)],
        compiler_params=_cparams("arbitrary"),
        name="s5_mixer",
    )(u, h0, jnp.asarray(perm, BF16), jnp.asarray(perm.T, BF16), prm["wb"], prm["lam"], prm["wc"],
      prm["dskip"], prm["wglu"], prm["bglu"])


def _s5_params(lam_re, lam_im, log_dt, b_re, b_im, c_re, c_im, d_skip, w_glu, b_glu):
    lam = lax.complex(lam_re.astype(F32), lam_im.astype(F32))
    dt = jnp.exp(log_dt.astype(F32))[:, None]
    lam_bar = jnp.exp(lam * dt)
    b_bar = ((lam_bar - 1.0) / lam)[..., None] * lax.complex(b_re.astype(F32), b_im.astype(F32))
    gpt = LANES // CH_B
    eye = jnp.eye(gpt, dtype=F32)

    def b_blocks(part):
        x = part.reshape(S5_UT, gpt, P_B, CH_B)
        x = jnp.einsum("jgpc,gh->jgchp", x, eye)
        return x.reshape(S5_UT, gpt * CH_B, gpt * P_B)

    def c_blocks(part):
        x = part.reshape(S5_UT, gpt, CH_B, P_B)
        x = jnp.einsum("jgcp,gh->jgphc", x, eye)
        return x.reshape(S5_UT, gpt * P_B, gpt * CH_B)

    wb = jnp.concatenate([b_blocks(b_bar.real), b_blocks(b_bar.imag)], axis=2).astype(BF16)
    wc = jnp.concatenate([c_blocks(c_re.astype(F32)), -c_blocks(c_im.astype(F32))], axis=1).astype(BF16)
    lam2 = jnp.stack([lam_bar.real.reshape(S5_N), lam_bar.imag.reshape(S5_N)])
    return dict(wb=wb, wc=wc, lam=lam2, dskip=d_skip.astype(F32).reshape(1, D_B),
                wglu=w_glu.astype(BF16), bglu=b_glu.astype(F32).reshape(1, D_B))


HIST = SUBLANES
SSD_SHORT = 16
SSD_CHUNKS_PER_STEP = 4
XB0 = D_INNER
XC0 = D_INNER + G_C * N_C
GW = D_INNER // G_C


def _expand_heads(a, e_ref):
    e = e_ref[...]
    a1, a2, _ = _split3(a)
    return jnp.dot(a1, e, preferred_element_type=F32) + jnp.dot(a2, e, preferred_element_type=F32)


def _ssd_body(z_ref, xbc_ref, dt_ref, ssm0_ref, a_ref, dexp_ref, gw_ref, e_ref, *rest, lc, activated, nsub):
    if activated:
        y_ref, hl_ref, state = rest[-3:]
    else:
        conv0_ref, convw_ref, convb_ref, dtb_ref = rest[:4]
        y_ref, convst_ref, hl_ref, xpad, state = rest[-5:]
    ch = pl.program_id(1)
    cl = SSD_CHUNK if lc > SSD_SHORT else SSD_SHORT

    @pl.when(ch == 0)
    def _():
        state[...] = ssm0_ref[...]

    if activated:
        assert lc == nsub * cl
    else:
        @pl.when(ch == 0)
        def _():
            xpad[0:HIST, :] = conv0_ref[...]

        if lc < cl:
            xpad[HIST:HIST + cl, :] = jnp.zeros((cl, CONV_DIM), F32)
        xpad[HIST:HIST + lc, :] = xbc_ref[...]
        conv = convb_ref[...] + convw_ref[CONV_K - 1:CONV_K, :] * xpad[HIST:HIST + cl, :]
        for kk in range(CONV_K - 1):
            sh = CONV_K - 1 - kk
            conv = conv + convw_ref[kk:kk + 1, :] * xpad[HIST - sh:HIST - sh + cl, :]
        xs = conv * jax.nn.sigmoid(conv)
        tail = xpad[lc:lc + HIST, :]
        xpad[0:HIST, :] = tail
        convst_ref[...] = tail

        dt = jax.nn.softplus(dt_ref[...] + dtb_ref[...])
        if lc < cl:
            dt = jnp.concatenate([dt, jnp.zeros((cl - lc, LANES), F32)], axis=0)
    def chunk(xs, dt, zact):
        x = xs[:, :D_INNER]
        bgs, cbs, y_offs = [], [], []
        for g in range(G_C):
            bg = xs[:, XB0 + g * N_C:XB0 + (g + 1) * N_C].astype(BF16)
            cg = xs[:, XC0 + g * N_C:XC0 + (g + 1) * N_C].astype(BF16)
            prev = state[g * GW:(g + 1) * GW, :]
            bgs.append(bg)
            cbs.append(lax.dot_general(cg, bg, (((1,), (1,)), ((), ())), preferred_element_type=F32))
            y_offs.append(lax.dot_general(cg, prev.astype(BF16), (((1,), (1,)), ((), ())),
                                          preferred_element_type=F32))

        da = dt * a_ref[...]
        ri = lax.broadcasted_iota(jnp.int32, (cl, cl), 0)
        ci = lax.broadcasted_iota(jnp.int32, (cl, cl), 1)
        tri = ri >= ci
        tri01 = jnp.where(tri, 1.0, 0.0).astype(BF16)
        cs = [jnp.dot(tri01, p, preferred_element_type=F32) for p in _split3(da)]
        a_cs = cs[0] + cs[1] + cs[2]
        a_cs_t = a_cs.T
        dt_t = dt.T
        e_cs = jnp.exp(a_cs)
        tot = a_cs[cl - 1:cl, :]
        w_end = jnp.exp(tot - a_cs) * dt
        both = _expand_heads(jnp.concatenate([e_cs, w_end], axis=0), e_ref)
        scale_x, wend_x = both[:cl], both[cl:]

        xw = (x * wend_x).astype(BF16)
        lane = lax.broadcasted_iota(jnp.int32, (cl, LANES), 1)
        head0 = lane < P_C
        y_parts = []
        for g in range(G_C):
            bg, cb, y_off = bgs[g], cbs[g], y_offs[g]
            for jp in range(J_C // 2):
                ms = []
                for hh in range(2):
                    h = g * J_C + 2 * jp + hh
                    seg = a_cs[:, h:h + 1] - a_cs_t[h:h + 1, :]
                    dec = jnp.exp(jnp.where(tri, seg, NEG_BIG))
                    ms.append((cb * dec * dt_t[h:h + 1, :]).astype(BF16))
                c0 = g * GW + jp * LANES
                xp = x[:, c0:c0 + LANES].astype(BF16)
                r = jnp.dot(jnp.concatenate(ms, axis=0), xp, preferred_element_type=F32)
                yd = jnp.where(head0, r[:cl], r[cl:])
                y_parts.append(yd + y_off[:, jp * LANES:(jp + 1) * LANES] * scale_x[:, c0:c0 + LANES])
            new = lax.dot_general(xw[:, g * GW:(g + 1) * GW], bg, (((0,), (0,)), ((), ())),
                                  preferred_element_type=F32)
            for j in range(J_C):
                h = g * J_C + j
                r0 = g * GW + j * P_C
                state[r0:r0 + P_C, :] = state[r0:r0 + P_C, :] * jnp.exp(tot[:, h:h + 1]) + new[j * P_C:(j + 1) * P_C, :]

        y = jnp.concatenate(y_parts, axis=1) + dexp_ref[...] * x
        y = y[:lc] if lc < cl else y
        gt = y * zact
        outs = []
        for g in range(G_C):
            gg = gt[:, g * GW:(g + 1) * GW]
            ms = jnp.mean(gg * gg, axis=-1, keepdims=True)
            outs.append(gg * lax.rsqrt(ms + RMS_EPS))
        return jnp.concatenate(outs, axis=1) * gw_ref[...]

    if activated:
        for sc in range(nsub):
            rows = slice(sc * cl, (sc + 1) * cl)
            y_ref[rows, :] = chunk(xbc_ref[rows, :], dt_ref[rows, :], z_ref[rows, :]).astype(y_ref.dtype)
    else:
        z = z_ref[...]
        y_ref[...] = chunk(xs, dt, z * jax.nn.sigmoid(z)).astype(y_ref.dtype)
    hl_ref[...] = state[...]


def _ssd_mixer(z, xbc, dtr, conv0, ssm0, ssm_layer, prm, stack=None, layer=0, n_layers=1, activated=False):
    b, seq, _ = z.shape
    lc = SSD_CHUNK if seq % SSD_CHUNK == 0 else seq
    assert lc % SUBLANES == 0 and CONV_K - 1 <= lc <= SSD_CHUNK
    nsub = 1
    if activated:
        nsub = SSD_CHUNKS_PER_STEP if seq % (SSD_CHUNKS_PER_STEP * SSD_CHUNK) == 0 else 1
        lc = nsub * SSD_CHUNK
    nch = seq // lc
    body = functools.partial(_ssd_body, lc=lc, activated=activated, nsub=nsub)
    keep = (lambda items: [it for i, it in enumerate(items) if i != 1]) if activated else (lambda items: items)
    conv_args = [] if activated else [conv0, prm["convw"], prm["convb"], prm["dtb"]]

    def row_blk(n):
        return pl.BlockSpec((None, lc, n), lambda bb, c: (bb, c, 0))

    def per_b(r, n):
        return pl.BlockSpec((None, r, n), lambda bb, c: (bb, 0, 0))

    conv_specs = [] if activated else [per_b(HIST, CONV_DIM), _const_spec((CONV_K, CONV_DIM)),
                                       _const_spec((1, CONV_DIM)), _const_spec((1, LANES))]

    return pl.pallas_call(
        body,
        grid=(b, nch),
        in_specs=[row_blk(D_INNER), row_blk(CONV_DIM), row_blk(LANES),
                  pl.BlockSpec((None, None, D_INNER, N_C), lambda bb, c: (ssm_layer, bb, 0, 0)),
                  _const_spec((1, LANES)), _const_spec((1, D_INNER)), _const_spec((1, D_INNER)),
                  _const_spec((LANES, D_INNER))] + conv_specs
        + ([] if stack is None else [pl.BlockSpec(memory_space=pl.ANY)]),
        out_specs=keep([row_blk(D_INNER), per_b(HIST, CONV_DIM),
                        pl.BlockSpec((None, D_INNER, N_C), lambda bb, c: (layer * b + bb, 0, 0))]),
        out_shape=keep([jax.ShapeDtypeStruct((b, seq, D_INNER), _act_dtype(lc)),
                        jax.ShapeDtypeStruct((b, HIST, CONV_DIM), F32),
                        jax.ShapeDtypeStruct((n_layers * b, D_INNER, N_C), F32)]),
        scratch_shapes=([] if activated else [pltpu.VMEM((HIST + SSD_CHUNK, CONV_DIM), F32)])
        + [pltpu.VMEM((D_INNER, N_C), F32)],
        input_output_aliases={} if stack is None else {8 + len(conv_args): 1 if activated else 2},
        compiler_params=_cparams("parallel", "arbitrary"),
        name="ssd_mixer",
    )(z, xbc, dtr, ssm0, prm["a"], prm["dexp"], prm["gw"], prm["expand"], *conv_args,
      *(() if stack is None else (stack,)))


def _ssd_params(conv_w, conv_b, dt_bias, a_log, d_skip, gnorm_w):
    pad = LANES - H_C
    expand = np.zeros((LANES, D_INNER), np.float32)
    for h in range(H_C):
        expand[h, h * P_C:(h + 1) * P_C] = 1.0
    return dict(
        convw=conv_w.astype(F32), convb=conv_b.astype(F32).reshape(1, CONV_DIM),
        dtb=jnp.pad(dt_bias.astype(F32), (0, pad)).reshape(1, LANES),
        a=jnp.pad(-jnp.exp(a_log.astype(F32)), (0, pad)).reshape(1, LANES),
        dexp=jnp.repeat(d_skip.astype(F32), P_C).reshape(1, D_INNER),
        gw=gnorm_w.astype(F32).reshape(1, D_INNER),
        expand=jnp.asarray(expand, BF16))


def _trunk(x, p, slopes, caches, tm):
    b, seq, _ = x.shape
    m = b * seq
    x2 = x.reshape(m, D_MODEL)
    new_s5, new_conv = [], []
    kv_stacks, ssm_stack = None, None
    kv_transposed = seq % tm == 0
    n_even, n_odd = (DEPTH + 1) // 2, DEPTH // 2
    for i in range(DEPTH):
        j = i // 2
        wn_pre = p["norm_mix_pre"][i].reshape(1, D_MODEL)
        if i % 2 == 0:
            assert seq <= WINDOW
            q, k, v, u, *kv_stacks = _norm_proj(x2, wn_pre, p["w_in_even"], j, (D_A, D_A, D_A, D_B),
                                                (DH_A ** -0.5, 1.0, 1.0, 1.0), tm, head_splits=(1, 2),
                                                stacks=kv_stacks, layer=j, n_layers=n_even,
                                                seq=seq if kv_transposed else None)
            q3, k3, v3 = (t.reshape(b, seq, D_A) for t in (q, k, v))
            if caches is None:
                o_a = _attn_prompt(q3, k3, v3, slopes)
                h0 = jnp.zeros((b, 2 * S5_N), F32)
                tsub, nsub = S5_TSUB, S5_NSUB
            else:
                o_a = _attn_sample(q3, k3, v3, caches["k"], caches["v"], j, slopes)
                s0 = caches["s5"][j].astype(F32)
                h0 = jnp.concatenate([s0[..., 0].reshape(b, S5_N), s0[..., 1].reshape(b, S5_N)], axis=1)
                tsub, nsub = seq, 1
            o_b, hl = _s5_mixer(u.reshape(b, seq, D_B), h0, p["s5"][j], tsub, nsub)
            new_s5.append(jnp.stack([hl[:, :S5_N].reshape(b, G_B, P_B), hl[:, S5_N:].reshape(b, G_B, P_B)], axis=-1))
            acts = [o_a.reshape(m, D_A), o_b.reshape(m, D_B)]
            w_out = p["w_out_even"]
        else:
            if caches is None and seq % tm == 0:
                tmo = tm
                z, xbc, dtr, tails = _norm_proj_ssd(x2, wn_pre, p["w_in_odd"], j, p["ssd"][j], tmo, seq)
                ssm0 = jnp.zeros((1, b, D_INNER, N_C), F32)
                yg, ssm_stack = _ssd_mixer(z.reshape(b, seq, D_INNER), xbc.reshape(b, seq, CONV_DIM),
                                           dtr.reshape(b, seq, LANES), None, ssm0, 0, p["ssd"][j],
                                           stack=ssm_stack, layer=j, n_layers=n_odd, activated=True)
                convst = tails.reshape(b, seq // tmo, HIST, CONV_DIM)[:, -1]
            else:
                z, xbc, dtr = _norm_proj(x2, wn_pre, p["w_in_odd"], j, (D_INNER, CONV_DIM, LANES),
                                         (1.0, 1.0, 1.0), tm)
                if caches is None:
                    conv0 = jnp.zeros((b, HIST, CONV_DIM), F32)
                    ssm0, ssm_layer = jnp.zeros((1, b, D_INNER, N_C), F32), 0
                else:
                    conv0 = jnp.pad(caches["conv"][j].astype(F32), ((0, 0), (HIST - (CONV_K - 1), 0), (0, 0)))
                    ssm0, ssm_layer = caches["ssm"].astype(F32).reshape(-1, b, D_INNER, N_C), j
                yg, convst, ssm_stack = _ssd_mixer(z.reshape(b, seq, D_INNER), xbc.reshape(b, seq, CONV_DIM),
                                                   dtr.reshape(b, seq, LANES), conv0, ssm0, ssm_layer, p["ssd"][j],
                                                   stack=ssm_stack, layer=j, n_layers=n_odd)
            new_conv.append(convst[:, HIST - (CONV_K - 1):])
            acts = [yg.reshape(m, D_INNER)]
            w_out = p["w_out_odd"]
        x2 = _proj_mlp(acts, w_out, j, x2, p["norm_mix_post"][i].reshape(1, D_MODEL),
                       p["norm_mlp_pre"][i].reshape(1, D_MODEL), p["w_mlp_up"], p["w_mlp_down"], i,
                       p["norm_mlp_post"][i].reshape(1, D_MODEL), tm)
    if kv_transposed:
        k_all, v_all = (jnp.transpose(t, (0, 1, 4, 2, 3)) for t in kv_stacks)
    else:
        k_all, v_all = (t.reshape(n_even, b, seq, H_A, DH_A) for t in kv_stacks)
    return (x2.reshape(b, seq, D_MODEL), k_all, v_all, jnp.stack(new_s5), jnp.stack(new_conv),
            ssm_stack.reshape(n_odd, b, H_C, P_C, N_C))


def kernel(x_prompt, x_sample, cache_k, cache_v, state_s5, state_conv, state_ssm, norm_mix_pre, norm_mix_post, norm_mlp_pre, norm_mlp_post, w_mlp_up, w_mlp_down, w_in_even, w_out_even, s5_lambda_re, s5_lambda_im, s5_log_dt, s5_b_re, s5_b_im, s5_c_re, s5_c_im, s5_d, s5_w_glu, s5_b_glu, w_in_odd, conv_w, conv_b, dt_bias, a_log, d_skip, gnorm_w, w_out_odd):
    n_even, n_odd = w_in_even.shape[0], w_in_odd.shape[0]
    odd_pad = LANES - H_C
    p = dict(
        norm_mix_pre=norm_mix_pre.astype(F32), norm_mix_post=norm_mix_post.astype(F32),
        norm_mlp_pre=norm_mlp_pre.astype(F32), norm_mlp_post=norm_mlp_post.astype(F32),
        w_mlp_up=w_mlp_up.astype(BF16), w_mlp_down=w_mlp_down.astype(BF16),
        w_in_even=w_in_even, w_out_even=w_out_even,
        w_in_odd=jnp.pad(w_in_odd.astype(BF16), ((0, 0), (0, 0), (0, odd_pad))),
        w_out_odd=w_out_odd,
        s5=[_s5_params(s5_lambda_re[j], s5_lambda_im[j], s5_log_dt[j], s5_b_re[j], s5_b_im[j], s5_c_re[j],
                       s5_c_im[j], s5_d[j], s5_w_glu[j], s5_b_glu[j]) for j in range(n_even)],
        ssd=[_ssd_params(conv_w[j], conv_b[j], dt_bias[j], a_log[j], d_skip[j], gnorm_w[j]) for j in range(n_odd)],
    )
    slopes = jnp.asarray(np.power(2.0, -8.0 * np.arange(1, H_A + 1) / H_A), dtype=F32)
    caches = dict(k=cache_k, v=cache_v, s5=state_s5, conv=state_conv, ssm=state_ssm)
    y_p, k_p, v_p, s5_p, conv_p, ssm_p = _trunk(x_prompt, p, slopes, None, tm=TM_LONG)
    y_s, k_s, v_s, s5_s, conv_s, ssm_s = _trunk(x_sample, p, slopes, caches, tm=TM_SHORT)
    return (y_p, y_s, k_p, v_p, s5_p, conv_p, ssm_p, k_s, v_s, s5_s, conv_s, ssm_s)
```
